```python
import jax, jax.numpy as jnp
from jax import lax
import numpy as np

D_MODEL = 1024
BATCH = 32
SEQ = 256
DEPTH = 4
DEC_BATCH = 8
DEC_SEQ = 2048
PAST_LEN = 256

GRID_W = 64
N_HEADS = 16
N_KV_HEADS = 4
HEAD_DIM = D_MODEL // N_HEADS
GROUP = N_HEADS // N_KV_HEADS
Q_DIM = N_HEADS * HEAD_DIM
KV_DIM = N_KV_HEADS * HEAD_DIM
QKV_DIM = Q_DIM + 2 * KV_DIM
Q_BLOCK = 128
WINDOW = 128
ROPE_THETA = 10000.0
D_FF = 2816
N_EXPERTS = 8
TOP_K = 2
D_FF_EXPERT = 3584
N_MOD = 6
N_A = (DEPTH + 1) // 2
N_B = DEPTH // 2
N_DENSE = (DEPTH + 1) // 2
N_MOE = DEPTH // 2
EPS = 1e-6
NEG_INF = -1e30
ATTN_SCALE = HEAD_DIM ** -0.5

kernel_name = "hybrid_diffusion_prefix_trunk_step"


def rmsnorm(x, g):
    xf = x.astype(jnp.float32)
    y = xf * lax.rsqrt(jnp.mean(xf * xf, axis=-1, keepdims=True) + EPS)
    return (y * g.astype(jnp.float32)).astype(x.dtype)


def ada_params(cvec, w, b):
    m = jax.nn.silu(cvec) @ w + b
    return jnp.split(m, N_MOD, axis=-1)


def modulate(h, shift, scale):
    return h * (1 + scale) + shift


def axial_rope_tables(n_tokens):
    n_rows = n_tokens // GRID_W
    rows = jnp.repeat(jnp.arange(n_rows), GRID_W).astype(jnp.float32)
    cols = jnp.tile(jnp.arange(GRID_W), n_rows).astype(jnp.float32)
    pairs_per_axis = HEAD_DIM // 4
    inv = ROPE_THETA ** (-jnp.arange(pairs_per_axis, dtype=jnp.float32) / pairs_per_axis)
    ang = jnp.concatenate([rows[:, None] * inv, cols[:, None] * inv], axis=-1)
    return jnp.cos(ang), jnp.sin(ang)


def apply_rope(x, cos, sin):
    xf = x.astype(jnp.float32).reshape(x.shape[:-1] + (HEAD_DIM // 2, 2))
    shp = (1, cos.shape[0]) + (1,) * (x.ndim - 3) + (HEAD_DIM // 2,)
    c = cos.reshape(shp)
    s = sin.reshape(shp)
    x0, x1 = xf[..., 0], xf[..., 1]
    out = jnp.stack([x0 * c - x1 * s, x0 * s + x1 * c], axis=-1).reshape(x.shape)
    return out.astype(x.dtype)


def qkv_proj(h, w):
    b, s = h.shape[:2]
    qkv = h @ w
    q, k, v = jnp.split(qkv, [Q_DIM, Q_DIM + KV_DIM], axis=-1)
    return (q.reshape(b, s, N_KV_HEADS, GROUP, HEAD_DIM),
            k.reshape(b, s, N_KV_HEADS, HEAD_DIM),
            v.reshape(b, s, N_KV_HEADS, HEAD_DIM))


def sink_column(sink, s):
    sk = sink.astype(jnp.float32).reshape(N_KV_HEADS, GROUP)[None, :, :, None, None]
    return jnp.broadcast_to(sk, s.shape[:-1] + (1,))


def context_attention(q, k, v, sink):
    b, s = q.shape[:2]
    sc = jnp.einsum('bqkgd,bskd->bkgqs', q, k).astype(jnp.float32) * ATTN_SCALE
    if sink is not None:
        sc = jnp.concatenate([sc, sink_column(sink, sc)], axis=-1)
        p = jax.nn.softmax(sc, axis=-1)[..., :-1]
    else:
        p = jax.nn.softmax(sc, axis=-1)
    o = jnp.einsum('bkgqs,bskd->bqkgd', p.astype(v.dtype), v)
    return o.reshape(b, s, D_MODEL)


def latent_global_attention(q, k, v, k_ctx, v_ctx):
    b, s = q.shape[:2]
    nb = s // Q_BLOCK
    k_all = jnp.concatenate([k, k_ctx.astype(k.dtype)], axis=1)
    v_all = jnp.concatenate([v, v_ctx.astype(v.dtype)], axis=1)
    qb = q.reshape(b, nb, Q_BLOCK, N_KV_HEADS, GROUP, HEAD_DIM).transpose(1, 0, 2, 3, 4, 5)

    def block(qi):
        sc = jnp.einsum('bqkgd,bskd->bkgqs', qi, k_all).astype(jnp.float32) * ATTN_SCALE
        p = jax.nn.softmax(sc, axis=-1).astype(v_all.dtype)
        return jnp.einsum('bkgqs,bskd->bqkgd', p, v_all)

    o = lax.map(block, qb)
    return o.transpose(1, 0, 2, 3, 4, 5).reshape(b, s, D_MODEL)


def latent_window_attention(q, k, v, k_ctx, v_ctx, sink):
    b, s = q.shape[:2]
    nb = s // Q_BLOCK
    span = Q_BLOCK + 2 * WINDOW
    pad = ((0, 0), (WINDOW, WINDOW), (0, 0), (0, 0))
    kp = jnp.pad(k, pad)
    vp = jnp.pad(v, pad)
    k_ctx = k_ctx.astype(k.dtype)
    v_ctx = v_ctx.astype(v.dtype)
    qb = q.reshape(b, nb, Q_BLOCK, N_KV_HEADS, GROUP, HEAD_DIM).transpose(1, 0, 2, 3, 4, 5)

    def block(args):
        j, qi = args
        start = j * Q_BLOCK
        kw = lax.dynamic_slice_in_dim(kp, start, span, axis=1)
        vw = lax.dynamic_slice_in_dim(vp, start, span, axis=1)
        qpos = start + jnp.arange(Q_BLOCK)
        kpos = start - WINDOW + jnp.arange(span)
        valid = ((kpos >= 0) & (kpos < s))[None, :] & (jnp.abs(qpos[:, None] - kpos[None, :]) <= WINDOW)
        s_loc = jnp.einsum('bqkgd,bskd->bkgqs', qi, kw).astype(jnp.float32) * ATTN_SCALE
        s_loc = jnp.where(valid[None, None, None], s_loc, NEG_INF)
        s_ctx = jnp.einsum('bqkgd,bskd->bkgqs', qi, k_ctx).astype(jnp.float32) * ATTN_SCALE
        sc = jnp.concatenate([s_loc, s_ctx, sink_column(sink, s_loc)], axis=-1)
        p = jax.nn.softmax(sc, axis=-1).astype(v.dtype)
        return (jnp.einsum('bkgqs,bskd->bqkgd', p[..., :span], vw)
                + jnp.einsum('bkgqs,bskd->bqkgd', p[..., span:-1], v_ctx))

    o = lax.map(block, (jnp.arange(nb), qb))
    return o.transpose(1, 0, 2, 3, 4, 5).reshape(b, s, D_MODEL)


def swiglu(h, w_up, w_down):
    g, u = jnp.split(h @ w_up, 2, axis=-1)
    return (jax.nn.silu(g) * u) @ w_down


def moe_swiglu(h, w_router, w_up, w_down):
    b, s, d = h.shape
    t = h.reshape(b * s, d)
    logits = (t @ w_router).astype(jnp.float32)
    top_v, top_i = lax.top_k(logits, TOP_K)
    gates = jax.nn.softmax(top_v, axis=-1)
    combine = jnp.sum(jax.nn.one_hot(top_i, N_EXPERTS, dtype=jnp.float32) * gates[..., None], axis=1)
    combine = combine.astype(h.dtype)
    out = jnp.zeros_like(t)
    for e in range(N_EXPERTS):
        out = out + combine[:, e:e + 1] * swiglu(t, w_up[e], w_down[e])
    return out.reshape(b, s, d)


def setup_inputs(seed: int = 0) -> dict:
    key = jax.random.key(seed)
    ks = jax.random.split(key, 24)
    f32 = jnp.float32
    nrm = lambda k, shp: jax.random.normal(k, shp, f32)
    return {
        "x_prompt": nrm(ks[0], (BATCH, SEQ, D_MODEL)),
        "x_sample": nrm(ks[1], (DEC_BATCH, DEC_SEQ, D_MODEL)),
        "cache_k": nrm(ks[2], (DEC_BATCH, DEPTH, PAST_LEN, N_KV_HEADS, HEAD_DIM)),
        "cache_v": nrm(ks[3], (DEC_BATCH, DEPTH, PAST_LEN, N_KV_HEADS, HEAD_DIM)),
        "c": nrm(ks[4], (DEC_BATCH, D_MODEL)),
        "c_ctx": nrm(ks[5], (D_MODEL,)),
        "w_ada": nrm(ks[6], (DEPTH, D_MODEL, N_MOD * D_MODEL)) * (0.5 * D_MODEL ** -0.5),
        "b_ada": nrm(ks[7], (DEPTH, N_MOD * D_MODEL)) * 0.01,
        "norm_mix": 1.0 + 0.05 * nrm(ks[8], (DEPTH, D_MODEL)),
        "norm_ffn": 1.0 + 0.05 * nrm(ks[9], (DEPTH, D_MODEL)),
        "norm_final": 1.0 + 0.05 * nrm(ks[10], (D_MODEL,)),
        "w_qkv": nrm(ks[11], (DEPTH, D_MODEL, QKV_DIM)) * D_MODEL ** -0.5,
        "w_o": nrm(ks[12], (DEPTH, D_MODEL, D_MODEL)) * D_MODEL ** -0.5,
        "q_norm": 1.0 + 0.05 * nrm(ks[13], (N_A, HEAD_DIM)),
        "k_norm": 1.0 + 0.05 * nrm(ks[14], (N_A, HEAD_DIM)),
        "sinks": 0.5 * nrm(ks[15], (N_B, N_HEADS)),
        "w_ffn_up": nrm(ks[16], (N_DENSE, D_MODEL, 2 * D_FF)) * D_MODEL ** -0.5,
        "w_ffn_down": nrm(ks[17], (N_DENSE, D_FF, D_MODEL)) * D_FF ** -0.5,
        "w_router": nrm(ks[18], (N_MOE, D_MODEL, N_EXPERTS)) * D_MODEL ** -0.5,
        "w_exp_up": nrm(ks[19], (N_MOE, N_EXPERTS, D_MODEL, 2 * D_FF_EXPERT)) * D_MODEL ** -0.5,
        "w_exp_down": nrm(ks[20], (N_MOE, N_EXPERTS, D_FF_EXPERT, D_MODEL)) * D_FF_EXPERT ** -0.5,
    }


def reference(x_prompt, x_sample, cache_k, cache_v, c, c_ctx, w_ada, b_ada, norm_mix, norm_ffn,
              norm_final, w_qkv, w_o, q_norm, k_norm, sinks, w_ffn_up, w_ffn_down, w_router,
              w_exp_up, w_exp_down):
    xp = x_prompt
    xs = x_sample
    cos, sin = axial_rope_tables(xs.shape[1])
    new_k = []
    new_v = []
    for l in range(DEPTH):
        idx = l // 2
        use_a = (l % 2 == 0)
        p_sh1, p_sc1, p_g1, p_sh2, p_sc2, p_g2 = ada_params(c_ctx, w_ada[l], b_ada[l])
        s_mods = ada_params(c, w_ada[l], b_ada[l])
        s_sh1, s_sc1, s_g1, s_sh2, s_sc2, s_g2 = [m[:, None, :] for m in s_mods]

        h = modulate(rmsnorm(xp, norm_mix[l]), p_sh1, p_sc1)
        q, k, v = qkv_proj(h, w_qkv[l])
        if use_a:
            q = rmsnorm(q, q_norm[idx])
            k = rmsnorm(k, k_norm[idx])
            o = context_attention(q, k, v, None)
        else:
            o = context_attention(q, k, v, sinks[idx])
        new_k.append(k)
        new_v.append(v)
        xp = xp + p_g1 * (o @ w_o[l])
        h = modulate(rmsnorm(xp, norm_ffn[l]), p_sh2, p_sc2)
        if use_a:
            f = swiglu(h, w_ffn_up[idx], w_ffn_down[idx])
        else:
            f = moe_swiglu(h, w_router[idx], w_exp_up[idx], w_exp_down[idx])
        xp = xp + p_g2 * f

        k_ctx = cache_k[:, l]
        v_ctx = cache_v[:, l]
        h = modulate(rmsnorm(xs, norm_mix[l]), s_sh1, s_sc1)
        q, k, v = qkv_proj(h, w_qkv[l])
        if use_a:
            q = apply_rope(rmsnorm(q, q_norm[idx]), cos, sin)
            k = apply_rope(rmsnorm(k, k_norm[idx]), cos, sin)
            o = latent_global_attention(q, k, v, k_ctx, v_ctx)
        else:
            q = apply_rope(q, cos, sin)
            k = apply_rope(k, cos, sin)
            o = latent_window_attention(q, k, v, k_ctx, v_ctx, sinks[idx])
        xs = xs + s_g1 * (o @ w_o[l])
        h = modulate(rmsnorm(xs, norm_ffn[l]), s_sh2, s_sc2)
        if use_a:
            f = swiglu(h, w_ffn_up[idx], w_ffn_down[idx])
        else:
            f = moe_swiglu(h, w_router[idx], w_exp_up[idx], w_exp_down[idx])
        xs = xs + s_g2 * f

    y_prompt = rmsnorm(xp, norm_final)
    y_sample = rmsnorm(xs, norm_final)
    k_state = jnp.stack(new_k, axis=1)
    v_state = jnp.stack(new_v, axis=1)
    return (y_prompt, y_sample, k_state, v_state)
```

```python
import functools

import jax
import jax.numpy as jnp
from jax import lax
from jax.experimental import pallas as pl
from jax.experimental.pallas import tpu as pltpu

N_HEADS = 16
N_KV_HEADS = 4
GROUP = N_HEADS // N_KV_HEADS
HEAD_DIM = 64
GRID_W = 64
WINDOW = 128
ROPE_THETA = 10000.0
N_MOD = 6
TOP_K = 2
EPS = 1e-6
NEG_INF = -1e30
ATTN_SCALE = HEAD_DIM ** -0.5

LANES = 128
MXU_DIM = 256
MOD_ROWS = 16
Q_TILE = 256
VMEM_LIMIT = 56 * 1024 * 1024

F32 = jnp.float32
BF16 = jnp.bfloat16
NT_DIMS = (((1,), (1,)), ((), ()))


def _params(sem, vmem=VMEM_LIMIT):
    return pltpu.CompilerParams(dimension_semantics=sem, vmem_limit_bytes=vmem)


def _pick_tile(n, candidates):
    for c in candidates:
        if n % c == 0:
            return c
    return n


def _ada_kernel(c_ref, w_ref, b_ref, o_ref):
    c = c_ref[...]
    a = c / (1.0 + jnp.exp(-c))
    o_ref[...] = jnp.dot(a, w_ref[...], preferred_element_type=F32,
                         precision=lax.Precision.HIGHEST) + b_ref[...]


def _ada_mods(cvecs, w_ada, b_ada):
    depth, d, n = w_ada.shape
    tn = _pick_tile(n, (1536, 1024, 512, 256, 128))
    return pl.pallas_call(
        _ada_kernel,
        grid=(depth, n // tn),
        in_specs=[
            pl.BlockSpec((MOD_ROWS, d), lambda l, j: (0, 0)),
            pl.BlockSpec((None, d, tn), lambda l, j: (l, 0, j)),
            pl.BlockSpec((None, 1, tn), lambda l, j: (l, 0, j)),
        ],
        out_specs=pl.BlockSpec((None, MOD_ROWS, tn), lambda l, j: (l, 0, j)),
        out_shape=jax.ShapeDtypeStruct((depth, MOD_ROWS, n), F32),
        compiler_params=_params(("arbitrary", "arbitrary")),
    )(cvecs, w_ada, b_ada.reshape(depth, 1, n))


def _norm_mod(x, g, shift, scale):
    ms = jnp.mean(x * x, axis=-1, keepdims=True)
    y = (x * lax.rsqrt(ms + EPS)) * g
    return y * (1.0 + scale) + shift


def _mod_spec(which, n_ctx_tiles, tiles_per_seq, n_ctx_row):
    def index(i, *_):
        row = jnp.where(i < n_ctx_tiles, n_ctx_row, (i - n_ctx_tiles) // tiles_per_seq)
        return (row * N_MOD + which, 0, 0)
    return index


def _qkv_kernel(x_ref, sh_ref, sc_ref, g_ref, w_ref, qn_ref, kn_ref, cos_ref, sin_ref, ones_ref,
                q_ref, kf_ref, vf_ref, kh_ref, vh_ref, *, use_a, d_model):
    q_dim = d_model
    kv_dim = N_KV_HEADS * HEAD_DIM
    h = _norm_mod(x_ref[...], g_ref[...], sh_ref[...], sc_ref[...])
    qkv = jnp.dot(h.astype(BF16), w_ref[...], preferred_element_type=F32)

    cos = cos_ref[...]
    sin = sin_ref[...]
    even = (lax.broadcasted_iota(jnp.int32, cos.shape, 1) % 2) == 0

    def head_norm(xc, gain):
        ss = jnp.dot((xc * xc).astype(BF16), ones_ref[...], preferred_element_type=F32)
        return xc * lax.rsqrt(ss * (1.0 / HEAD_DIM) + EPS) * gain

    def rope(xc):
        nxt = pltpu.roll(xc, LANES - 1, 1)
        prv = pltpu.roll(xc, 1, 1)
        return xc * cos + jnp.where(even, nxt, prv) * sin

    def process(col0, width, gain_ref):
        outs = []
        for c0 in range(0, width, MXU_DIM):
            xc = qkv[:, col0 + c0:col0 + c0 + MXU_DIM]
            if use_a:
                xc = head_norm(xc, gain_ref[:, c0:c0 + MXU_DIM])
            outs.append(jnp.concatenate(
                [rope(xc[:, l0:l0 + LANES]) for l0 in range(0, MXU_DIM, LANES)], axis=-1))
        return outs

    q_cols = process(0, q_dim, qn_ref)
    for i, qc in enumerate(q_cols):
        q_ref[:, i * MXU_DIM:(i + 1) * MXU_DIM] = (qc * ATTN_SCALE).astype(BF16)
    k = process(q_dim, kv_dim, kn_ref)[0]
    v = qkv[:, q_dim + kv_dim:]
    kf_ref[...] = k
    vf_ref[...] = v
    for hh in range(N_KV_HEADS):
        kh_ref[hh] = k[:, hh * HEAD_DIM:(hh + 1) * HEAD_DIM].astype(BF16)
        vh_ref[hh] = v[:, hh * HEAD_DIM:(hh + 1) * HEAD_DIM].astype(BF16)


def _qkv(x, mods, g, w, qn, kn, cos, sin, ones, *, use_a, tm, n_ctx_tiles, tiles_per_seq, n_ctx_row):
    t, d = x.shape
    kv_dim = N_KV_HEADS * HEAD_DIM
    n_out = w.shape[1]
    mod = functools.partial(_mod_spec, n_ctx_tiles=n_ctx_tiles, tiles_per_seq=tiles_per_seq,
                            n_ctx_row=n_ctx_row)
    row = lambda i: (i, 0)
    fixed = lambda i: (0, 0)
    return pl.pallas_call(
        functools.partial(_qkv_kernel, use_a=use_a, d_model=d),
        grid=(t // tm,),
        in_specs=[
            pl.BlockSpec((tm, d), row),
            pl.BlockSpec((None, 1, d), mod(0)),
            pl.BlockSpec((None, 1, d), mod(1)),
            pl.BlockSpec((1, d), fixed),
            pl.BlockSpec((d, n_out), fixed),
            pl.BlockSpec((1, d), fixed),
            pl.BlockSpec((1, kv_dim), fixed),
            pl.BlockSpec((tm, LANES), row),
            pl.BlockSpec((tm, LANES), row),
            pl.BlockSpec((MXU_DIM, MXU_DIM), fixed),
        ],
        out_specs=[
            pl.BlockSpec((tm, d), row),
            pl.BlockSpec((tm, kv_dim), row),
            pl.BlockSpec((tm, kv_dim), row),
            pl.BlockSpec((N_KV_HEADS, tm, HEAD_DIM), lambda i: (0, i, 0)),
            pl.BlockSpec((N_KV_HEADS, tm, HEAD_DIM), lambda i: (0, i, 0)),
        ],
        out_shape=[
            jax.ShapeDtypeStruct((t, d), BF16),
            jax.ShapeDtypeStruct((t, kv_dim), F32),
            jax.ShapeDtypeStruct((t, kv_dim), F32),
            jax.ShapeDtypeStruct((N_KV_HEADS, t, HEAD_DIM), BF16),
            jax.ShapeDtypeStruct((N_KV_HEADS, t, HEAD_DIM), BF16),
        ],
        compiler_params=_params(("arbitrary",)),
    )(x, mods, mods, g, w, qn, kn, cos, sin, ones)


def _softmax_pv(score_parts, value_parts, sink):
    m = score_parts[0].max(axis=-1, keepdims=True)
    for s in score_parts[1:]:
        m = jnp.maximum(m, s.max(axis=-1, keepdims=True))
    if sink is not None:
        m = jnp.maximum(m, sink)
    denom = None
    acc = None
    for s, v in zip(score_parts, value_parts):
        p = jnp.exp(s - m)
        ps = p.sum(axis=-1, keepdims=True)
        pv = jnp.dot(p.astype(BF16), v, preferred_element_type=F32)
        denom = ps if denom is None else denom + ps
        acc = pv if acc is None else acc + pv
    if sink is not None:
        denom = denom + jnp.exp(sink - m)
    return acc / denom


def _attn_ctx_kernel(sink_ref, q_ref, k_ref, v_ref, o_ref, *, has_sink):
    grp = pl.program_id(1)
    k = k_ref[...]
    v = v_ref[...]
    for j in range(GROUP):
        qj = q_ref[:, j * HEAD_DIM:(j + 1) * HEAD_DIM]
        s = lax.dot_general(qj, k, NT_DIMS, preferred_element_type=F32)
        sink = sink_ref[grp * GROUP + j] if has_sink else None
        o_ref[:, j * HEAD_DIM:(j + 1) * HEAD_DIM] = _softmax_pv([s], [v], sink).astype(o_ref.dtype)


def _attn_ctx(sink, q, kh, vh, *, n_ctx_seq, seq, has_sink):
    t, d = q.shape
    gw = GROUP * HEAD_DIM
    return pl.pallas_call(
        functools.partial(_attn_ctx_kernel, has_sink=has_sink),
        grid=(n_ctx_seq, N_KV_HEADS),
        in_specs=[
            pl.BlockSpec(memory_space=pltpu.SMEM),
            pl.BlockSpec((seq, gw), lambda b, g: (b, g)),
            pl.BlockSpec((None, seq, HEAD_DIM), lambda b, g: (g, b, 0)),
            pl.BlockSpec((None, seq, HEAD_DIM), lambda b, g: (g, b, 0)),
        ],
        out_specs=pl.BlockSpec((seq, gw), lambda b, g: (b, g)),
        out_shape=jax.ShapeDtypeStruct((t, d), BF16),
        compiler_params=_params(("arbitrary", "arbitrary")),
    )(sink, q, kh, vh)


def _attn_global_kernel(o_prev_ref, q_ref, k_ref, v_ref, kc_ref, vc_ref, o_ref):
    del o_prev_ref
    k = k_ref[...]
    v = v_ref[...]
    kc = kc_ref[...]
    vc = vc_ref[...]
    for j in range(GROUP):
        qj = q_ref[:, j * HEAD_DIM:(j + 1) * HEAD_DIM]
        s_lat = lax.dot_general(qj, k, NT_DIMS, preferred_element_type=F32)
        s_ctx = lax.dot_general(qj, kc, NT_DIMS, preferred_element_type=F32)
        o = _softmax_pv([s_lat, s_ctx], [v, vc], None)
        o_ref[:, j * HEAD_DIM:(j + 1) * HEAD_DIM] = o.astype(o_ref.dtype)


def _attn_global(o_prev, q, kh, vh, kc, vc, *, n_ctx_tokens, n_lat_seq, lat_seq, past):
    t, d = q.shape
    gw = GROUP * HEAD_DIM
    q_blocks = lat_seq // Q_TILE
    q0 = n_ctx_tokens // Q_TILE
    s0 = n_ctx_tokens // lat_seq
    qmap = lambda b, g, i: (q0 + b * q_blocks + i, g)
    return pl.pallas_call(
        _attn_global_kernel,
        grid=(n_lat_seq, N_KV_HEADS, q_blocks),
        in_specs=[
            pl.BlockSpec(memory_space=pl.ANY),
            pl.BlockSpec((Q_TILE, gw), qmap),
            pl.BlockSpec((None, lat_seq, HEAD_DIM), lambda b, g, i: (g, s0 + b, 0)),
            pl.BlockSpec((None, lat_seq, HEAD_DIM), lambda b, g, i: (g, s0 + b, 0)),
            pl.BlockSpec((None, past, HEAD_DIM), lambda b, g, i: (g, b, 0)),
            pl.BlockSpec((None, past, HEAD_DIM), lambda b, g, i: (g, b, 0)),
        ],
        out_specs=pl.BlockSpec((Q_TILE, gw), qmap),
        out_shape=jax.ShapeDtypeStruct((t, d), BF16),
        input_output_aliases={0: 0},
        compiler_params=_params(("arbitrary", "arbitrary", "arbitrary")),
    )(o_prev, q, kh, vh, kc, vc)


def _attn_window_kernel(sink_ref, o_prev_ref, q_ref, kp_ref, kcur_ref, kn_ref, vp_ref, vcur_ref, vn_ref,
                        kc_ref, vc_ref, o_ref, *, lat_seq):
    del o_prev_ref
    grp = pl.program_id(1)
    start = pl.program_id(2) * Q_TILE
    span = Q_TILE + 2 * WINDOW
    kw = jnp.concatenate([kp_ref[...], kcur_ref[...], kn_ref[...]], axis=0)
    vw = jnp.concatenate([vp_ref[...], vcur_ref[...], vn_ref[...]], axis=0)
    kc = kc_ref[...]
    vc = vc_ref[...]
    qpos = start + lax.broadcasted_iota(jnp.int32, (Q_TILE, span), 0)
    kpos = start - WINDOW + lax.broadcasted_iota(jnp.int32, (Q_TILE, span), 1)
    valid = (kpos >= 0) & (kpos < lat_seq) & (jnp.abs(qpos - kpos) <= WINDOW)
    for j in range(GROUP):
        qj = q_ref[:, j * HEAD_DIM:(j + 1) * HEAD_DIM]
        s_loc = lax.dot_general(qj, kw, NT_DIMS, preferred_element_type=F32)
        s_loc = jnp.where(valid, s_loc, NEG_INF)
        s_ctx = lax.dot_general(qj, kc, NT_DIMS, preferred_element_type=F32)
        o = _softmax_pv([s_loc, s_ctx], [vw, vc], sink_ref[grp * GROUP + j])
        o_ref[:, j * HEAD_DIM:(j + 1) * HEAD_DIM] = o.astype(o_ref.dtype)


def _attn_window(sink, o_prev, q, kh, vh, kc, vc, *, n_ctx_tokens, n_lat_seq, lat_seq, past):
    t, d = q.shape
    gw = GROUP * HEAD_DIM
    q_blocks = lat_seq // Q_TILE
    q0 = n_ctx_tokens // Q_TILE
    half = Q_TILE // WINDOW
    w0 = n_ctx_tokens // WINDOW
    w_blocks = lat_seq // WINDOW
    qmap = lambda b, g, i: (q0 + b * q_blocks + i, g)
    prev = lambda b, g, i: (g, w0 + b * w_blocks + jnp.maximum(i * half - 1, 0), 0)
    cur = lambda b, g, i: (g, q0 + b * q_blocks + i, 0)
    nxt = lambda b, g, i: (g, w0 + b * w_blocks + jnp.minimum((i + 1) * half, w_blocks - 1), 0)
    ctx = lambda b, g, i: (g, b, 0)
    kv_specs = [
        pl.BlockSpec((None, WINDOW, HEAD_DIM), prev),
        pl.BlockSpec((None, Q_TILE, HEAD_DIM), cur),
        pl.BlockSpec((None, WINDOW, HEAD_DIM), nxt),
    ]
    return pl.pallas_call(
        functools.partial(_attn_window_kernel, lat_seq=lat_seq),
        grid=(n_lat_seq, N_KV_HEADS, q_blocks),
        in_specs=[
            pl.BlockSpec(memory_space=pltpu.SMEM),
            pl.BlockSpec(memory_space=pl.ANY),
            pl.BlockSpec((Q_TILE, gw), qmap),
            *kv_specs, *kv_specs,
            pl.BlockSpec((None, past, HEAD_DIM), ctx),
            pl.BlockSpec((None, past, HEAD_DIM), ctx),
        ],
        out_specs=pl.BlockSpec((Q_TILE, gw), qmap),
        out_shape=jax.ShapeDtypeStruct((t, d), BF16),
        input_output_aliases={1: 0},
        compiler_params=_params(("arbitrary", "arbitrary", "arbitrary")),
    )(sink, o_prev, q, kh, kh, kh, vh, vh, vh, kc, vc)


def _proj_kernel(o_ref, w_ref, x_ref, gate_ref, out_ref):
    y = jnp.dot(o_ref[...], w_ref[...], preferred_element_type=F32)
    out_ref[...] = x_ref[...] + gate_ref[...] * y


def _proj(o, w, x, mods, *, tm, n_ctx_tiles, tiles_per_seq, n_ctx_row):
    t, d = x.shape
    mod = functools.partial(_mod_spec, n_ctx_tiles=n_ctx_tiles, tiles_per_seq=tiles_per_seq,
                            n_ctx_row=n_ctx_row)
    row = lambda i: (i, 0)
    return pl.pallas_call(
        _proj_kernel,
        grid=(t // tm,),
        in_specs=[
            pl.BlockSpec((tm, d), row),
            pl.BlockSpec((d, d), lambda i: (0, 0)),
            pl.BlockSpec((tm, d), row),
            pl.BlockSpec((None, 1, d), mod(2)),
        ],
        out_specs=pl.BlockSpec((tm, d), row),
        out_shape=jax.ShapeDtypeStruct((t, d), F32),
        compiler_params=_params(("arbitrary",)),
    )(o, w, x, mods)


def _swiglu_act(h, wg, wu):
    g = jnp.dot(h, wg, preferred_element_type=F32)
    u = jnp.dot(h, wu, preferred_element_type=F32)
    return (g / (1.0 + jnp.exp(-g))) * u


def _ffn_kernel(x_ref, sh_ref, sc_ref, gate_ref, g_ref, wg_ref, wu_ref, wd_ref, out_ref, h_ref, acc_ref):
    j = pl.program_id(1)

    @pl.when(j == 0)
    def _():
        h_ref[...] = _norm_mod(x_ref[...], g_ref[...], sh_ref[...], sc_ref[...]).astype(BF16)
        acc_ref[...] = jnp.zeros_like(acc_ref)

    a = _swiglu_act(h_ref[...], wg_ref[...], wu_ref[...])
    acc_ref[...] += jnp.dot(a.astype(BF16), wd_ref[...], preferred_element_type=F32)

    @pl.when(j == pl.num_programs(1) - 1)
    def _():
        out_ref[...] = x_ref[...] + gate_ref[...] * acc_ref[...]


def _ffn(x, mods, g, w_up, w_down, *, tm, tf, n_ctx_tiles, tiles_per_seq, n_ctx_row):
    t, d = x.shape
    d_ff = w_down.shape[0]
    nf = d_ff // tf
    mod = functools.partial(_mod_spec, n_ctx_tiles=n_ctx_tiles, tiles_per_seq=tiles_per_seq,
                            n_ctx_row=n_ctx_row)
    row = lambda i, j: (i, 0)
    return pl.pallas_call(
        _ffn_kernel,
        grid=(t // tm, nf),
        in_specs=[
            pl.BlockSpec((tm, d), row),
            pl.BlockSpec((None, 1, d), mod(3)),
            pl.BlockSpec((None, 1, d), mod(4)),
            pl.BlockSpec((None, 1, d), mod(5)),
            pl.BlockSpec((1, d), lambda i, j: (0, 0)),
            pl.BlockSpec((d, tf), lambda i, j: (0, j)),
            pl.BlockSpec((d, tf), lambda i, j: (0, nf + j)),
            pl.BlockSpec((tf, d), lambda i, j: (j, 0)),
        ],
        out_specs=pl.BlockSpec((tm, d), row),
        out_shape=jax.ShapeDtypeStruct((t, d), F32),
        scratch_shapes=[pltpu.VMEM((tm, d), BF16), pltpu.VMEM((tm, d), F32)],
        compiler_params=_params(("arbitrary", "arbitrary")),
    )(x, mods, mods, mods, g, w_up, w_up, w_down)


def _router_kernel(x_ref, sh_ref, sc_ref, g_ref, wr_ref, h_ref, comb_ref, *, n_experts):
    h = _norm_mod(x_ref[...], g_ref[...], sh_ref[...], sc_ref[...])
    h_ref[...] = h.astype(BF16)
    logits = jnp.dot(h, wr_ref[...], preferred_element_type=F32, precision=lax.Precision.HIGHEST)
    lane = lax.broadcasted_iota(jnp.int32, logits.shape, 1).astype(F32)
    logits = jnp.where(lane < n_experts, logits, -jnp.inf)
    top1 = logits.max(axis=-1, keepdims=True)
    idx1 = jnp.where(logits == top1, lane, float(LANES)).min(axis=-1, keepdims=True)
    rest = jnp.where(lane == idx1, -jnp.inf, logits)
    top2 = rest.max(axis=-1, keepdims=True)
    idx2 = jnp.where(rest == top2, lane, float(LANES)).min(axis=-1, keepdims=True)
    e = jnp.exp(top2 - top1)
    g1 = 1.0 / (1.0 + e)
    g2 = e / (1.0 + e)
    comb_ref[...] = jnp.where(lane == idx1, g1, 0.0) + jnp.where(lane == idx2, g2, 0.0)


def _router(x, mods, g, wr, *, n_experts, tm, n_ctx_tiles, tiles_per_seq, n_ctx_row):
    t, d = x.shape
    mod = functools.partial(_mod_spec, n_ctx_tiles=n_ctx_tiles, tiles_per_seq=tiles_per_seq,
                            n_ctx_row=n_ctx_row)
    row = lambda i: (i, 0)
    return pl.pallas_call(
        functools.partial(_router_kernel, n_experts=n_experts),
        grid=(t // tm,),
        in_specs=[
            pl.BlockSpec((tm, d), row),
            pl.BlockSpec((None, 1, d), mod(3)),
            pl.BlockSpec((None, 1, d), mod(4)),
            pl.BlockSpec((1, d), lambda i: (0, 0)),
            pl.BlockSpec((d, LANES), lambda i: (0, 0)),
        ],
        out_specs=[pl.BlockSpec((tm, d), row), pl.BlockSpec((tm, LANES), row)],
        out_shape=[jax.ShapeDtypeStruct((t, d), BF16), jax.ShapeDtypeStruct((t, LANES), F32)],
        compiler_params=_params(("arbitrary",)),
    )(x, mods, mods, g, wr)


def _moe_kernel(h_ref, comb_ref, x_ref, gate_ref, wg_ref, wu_ref, wd_ref, out_ref, acc_ref):
    e = pl.program_id(1)
    j = pl.program_id(2)

    @pl.when((e == 0) & (j == 0))
    def _():
        acc_ref[...] = jnp.zeros_like(acc_ref)

    comb = comb_ref[...]
    lane = lax.broadcasted_iota(jnp.int32, comb.shape, 1)
    weight = jnp.where(lane == e, comb, 0.0).sum(axis=-1, keepdims=True)
    a = _swiglu_act(h_ref[...], wg_ref[...], wu_ref[...]) * weight
    acc_ref[...] += jnp.dot(a.astype(BF16), wd_ref[...], preferred_element_type=F32)

    @pl.when((e == pl.num_programs(1) - 1) & (j == pl.num_programs(2) - 1))
    def _():
        out_ref[...] = x_ref[...] + gate_ref[...] * acc_ref[...]


def _moe(h, comb, x, mods, w_up, w_down, *, tm, tf, n_ctx_tiles, tiles_per_seq, n_ctx_row):
    t, d = x.shape
    n_experts, d_ff, _ = w_down.shape
    nf = d_ff // tf
    mod = functools.partial(_mod_spec, n_ctx_tiles=n_ctx_tiles, tiles_per_seq=tiles_per_seq,
                            n_ctx_row=n_ctx_row)
    row = lambda i, e, j: (i, 0)
    return pl.pallas_call(
        _moe_kernel,
        grid=(t // tm, n_experts, nf),
        in_specs=[
            pl.BlockSpec((tm, d), row),
            pl.BlockSpec((tm, LANES), row),
            pl.BlockSpec((tm, d), row),
            pl.BlockSpec((None, 1, d), mod(5)),
            pl.BlockSpec((None, d, tf), lambda i, e, j: (e, 0, j)),
            pl.BlockSpec((None, d, tf), lambda i, e, j: (e, 0, nf + j)),
            pl.BlockSpec((None, tf, d), lambda i, e, j: (e, j, 0)),
        ],
        out_specs=pl.BlockSpec((tm, d), row),
        out_shape=jax.ShapeDtypeStruct((t, d), F32),
        scratch_shapes=[pltpu.VMEM((tm, d), F32)],
        compiler_params=_params(("arbitrary", "arbitrary", "arbitrary")),
    )(h, comb, x, mods, w_up, w_up, w_down)


def _final_norm_kernel(x_ref, g_ref, o_ref):
    x = x_ref[...]
    ms = jnp.mean(x * x, axis=-1, keepdims=True)
    o_ref[...] = (x * lax.rsqrt(ms + EPS)) * g_ref[...]


def _final_norm(x, g, *, tm):
    t, d = x.shape
    return pl.pallas_call(
        _final_norm_kernel,
        grid=(t // tm,),
        in_specs=[pl.BlockSpec((tm, d), lambda i: (i, 0)), pl.BlockSpec((1, d), lambda i: (0, 0))],
        out_specs=pl.BlockSpec((tm, d), lambda i: (i, 0)),
        out_shape=jax.ShapeDtypeStruct((t, d), F32),
        compiler_params=_params(("arbitrary",)),
    )(x, g)


def _rope_tables(n_ctx_tokens, n_lat_seq, lat_seq):
    pos = jnp.arange(lat_seq)
    rows = (pos // GRID_W).astype(F32)
    cols = (pos % GRID_W).astype(F32)
    pairs_per_axis = HEAD_DIM // 4
    inv = ROPE_THETA ** (-jnp.arange(pairs_per_axis, dtype=F32) / pairs_per_axis)
    ang = jnp.concatenate([rows[:, None] * inv, cols[:, None] * inv], axis=-1)
    cos = jnp.repeat(jnp.cos(ang), 2, axis=-1)
    sin = jnp.repeat(jnp.sin(ang), 2, axis=-1) * jnp.tile(jnp.array([-1.0, 1.0], F32), HEAD_DIM // 2)
    reps = LANES // HEAD_DIM
    cos = jnp.tile(cos, (n_lat_seq, reps))
    sin = jnp.tile(sin, (n_lat_seq, reps))
    cos = jnp.concatenate([jnp.ones((n_ctx_tokens, LANES), F32), cos], axis=0)
    sin = jnp.concatenate([jnp.zeros((n_ctx_tokens, LANES), F32), sin], axis=0)
    return cos, sin


def kernel(x_prompt, x_sample, cache_k, cache_v, c, c_ctx, w_ada, b_ada, norm_mix, norm_ffn, norm_final,
           w_qkv, w_o, q_norm, k_norm, sinks, w_ffn_up, w_ffn_down, w_router, w_exp_up, w_exp_down):
    n_ctx_seq, seq, d = x_prompt.shape
    n_lat_seq, lat_seq, _ = x_sample.shape
    depth = w_ada.shape[0]
    past = cache_k.shape[2]
    n_experts = w_router.shape[-1]
    kv_dim = N_KV_HEADS * HEAD_DIM
    n_ctx_tokens = n_ctx_seq * seq
    n_lat_tokens = n_lat_seq * lat_seq
    assert d == N_HEADS * HEAD_DIM and seq == Q_TILE and lat_seq % Q_TILE == 0
    assert n_ctx_tokens % lat_seq == 0 and n_lat_seq < MOD_ROWS and n_experts <= LANES

    tm = 2 * Q_TILE if (lat_seq % (2 * Q_TILE) == 0 and n_ctx_tokens % (2 * Q_TILE) == 0) else Q_TILE
    tiling = dict(tm=tm, n_ctx_tiles=n_ctx_tokens // tm, tiles_per_seq=lat_seq // tm, n_ctx_row=n_lat_seq)

    x = jnp.concatenate([x_prompt.reshape(n_ctx_tokens, d), x_sample.reshape(n_lat_tokens, d)], axis=0)

    cvecs = jnp.zeros((MOD_ROWS, d), F32).at[:n_lat_seq].set(c).at[n_lat_seq].set(c_ctx)
    mods_all = _ada_mods(cvecs, w_ada, b_ada)
    mods_all = mods_all.reshape(depth, MOD_ROWS * N_MOD, 1, d)

    cos, sin = _rope_tables(n_ctx_tokens, n_lat_seq, lat_seq)
    head_id = jnp.arange(MXU_DIM) // HEAD_DIM
    ones = (head_id[:, None] == head_id[None, :]).astype(BF16)

    kc_all = cache_k.transpose(1, 3, 0, 2, 4).reshape(depth, N_KV_HEADS, n_lat_seq * past, HEAD_DIM).astype(BF16)
    vc_all = cache_v.transpose(1, 3, 0, 2, 4).reshape(depth, N_KV_HEADS, n_lat_seq * past, HEAD_DIM).astype(BF16)

    new_k = []
    new_v = []
    for l in range(depth):
        idx = l // 2
        use_a = (l % 2 == 0)
        mods = mods_all[l]
        w_l = w_qkv[l].astype(BF16)
        if use_a:
            qn = jnp.tile(q_norm[idx], d // HEAD_DIM).reshape(1, d)
            kn = jnp.tile(k_norm[idx], N_KV_HEADS).reshape(1, kv_dim)
            sink = jnp.zeros((N_HEADS,), F32)
        else:
            qn = jnp.ones((1, d), F32)
            kn = jnp.ones((1, kv_dim), F32)
            sink = sinks[idx].astype(F32)

        q, kf, vf, kh, vh = _qkv(x, mods, norm_mix[l].reshape(1, d), w_l, qn, kn, cos, sin, ones,
                                 use_a=use_a, **tiling)
        new_k.append(kf[:n_ctx_tokens].reshape(n_ctx_seq, seq, N_KV_HEADS, HEAD_DIM))
        new_v.append(vf[:n_ctx_tokens].reshape(n_ctx_seq, seq, N_KV_HEADS, HEAD_DIM))

        o = _attn_ctx(sink, q, kh, vh, n_ctx_seq=n_ctx_seq, seq=seq, has_sink=not use_a)
        lat = dict(n_ctx_tokens=n_ctx_tokens, n_lat_seq=n_lat_seq, lat_seq=lat_seq, past=past)
        if use_a:
            o = _attn_global(o, q, kh, vh, kc_all[l], vc_all[l], **lat)
        else:
            o = _attn_window(sink, o, q, kh, vh, kc_all[l], vc_all[l], **lat)
        x = _proj(o, w_o[l].astype(BF16), x, mods, **tiling)

        if use_a:
            d_ff = w_ffn_down.shape[1]
            tf = _pick_tile(d_ff, (1408, 1024, 512, 256, 128))
            x = _ffn(x, mods, norm_ffn[l].reshape(1, d), w_ffn_up[idx].astype(BF16),
                     w_ffn_down[idx].astype(BF16), tf=tf, **tiling)
        else:
            d_ffe = w_exp_down.shape[2]
            tf = _pick_tile(d_ffe, (1792, 1024, 512, 256, 128))
            wr = jnp.zeros((d, LANES), F32).at[:, :n_experts].set(w_router[idx])
            h, comb = _router(x, mods, norm_ffn[l].reshape(1, d), wr, n_experts=n_experts, **tiling)
            x = _moe(h, comb, x, mods, w_exp_up[idx].astype(BF16), w_exp_down[idx].astype(BF16),
                     tf=tf, **tiling)

    y = _final_norm(x, norm_final.reshape(1, d), tm=tm)
    y_prompt = y[:n_ctx_tokens].reshape(n_ctx_seq, seq, d)
    y_sample = y[n_ctx_tokens:].reshape(n_lat_seq, lat_seq, d)
    return (y_prompt, y_sample, jnp.stack(new_k, axis=1), jnp.stack(new_v, axis=1))
```

```python
import functools

import jax
import jax.numpy as jnp
from jax import lax
from jax.experimental import pallas as pl
from jax.experimental.pallas import tpu as pltpu

N_HEADS = 16
N_KV_HEADS = 4
GROUP = N_HEADS // N_KV_HEADS
HEAD_DIM = 64
GRID_W = 64
WINDOW = 128
ROPE_THETA = 10000.0
N_MOD = 6
TOP_K = 2
EPS = 1e-6
NEG_INF = -1e30
ATTN_SCALE = HEAD_DIM ** -0.5

LANES = 128
MXU_DIM = 256
MOD_ROWS = 16
Q_TILE = 256
VMEM_LIMIT = 56 * 1024 * 1024

F32 = jnp.float32
BF16 = jnp.bfloat16
NT_DIMS = (((1,), (1,)), ((), ()))


def _params(sem, vmem=VMEM_LIMIT):
    return pltpu.CompilerParams(dimension_semantics=sem, vmem_limit_bytes=vmem)


def _pick_tile(n, candidates):
    for c in candidates:
        if n % c == 0:
            return c
    return n


def _ada_kernel(c_ref, w_ref, b_ref, o_ref):
    c = c_ref[...]
    a = c / (1.0 + jnp.exp(-c))
    o_ref[...] = jnp.dot(a, w_ref[...], preferred_element_type=F32,
                         precision=lax.Precision.HIGHEST) + b_ref[...]


def _ada_mods(cvecs, w_ada, b_ada):
    depth, d, n = w_ada.shape
    tn = _pick_tile(n, (1536, 1024, 512, 256, 128))
    return pl.pallas_call(
        _ada_kernel,
        grid=(depth, n // tn),
        in_specs=[
            pl.BlockSpec((MOD_ROWS, d), lambda l, j: (0, 0)),
            pl.BlockSpec((None, d, tn), lambda l, j: (l, 0, j)),
            pl.BlockSpec((None, 1, tn), lambda l, j: (l, 0, j)),
        ],
        out_specs=pl.BlockSpec((None, MOD_ROWS, tn), lambda l, j: (l, 0, j)),
        out_shape=jax.ShapeDtypeStruct((depth, MOD_ROWS, n), F32),
        compiler_params=_params(("arbitrary", "arbitrary")),
    )(cvecs, w_ada, b_ada.reshape(depth, 1, n))


def _norm_mod(x, g, shift, scale):
    ms = jnp.mean(x * x, axis=-1, keepdims=True)
    y = (x * lax.rsqrt(ms + EPS)) * g
    return y * (1.0 + scale) + shift


def _mod_spec(which, n_ctx_tiles, tiles_per_seq, n_ctx_row):
    def index(i, *_):
        row = jnp.where(i < n_ctx_tiles, n_ctx_row, (i - n_ctx_tiles) // tiles_per_seq)
        return (row * N_MOD + which, 0, 0)
    return index


def _qkv_kernel(x_ref, sh_ref, sc_ref, g_ref, w_ref, qn_ref, kn_ref, cos_ref, sin_ref, ones_ref,
                q_ref, kf_ref, vf_ref, kh_ref, vh_ref, *, use_a, d_model):
    q_dim = d_model
    kv_dim = N_KV_HEADS * HEAD_DIM
    h = _norm_mod(x_ref[...], g_ref[...], sh_ref[...], sc_ref[...])
    qkv = jnp.dot(h.astype(BF16), w_ref[...], preferred_element_type=F32)

    cos = cos_ref[...]
    sin = sin_ref[...]
    even = (lax.broadcasted_iota(jnp.int32, cos.shape, 1) % 2) == 0

    def head_norm(xc, gain):
        ss = jnp.dot((xc * xc).astype(BF16), ones_ref[...], preferred_element_type=F32)
        return xc * lax.rsqrt(ss * (1.0 / HEAD_DIM) + EPS) * gain

    def rope(xc):
        nxt = pltpu.roll(xc, LANES - 1, 1)
        prv = pltpu.roll(xc, 1, 1)
        return xc * cos + jnp.where(even, nxt, prv) * sin

    def process(col0, width, gain_ref):
        outs = []
        for c0 in range(0, width, MXU_DIM):
            xc = qkv[:, col0 + c0:col0 + c0 + MXU_DIM]
            if use_a:
                xc = head_norm(xc, gain_ref[:, c0:c0 + MXU_DIM])
            outs.append(jnp.concatenate(
                [rope(xc[:, l0:l0 + LANES]) for l0 in range(0, MXU_DIM, LANES)], axis=-1))
        return outs

    q_cols = process(0, q_dim, qn_ref)
    for i, qc in enumerate(q_cols):
        q_ref[:, i * MXU_DIM:(i + 1) * MXU_DIM] = (qc * ATTN_SCALE).astype(BF16)
    k = process(q_dim, kv_dim, kn_ref)[0]
    v = qkv[:, q_dim + kv_dim:]
    kf_ref[...] = k
    vf_ref[...] = v
    for hh in range(N_KV_HEADS):
        kh_ref[hh] = k[:, hh * HEAD_DIM:(hh + 1) * HEAD_DIM].astype(BF16)
        vh_ref[hh] = v[:, hh * HEAD_DIM:(hh + 1) * HEAD_DIM].astype(BF16)


def _qkv(x, mods, g, w, qn, kn, cos, sin, ones, *, use_a, tm, n_ctx_tiles, tiles_per_seq, n_ctx_row):
    t, d = x.shape
    kv_dim = N_KV_HEADS * HEAD_DIM
    n_out = w.shape[1]
    mod = functools.partial(_mod_spec, n_ctx_tiles=n_ctx_tiles, tiles_per_seq=tiles_per_seq,
                            n_ctx_row=n_ctx_row)
    row = lambda i: (i, 0)
    fixed = lambda i: (0, 0)
    return pl.pallas_call(
        functools.partial(_qkv_kernel, use_a=use_a, d_model=d),
        grid=(t // tm,),
        in_specs=[
            pl.BlockSpec((tm, d), row),
            pl.BlockSpec((None, 1, d), mod(0)),
            pl.BlockSpec((None, 1, d), mod(1)),
            pl.BlockSpec((1, d), fixed),
            pl.BlockSpec((d, n_out), fixed),
            pl.BlockSpec((1, d), fixed),
            pl.BlockSpec((1, kv_dim), fixed),
            pl.BlockSpec((tm, LANES), row),
            pl.BlockSpec((tm, LANES), row),
            pl.BlockSpec((MXU_DIM, MXU_DIM), fixed),
        ],
        out_specs=[
            pl.BlockSpec((tm, d), row),
            pl.BlockSpec((tm, kv_dim), row),
            pl.BlockSpec((tm, kv_dim), row),
            pl.BlockSpec((N_KV_HEADS, tm, HEAD_DIM), lambda i: (0, i, 0)),
            pl.BlockSpec((N_KV_HEADS, tm, HEAD_DIM), lambda i: (0, i, 0)),
        ],
        out_shape=[
            jax.ShapeDtypeStruct((t, d), BF16),
            jax.ShapeDtypeStruct((t, kv_dim), F32),
            jax.ShapeDtypeStruct((t, kv_dim), F32),
            jax.ShapeDtypeStruct((N_KV_HEADS, t, HEAD_DIM), BF16),
            jax.ShapeDtypeStruct((N_KV_HEADS, t, HEAD_DIM), BF16),
        ],
        compiler_params=_params(("arbitrary",)),
    )(x, mods, mods, g, w, qn, kn, cos, sin, ones)


def _softmax_pv(score_parts, value_parts, sink):
    m = score_parts[0].max(axis=-1, keepdims=True)
    for s in score_parts[1:]:
        m = jnp.maximum(m, s.max(axis=-1, keepdims=True))
    if sink is not None:
        m = jnp.maximum(m, sink)
    denom = None
    acc = None
    for s, v in zip(score_parts, value_parts):
        p = jnp.exp(s - m)
        ps = p.sum(axis=-1, keepdims=True)
        pv = jnp.dot(p.astype(BF16), v, preferred_element_type=F32)
        denom = ps if denom is None else denom + ps
        acc = pv if acc is None else acc + pv
    if sink is not None:
        denom = denom + jnp.exp(sink - m)
    return acc / denom


def _attn_ctx_kernel(sink_ref, q_ref, k_ref, v_ref, o_ref, *, has_sink):
    grp = pl.program_id(1)
    k = k_ref[...]
    v = v_ref[...]
    for j in range(GROUP):
        qj = q_ref[:, j * HEAD_DIM:(j + 1) * HEAD_DIM]
        s = lax.dot_general(qj, k, NT_DIMS, preferred_element_type=F32)
        sink = sink_ref[grp * GROUP + j] if has_sink else None
        o_ref[:, j * HEAD_DIM:(j + 1) * HEAD_DIM] = _softmax_pv([s], [v], sink).astype(o_ref.dtype)


def _attn_ctx(sink, q, kh, vh, *, n_ctx_seq, seq, has_sink):
    t, d = q.shape
    gw = GROUP * HEAD_DIM
    return pl.pallas_call(
        functools.partial(_attn_ctx_kernel, has_sink=has_sink),
        grid=(n_ctx_seq, N_KV_HEADS),
        in_specs=[
            pl.BlockSpec(memory_space=pltpu.SMEM),
            pl.BlockSpec((seq, gw), lambda b, g: (b, g)),
            pl.BlockSpec((None, seq, HEAD_DIM), lambda b, g: (g, b, 0)),
            pl.BlockSpec((None, seq, HEAD_DIM), lambda b, g: (g, b, 0)),
        ],
        out_specs=pl.BlockSpec((seq, gw), lambda b, g: (b, g)),
        out_shape=jax.ShapeDtypeStruct((n_ctx_seq * seq, d), BF16),
        compiler_params=_params(("arbitrary", "arbitrary")),
    )(sink, q, kh, vh)


def _attn_global_kernel(q_ref, k_ref, v_ref, kc_ref, vc_ref, o_ref):
    k = k_ref[...]
    v = v_ref[...]
    kc = kc_ref[...]
    vc = vc_ref[...]
    for j in range(GROUP):
        qj = q_ref[:, j * HEAD_DIM:(j + 1) * HEAD_DIM]
        s_lat = lax.dot_general(qj, k, NT_DIMS, preferred_element_type=F32)
        s_ctx = lax.dot_general(qj, kc, NT_DIMS, preferred_element_type=F32)
        o = _softmax_pv([s_lat, s_ctx], [v, vc], None)
        o_ref[:, j * HEAD_DIM:(j + 1) * HEAD_DIM] = o.astype(o_ref.dtype)


def _attn_global(q, kh, vh, kc, vc, *, n_ctx_tokens, n_lat_seq, lat_seq, past):
    t, d = q.shape
    gw = GROUP * HEAD_DIM
    q_blocks = lat_seq // Q_TILE
    q0 = n_ctx_tokens // Q_TILE
    s0 = n_ctx_tokens // lat_seq
    qmap = lambda b, g, i: (q0 + b * q_blocks + i, g)
    return pl.pallas_call(
        _attn_global_kernel,
        grid=(n_lat_seq, N_KV_HEADS, q_blocks),
        in_specs=[
            pl.BlockSpec((Q_TILE, gw), qmap),
            pl.BlockSpec((None, lat_seq, HEAD_DIM), lambda b, g, i: (g, s0 + b, 0)),
            pl.BlockSpec((None, lat_seq, HEAD_DIM), lambda b, g, i: (g, s0 + b, 0)),
            pl.BlockSpec((None, past, HEAD_DIM), lambda b, g, i: (g, b, 0)),
            pl.BlockSpec((None, past, HEAD_DIM), lambda b, g, i: (g, b, 0)),
        ],
        out_specs=pl.BlockSpec((Q_TILE, gw), lambda b, g, i: (b * q_blocks + i, g)),
        out_shape=jax.ShapeDtypeStruct((n_lat_seq * lat_seq, d), BF16),
        compiler_params=_params(("arbitrary", "arbitrary", "arbitrary")),
    )(q, kh, vh, kc, vc)


def _attn_window_kernel(sink_ref, q_ref, kp_ref, kcur_ref, kn_ref, vp_ref, vcur_ref, vn_ref,
                        kc_ref, vc_ref, o_ref, *, lat_seq):
    grp = pl.program_id(1)
    start = pl.program_id(2) * Q_TILE
    span = Q_TILE + 2 * WINDOW
    kw = jnp.concatenate([kp_ref[...], kcur_ref[...], kn_ref[...]], axis=0)
    vw = jnp.concatenate([vp_ref[...], vcur_ref[...], vn_ref[...]], axis=0)
    kc = kc_ref[...]
    vc = vc_ref[...]
    qpos = start + lax.broadcasted_iota(jnp.int32, (Q_TILE, span), 0)
    kpos = start - WINDOW + lax.broadcasted_iota(jnp.int32, (Q_TILE, span), 1)
    valid = (kpos >= 0) & (kpos < lat_seq) & (jnp.abs(qpos - kpos) <= WINDOW)
    for j in range(GROUP):
        qj = q_ref[:, j * HEAD_DIM:(j + 1) * HEAD_DIM]
        s_loc = lax.dot_general(qj, kw, NT_DIMS, preferred_element_type=F32)
        s_loc = jnp.where(valid, s_loc, NEG_INF)
        s_ctx = lax.dot_general(qj, kc, NT_DIMS, preferred_element_type=F32)
        o = _softmax_pv([s_loc, s_ctx], [vw, vc], sink_ref[grp * GROUP + j])
        o_ref[:, j * HEAD_DIM:(j + 1) * HEAD_DIM] = o.astype(o_ref.dtype)


def _attn_window(sink, q, kh, vh, kc, vc, *, n_ctx_tokens, n_lat_seq, lat_seq, past):
    t, d = q.shape
    gw = GROUP * HEAD_DIM
    q_blocks = lat_seq // Q_TILE
    q0 = n_ctx_tokens // Q_TILE
    half = Q_TILE // WINDOW
    w0 = n_ctx_tokens // WINDOW
    w_blocks = lat_seq // WINDOW
    qmap = lambda b, g, i: (q0 + b * q_blocks + i, g)
    prev = lambda b, g, i: (g, w0 + b * w_blocks + jnp.maximum(i * half - 1, 0), 0)
    cur = lambda b, g, i: (g, q0 + b * q_blocks + i, 0)
    nxt = lambda b, g, i: (g, w0 + b * w_blocks + jnp.minimum((i + 1) * half, w_blocks - 1), 0)
    ctx = lambda b, g, i: (g, b, 0)
    kv_specs = [
        pl.BlockSpec((None, WINDOW, HEAD_DIM), prev),
        pl.BlockSpec((None, Q_TILE, HEAD_DIM), cur),
        pl.BlockSpec((None, WINDOW, HEAD_DIM), nxt),
    ]
    return pl.pallas_call(
        functools.partial(_attn_window_kernel, lat_seq=lat_seq),
        grid=(n_lat_seq, N_KV_HEADS, q_blocks),
        in_specs=[
            pl.BlockSpec(memory_space=pltpu.SMEM),
            pl.BlockSpec((Q_TILE, gw), qmap),
            *kv_specs, *kv_specs,
            pl.BlockSpec((None, past, HEAD_DIM), ctx),
            pl.BlockSpec((None, past, HEAD_DIM), ctx),
        ],
        out_specs=pl.BlockSpec((Q_TILE, gw), lambda b, g, i: (b * q_blocks + i, g)),
        out_shape=jax.ShapeDtypeStruct((n_lat_seq * lat_seq, d), BF16),
        compiler_params=_params(("arbitrary", "arbitrary", "arbitrary")),
    )(sink, q, kh, kh, kh, vh, vh, vh, kc, vc)


def _proj_kernel(oc_ref, ol_ref, w_ref, x_ref, gate_ref, out_ref, *, n_ctx_tiles):
    def project(o_ref):
        y = jnp.dot(o_ref[...], w_ref[...], preferred_element_type=F32)
        out_ref[...] = x_ref[...] + gate_ref[...] * y

    @pl.when(pl.program_id(0) < n_ctx_tiles)
    def _():
        project(oc_ref)

    @pl.when(pl.program_id(0) >= n_ctx_tiles)
    def _():
        project(ol_ref)


def _proj(o_ctx, o_lat, w, x, mods, *, tm, n_ctx_tiles, tiles_per_seq, n_ctx_row):
    t, d = x.shape
    mod = functools.partial(_mod_spec, n_ctx_tiles=n_ctx_tiles, tiles_per_seq=tiles_per_seq,
                            n_ctx_row=n_ctx_row)
    row = lambda i: (i, 0)
    return pl.pallas_call(
        functools.partial(_proj_kernel, n_ctx_tiles=n_ctx_tiles),
        grid=(t // tm,),
        in_specs=[
            pl.BlockSpec((tm, d), lambda i: (jnp.minimum(i, n_ctx_tiles - 1), 0)),
            pl.BlockSpec((tm, d), lambda i: (jnp.maximum(i - n_ctx_tiles, 0), 0)),
            pl.BlockSpec((d, d), lambda i: (0, 0)),
            pl.BlockSpec((tm, d), row),
            pl.BlockSpec((None, 1, d), mod(2)),
        ],
        out_specs=pl.BlockSpec((tm, d), row),
        out_shape=jax.ShapeDtypeStruct((t, d), F32),
        compiler_params=_params(("arbitrary",)),
    )(o_ctx, o_lat, w, x, mods)


def _swiglu_act(h, wg, wu):
    g = jnp.dot(h, wg, preferred_element_type=F32)
    u = jnp.dot(h, wu, preferred_element_type=F32)
    return (g / (1.0 + jnp.exp(-g))) * u


def _ffn_kernel(x_ref, sh_ref, sc_ref, gate_ref, g_ref, wg_ref, wu_ref, wd_ref, out_ref, h_ref, acc_ref):
    j = pl.program_id(1)

    @pl.when(j == 0)
    def _():
        h_ref[...] = _norm_mod(x_ref[...], g_ref[...], sh_ref[...], sc_ref[...]).astype(BF16)
        acc_ref[...] = jnp.zeros_like(acc_ref)

    a = _swiglu_act(h_ref[...], wg_ref[...], wu_ref[...])
    acc_ref[...] += jnp.dot(a.astype(BF16), wd_ref[...], preferred_element_type=F32)

    @pl.when(j == pl.num_programs(1) - 1)
    def _():
        out_ref[...] = x_ref[...] + gate_ref[...] * acc_ref[...]


def _ffn(x, mods, g, w_up, w_down, *, tm, tf, n_ctx_tiles, tiles_per_seq, n_ctx_row):
    t, d = x.shape
    d_ff = w_down.shape[0]
    nf = d_ff // tf
    mod = functools.partial(_mod_spec, n_ctx_tiles=n_ctx_tiles, tiles_per_seq=tiles_per_seq,
                            n_ctx_row=n_ctx_row)
    row = lambda i, j: (i, 0)
    return pl.pallas_call(
        _ffn_kernel,
        grid=(t // tm, nf),
        in_specs=[
            pl.BlockSpec((tm, d), row),
            pl.BlockSpec((None, 1, d), mod(3)),
            pl.BlockSpec((None, 1, d), mod(4)),
            pl.BlockSpec((None, 1, d), mod(5)),
            pl.BlockSpec((1, d), lambda i, j: (0, 0)),
            pl.BlockSpec((d, tf), lambda i, j: (0, j)),
            pl.BlockSpec((d, tf), lambda i, j: (0, nf + j)),
            pl.BlockSpec((tf, d), lambda i, j: (j, 0)),
        ],
        out_specs=pl.BlockSpec((tm, d), row),
        out_shape=jax.ShapeDtypeStruct((t, d), F32),
        scratch_shapes=[pltpu.VMEM((tm, d), BF16), pltpu.VMEM((tm, d), F32)],
        compiler_params=_params(("arbitrary", "arbitrary")),
    )(x, mods, mods, mods, g, w_up, w_up, w_down)


META_IDX, META_GATE, META_RANK = 0, 2, 4


def _router_kernel(x_ref, sh_ref, sc_ref, g_ref, wr_ref, tri_ref, meta_ref, cnt_ref, run_ref, *, n_experts):
    @pl.when(pl.program_id(0) == 0)
    def _():
        run_ref[...] = jnp.zeros_like(run_ref)

    h = _norm_mod(x_ref[...], g_ref[...], sh_ref[...], sc_ref[...])
    logits = jnp.dot(h, wr_ref[...], preferred_element_type=F32, precision=lax.Precision.HIGHEST)
    lane = lax.broadcasted_iota(jnp.int32, logits.shape, 1).astype(F32)
    logits = jnp.where(lane < n_experts, logits, -jnp.inf)
    top1 = logits.max(axis=-1, keepdims=True)
    idx1 = jnp.where(logits == top1, lane, float(LANES)).min(axis=-1, keepdims=True)
    rest = jnp.where(lane == idx1, -jnp.inf, logits)
    top2 = rest.max(axis=-1, keepdims=True)
    idx2 = jnp.where(rest == top2, lane, float(LANES)).min(axis=-1, keepdims=True)
    e = jnp.exp(top2 - top1)
    g1 = 1.0 / (1.0 + e)
    g2 = e / (1.0 + e)

    sel1 = lane == idx1
    sel2 = lane == idx2
    sel = jnp.where(sel1, 1.0, 0.0) + jnp.where(sel2, 1.0, 0.0)
    before = jnp.dot(tri_ref[...], sel.astype(BF16), preferred_element_type=F32) + run_ref[...]
    r1 = jnp.where(sel1, before, 0.0).sum(axis=-1, keepdims=True)
    r2 = jnp.where(sel2, before, 0.0).sum(axis=-1, keepdims=True)
    run_ref[...] += sel.sum(axis=0, keepdims=True)
    cnt_ref[...] = run_ref[...]

    meta = jnp.zeros_like(logits)
    for off, (a, b) in ((META_IDX, (idx1, idx2)), (META_GATE, (g1, g2)), (META_RANK, (r1, r2))):
        meta = jnp.where(lane == off, a, meta)
        meta = jnp.where(lane == off + 1, b, meta)
    meta_ref[...] = meta


def _router(x, mods, g, wr, tri, *, n_experts, tm, n_ctx_tiles, tiles_per_seq, n_ctx_row):
    t, d = x.shape
    mod = functools.partial(_mod_spec, n_ctx_tiles=n_ctx_tiles, tiles_per_seq=tiles_per_seq,
                            n_ctx_row=n_ctx_row)
    row = lambda i: (i, 0)
    fixed = lambda i: (0, 0)
    return pl.pallas_call(
        functools.partial(_router_kernel, n_experts=n_experts),
        grid=(t // tm,),
        in_specs=[
            pl.BlockSpec((tm, d), row),
            pl.BlockSpec((None, 1, d), mod(3)),
            pl.BlockSpec((None, 1, d), mod(4)),
            pl.BlockSpec((1, d), fixed),
            pl.BlockSpec((d, LANES), fixed),
            pl.BlockSpec((tm, tm), fixed),
        ],
        out_specs=[pl.BlockSpec((tm, LANES), row), pl.BlockSpec((1, LANES), fixed)],
        out_shape=[jax.ShapeDtypeStruct((t, LANES), F32), jax.ShapeDtypeStruct((1, LANES), F32)],
        scratch_shapes=[pltpu.VMEM((1, LANES), F32)],
        compiler_params=_params(("arbitrary",)),
    )(x, mods, mods, g, wr, tri)


def _row_copy(src_ref, src_row, dst_ref, dst_row, sem):
    return pltpu.make_async_copy(src_ref.at[pl.ds(src_row, 1)], dst_ref.at[pl.ds(dst_row, 1)], sem)


def _dispatch_kernel(pos_ref, x_ref, sh_ref, sc_ref, g_ref, hs_init_ref, hs_ref, h_ref, sem, *, tm):
    del hs_init_ref
    i = pl.program_id(0)
    slot = i % 2

    def wait_slot(s):
        def body(r, carry):
            for _ in range(TOP_K):
                _row_copy(h_ref.at[s], 0, hs_ref, 0, sem.at[s]).wait()
            return carry
        lax.fori_loop(0, tm, body, 0)

    @pl.when(i >= 2)
    def _():
        wait_slot(slot)

    h_ref[slot] = _norm_mod(x_ref[...], g_ref[...], sh_ref[...], sc_ref[...])

    def send(r, carry):
        base = (i * tm + r) * TOP_K
        for k in range(TOP_K):
            _row_copy(h_ref.at[slot], r, hs_ref, pos_ref[base + k], sem.at[slot]).start()
        return carry
    lax.fori_loop(0, tm, send, 0, unroll=8)

    @pl.when(i == pl.num_programs(0) - 1)
    def _():
        wait_slot(slot)

        @pl.when(i >= 1)
        def _():
            wait_slot(1 - slot)


def _dispatch(pos, x, mods, g, hs_init, *, tm, n_ctx_tiles, tiles_per_seq, n_ctx_row):
    t, d = x.shape
    mod = functools.partial(_mod_spec, n_ctx_tiles=n_ctx_tiles, tiles_per_seq=tiles_per_seq,
                            n_ctx_row=n_ctx_row)
    return pl.pallas_call(
        functools.partial(_dispatch_kernel, tm=tm),
        grid_spec=pltpu.PrefetchScalarGridSpec(
            num_scalar_prefetch=1,
            grid=(t // tm,),
            in_specs=[
                pl.BlockSpec((tm, d), lambda i, pos: (i, 0)),
                pl.BlockSpec((None, 1, d), mod(3)),
                pl.BlockSpec((None, 1, d), mod(4)),
                pl.BlockSpec((1, d), lambda i, pos: (0, 0)),
                pl.BlockSpec(memory_space=pl.ANY),
            ],
            out_specs=pl.BlockSpec(memory_space=pl.ANY),
            scratch_shapes=[pltpu.VMEM((2, tm, d), F32), pltpu.SemaphoreType.DMA((2,))],
        ),
        out_shape=jax.ShapeDtypeStruct(hs_init.shape, F32),
        input_output_aliases={5: 0},
        compiler_params=_params(("arbitrary",)),
    )(pos, x, mods, mods, g, hs_init)


def _expert_kernel(te_ref, tv_ref, hs_ref, wg_ref, wu_ref, wd_ref, ys_ref, h_ref, acc_ref):
    del te_ref
    j = pl.program_id(1)

    @pl.when(tv_ref[pl.program_id(0)] != 0)
    def _():
        @pl.when(j == 0)
        def _():
            h_ref[...] = hs_ref[...].astype(BF16)
            acc_ref[...] = jnp.zeros_like(acc_ref)

        a = _swiglu_act(h_ref[...], wg_ref[...], wu_ref[...])
        acc_ref[...] += jnp.dot(a.astype(BF16), wd_ref[...], preferred_element_type=F32)

        @pl.when(j == pl.num_programs(1) - 1)
        def _():
            ys_ref[...] = acc_ref[...]

    @pl.when((tv_ref[pl.program_id(0)] == 0) & (j == 0))
    def _():
        ys_ref[...] = jnp.zeros_like(ys_ref)


def _experts(tile_expert, tile_valid, hs, w_up, w_down, *, tm, tf):
    p, d = hs.shape
    d_ff = w_down.shape[1]
    nf = d_ff // tf
    return pl.pallas_call(
        _expert_kernel,
        grid_spec=pltpu.PrefetchScalarGridSpec(
            num_scalar_prefetch=2,
            grid=(p // tm, nf),
            in_specs=[
                pl.BlockSpec((tm, d), lambda r, j, te, tv: (r, 0)),
                pl.BlockSpec((None, d, tf), lambda r, j, te, tv: (te[r], 0, j)),
                pl.BlockSpec((None, d, tf), lambda r, j, te, tv: (te[r], 0, nf + j)),
                pl.BlockSpec((None, tf, d), lambda r, j, te, tv: (te[r], j, 0)),
            ],
            out_specs=pl.BlockSpec((tm, d), lambda r, j, te, tv: (r, 0)),
            scratch_shapes=[pltpu.VMEM((tm, d), BF16), pltpu.VMEM((tm, d), F32)],
        ),
        out_shape=jax.ShapeDtypeStruct((p, d), F32),
        compiler_params=_params(("arbitrary", "arbitrary")),
    )(tile_expert, tile_valid, hs, w_up, w_up, w_down)


def _combine_kernel(pos_ref, ys_ref, meta_ref, x_ref, gate_ref, out_ref, y_ref, sem, *, tm):
    i = pl.program_id(0)
    slot = i % 2

    def fetch(step, s):
        def body(r, carry):
            base = (step * tm + r) * TOP_K
            for k in range(TOP_K):
                _row_copy(ys_ref, pos_ref[base + k], y_ref.at[s, k], r, sem.at[s]).start()
            return carry
        lax.fori_loop(0, tm, body, 0, unroll=8)

    @pl.when(i == 0)
    def _():
        fetch(0, 0)

    @pl.when(i + 1 < pl.num_programs(0))
    def _():
        fetch(i + 1, 1 - slot)

    def wait_row(r, carry):
        for k in range(TOP_K):
            _row_copy(ys_ref, 0, y_ref.at[slot, k], 0, sem.at[slot]).wait()
        return carry
    lax.fori_loop(0, tm, wait_row, 0)

    meta = meta_ref[...]
    f = meta[:, META_GATE:META_GATE + 1] * y_ref[slot, 0] + meta[:, META_GATE + 1:META_GATE + 2] * y_ref[slot, 1]
    out_ref[...] = x_ref[...] + gate_ref[...] * f


def _combine(pos, ys, meta, x, mods, *, tm, n_ctx_tiles, tiles_per_seq, n_ctx_row):
    t, d = x.shape
    mod = functools.partial(_mod_spec, n_ctx_tiles=n_ctx_tiles, tiles_per_seq=tiles_per_seq,
                            n_ctx_row=n_ctx_row)
    row = lambda i, pos: (i, 0)
    return pl.pallas_call(
        functools.partial(_combine_kernel, tm=tm),
        grid_spec=pltpu.PrefetchScalarGridSpec(
            num_scalar_prefetch=1,
            grid=(t // tm,),
            in_specs=[
                pl.BlockSpec(memory_space=pl.ANY),
                pl.BlockSpec((tm, LANES), row),
                pl.BlockSpec((tm, d), row),
                pl.BlockSpec((None, 1, d), mod(5)),
            ],
            out_specs=pl.BlockSpec((tm, d), row),
            scratch_shapes=[pltpu.VMEM((2, TOP_K, tm, d), F32), pltpu.SemaphoreType.DMA((2,))],
        ),
        out_shape=jax.ShapeDtypeStruct((t, d), F32),
        compiler_params=_params(("arbitrary",)),
    )(pos, ys, meta, x, mods)


def _moe(x, mods, g, w_router, w_up, w_down, *, tiling, tf):
    t, d = x.shape
    n_experts = w_router.shape[-1]
    tm = tiling["tm"]
    wr = jnp.zeros((d, LANES), F32).at[:, :n_experts].set(w_router)
    tri = (jnp.arange(tm)[:, None] > jnp.arange(tm)[None, :]).astype(BF16)
    meta, counts = _router(x, mods, g, wr, tri, n_experts=n_experts, **tiling)

    counts = counts[0, :n_experts].astype(jnp.int32)
    padded = ((counts + tm - 1) // tm) * tm
    ends = jnp.cumsum(padded)
    starts = ends - padded
    idx = meta[:, META_IDX:META_IDX + TOP_K].astype(jnp.int32)
    rank = meta[:, META_RANK:META_RANK + TOP_K].astype(jnp.int32)
    pos = (starts[idx] + rank).reshape(t * TOP_K)
    n_tiles = (t * TOP_K) // tm + n_experts
    tile_start = jnp.arange(n_tiles, dtype=jnp.int32) * tm
    tile_expert = jnp.minimum((tile_start[:, None] >= ends[None, :]).sum(axis=-1), n_experts - 1).astype(jnp.int32)
    tile_valid = (tile_start < ends[-1]).astype(jnp.int32)

    hs = _dispatch(pos, x, mods, g, jnp.zeros((n_tiles * tm, d), F32), **tiling)
    ys = _experts(tile_expert, tile_valid, hs, w_up, w_down, tm=tm, tf=tf)
    return _combine(pos, ys, meta, x, mods, **tiling)


def _final_norm_kernel(x_ref, g_ref, o_ref):
    x = x_ref[...]
    ms = jnp.mean(x * x, axis=-1, keepdims=True)
    o_ref[...] = (x * lax.rsqrt(ms + EPS)) * g_ref[...]


def _final_norm(x, g, *, tm):
    t, d = x.shape
    return pl.pallas_call(
        _final_norm_kernel,
        grid=(t // tm,),
        in_specs=[pl.BlockSpec((tm, d), lambda i: (i, 0)), pl.BlockSpec((1, d), lambda i: (0, 0))],
        out_specs=pl.BlockSpec((tm, d), lambda i: (i, 0)),
        out_shape=jax.ShapeDtypeStruct((t, d), F32),
        compiler_params=_params(("arbitrary",)),
    )(x, g)


def _rope_tables(n_ctx_tokens, n_lat_seq, lat_seq):
    pos = jnp.arange(lat_seq)
    rows = (pos // GRID_W).astype(F32)
    cols = (pos % GRID_W).astype(F32)
    pairs_per_axis = HEAD_DIM // 4
    inv = ROPE_THETA ** (-jnp.arange(pairs_per_axis, dtype=F32) / pairs_per_axis)
    ang = jnp.concatenate([rows[:, None] * inv, cols[:, None] * inv], axis=-1)
    cos = jnp.repeat(jnp.cos(ang), 2, axis=-1)
    sin = jnp.repeat(jnp.sin(ang), 2, axis=-1) * jnp.tile(jnp.array([-1.0, 1.0], F32), HEAD_DIM // 2)
    reps = LANES // HEAD_DIM
    cos = jnp.tile(cos, (n_lat_seq, reps))
    sin = jnp.tile(sin, (n_lat_seq, reps))
    cos = jnp.concatenate([jnp.ones((n_ctx_tokens, LANES), F32), cos], axis=0)
    sin = jnp.concatenate([jnp.zeros((n_ctx_tokens, LANES), F32), sin], axis=0)
    return cos, sin


def kernel(x_prompt, x_sample, cache_k, cache_v, c, c_ctx, w_ada, b_ada, norm_mix, norm_ffn, norm_final,
           w_qkv, w_o, q_norm, k_norm, sinks, w_ffn_up, w_ffn_down, w_router, w_exp_up, w_exp_down):
    n_ctx_seq, seq, d = x_prompt.shape
    n_lat_seq, lat_seq, _ = x_sample.shape
    depth = w_ada.shape[0]
    past = cache_k.shape[2]
    n_experts = w_router.shape[-1]
    kv_dim = N_KV_HEADS * HEAD_DIM
    n_ctx_tokens = n_ctx_seq * seq
    n_lat_tokens = n_lat_seq * lat_seq
    assert d == N_HEADS * HEAD_DIM and seq == Q_TILE and lat_seq % Q_TILE == 0
    assert n_ctx_tokens % lat_seq == 0 and n_lat_seq < MOD_ROWS and n_experts <= LANES

    tm = 2 * Q_TILE if (lat_seq % (2 * Q_TILE) == 0 and n_ctx_tokens % (2 * Q_TILE) == 0) else Q_TILE
    tiling = dict(tm=tm, n_ctx_tiles=n_ctx_tokens // tm, tiles_per_seq=lat_seq // tm, n_ctx_row=n_lat_seq)

    x = jnp.concatenate([x_prompt.reshape(n_ctx_tokens, d), x_sample.reshape(n_lat_tokens, d)], axis=0)

    cvecs = jnp.zeros((MOD_ROWS, d), F32).at[:n_lat_seq].set(c).at[n_lat_seq].set(c_ctx)
    mods_all = _ada_mods(cvecs, w_ada, b_ada)
    mods_all = mods_all.reshape(depth, MOD_ROWS * N_MOD, 1, d)

    cos, sin = _rope_tables(n_ctx_tokens, n_lat_seq, lat_seq)
    head_id = jnp.arange(MXU_DIM) // HEAD_DIM
    ones = (head_id[:, None] == head_id[None, :]).astype(BF16)

    kc_all = cache_k.transpose(1, 3, 0, 2, 4).reshape(depth, N_KV_HEADS, n_lat_seq * past, HEAD_DIM).astype(BF16)
    vc_all = cache_v.transpose(1, 3, 0, 2, 4).reshape(depth, N_KV_HEADS, n_lat_seq * past, HEAD_DIM).astype(BF16)

    new_k = []
    new_v = []
    for l in range(depth):
        idx = l // 2
        use_a = (l % 2 == 0)
        mods = mods_all[l]
        w_l = w_qkv[l].astype(BF16)
        if use_a:
            qn = jnp.tile(q_norm[idx], d // HEAD_DIM).reshape(1, d)
            kn = jnp.tile(k_norm[idx], N_KV_HEADS).reshape(1, kv_dim)
            sink = jnp.zeros((N_HEADS,), F32)
        else:
            qn = jnp.ones((1, d), F32)
            kn = jnp.ones((1, kv_dim), F32)
            sink = sinks[idx].astype(F32)

        q, kf, vf, kh, vh = _qkv(x, mods, norm_mix[l].reshape(1, d), w_l, qn, kn, cos, sin, ones,
                                 use_a=use_a, **tiling)
        new_k.append(kf[:n_ctx_tokens].reshape(n_ctx_seq, seq, N_KV_HEADS, HEAD_DIM))
        new_v.append(vf[:n_ctx_tokens].reshape(n_ctx_seq, seq, N_KV_HEADS, HEAD_DIM))

        o_ctx = _attn_ctx(sink, q, kh, vh, n_ctx_seq=n_ctx_seq, seq=seq, has_sink=not use_a)
        lat = dict(n_ctx_tokens=n_ctx_tokens, n_lat_seq=n_lat_seq, lat_seq=lat_seq, past=past)
        if use_a:
            o_lat = _attn_global(q, kh, vh, kc_all[l], vc_all[l], **lat)
        else:
            o_lat = _attn_window(sink, q, kh, vh, kc_all[l], vc_all[l], **lat)
        x = _proj(o_ctx, o_lat, w_o[l].astype(BF16), x, mods, **tiling)

        if use_a:
            d_ff = w_ffn_down.shape[1]
            tf = _pick_tile(d_ff, (1408, 1024, 512, 256, 128))
            x = _ffn(x, mods, norm_ffn[l].reshape(1, d), w_ffn_up[idx].astype(BF16),
                     w_ffn_down[idx].astype(BF16), tf=tf, **tiling)
        else:
            d_ffe = w_exp_down.shape[2]
            tf = _pick_tile(d_ffe, (1792, 1024, 512, 256, 128))
            x = _moe(x, mods, norm_ffn[l].reshape(1, d), w_router[idx], w_exp_up[idx].astype(BF16),
                     w_exp_down[idx].astype(BF16), tiling=tiling, tf=tf)

    y = _final_norm(x, norm_final.reshape(1, d), tm=tm)
    y_prompt = y[:n_ctx_tokens].reshape(n_ctx_seq, seq, d)
    y_sample = y[n_ctx_tokens:].reshape(n_lat_seq, lat_seq, d)
    return (y_prompt, y_sample, jnp.stack(new_k, axis=1), jnp.stack(new_v, axis=1))
```

```python
import functools

import jax
import jax.numpy as jnp
from jax import lax
from jax.experimental import pallas as pl
from jax.experimental.pallas import tpu as pltpu

N_HEADS = 16
N_KV_HEADS = 4
GROUP = N_HEADS // N_KV_HEADS
HEAD_DIM = 64
GRID_W = 64
WINDOW = 128
ROPE_THETA = 10000.0
N_MOD = 6
TOP_K = 2
EPS = 1e-6
NEG_INF = -1e30
ATTN_SCALE = HEAD_DIM ** -0.5

LANES = 128
SUBLANES = 8
MXU_DIM = 256
MOD_ROWS = 16
Q_TILE = 256
VMEM_LIMIT = 56 * 1024 * 1024

F32 = jnp.float32
BF16 = jnp.bfloat16
NT_DIMS = (((1,), (1,)), ((), ()))


def _params(sem, vmem=VMEM_LIMIT):
    return pltpu.CompilerParams(dimension_semantics=sem, vmem_limit_bytes=vmem)


def _pick_tile(n, candidates):
    for c in candidates:
        if n % c == 0:
            return c
    return n


def _ada_kernel(c_ref, w_ref, b_ref, o_ref):
    c = c_ref[...]
    a = c / (1.0 + jnp.exp(-c))
    o_ref[...] = jnp.dot(a, w_ref[...], preferred_element_type=F32,
                         precision=lax.Precision.HIGHEST) + b_ref[...]


def _ada_mods(cvecs, w_ada, b_ada):
    depth, d, n = w_ada.shape
    tn = _pick_tile(n, (1536, 1024, 512, 256, 128))
    return pl.pallas_call(
        _ada_kernel,
        grid=(depth, n // tn),
        in_specs=[
            pl.BlockSpec((MOD_ROWS, d), lambda l, j: (0, 0)),
            pl.BlockSpec((None, d, tn), lambda l, j: (l, 0, j)),
            pl.BlockSpec((None, 1, tn), lambda l, j: (l, 0, j)),
        ],
        out_specs=pl.BlockSpec((None, MOD_ROWS, tn), lambda l, j: (l, 0, j)),
        out_shape=jax.ShapeDtypeStruct((depth, MOD_ROWS, n), F32),
        compiler_params=_params(("arbitrary", "arbitrary")),
    )(cvecs, w_ada, b_ada.reshape(depth, 1, n))


def _norm_mod(x, g, shift, scale):
    ms = jnp.mean(x * x, axis=-1, keepdims=True)
    y = (x * lax.rsqrt(ms + EPS)) * g
    return y * (1.0 + scale) + shift


class _Tiling:
    def __init__(self, tm, n_ctx_tokens, lat_seq, n_lat_seq, d):
        self.tm = tm
        self.n_ctx_tiles = n_ctx_tokens // tm
        self.tiles_per_seq = lat_seq // tm
        self.ctx_row = n_lat_seq
        self.d = d

    def mod_spec(self, layer, which):
        def index(i, *_):
            row = jnp.where(i < self.n_ctx_tiles, self.ctx_row, (i - self.n_ctx_tiles) // self.tiles_per_seq)
            return (layer, row * N_MOD + which, 0, 0)
        return pl.BlockSpec((None, None, 1, self.d), index)


def _layer_vec_spec(layer, d):
    return pl.BlockSpec((None, 1, d), lambda *_: (layer, 0, 0))


def _qkv_kernel(x_ref, sh_ref, sc_ref, g_ref, w_ref, qn_ref, kn_ref, cos_ref, sin_ref, ones_ref,
                q_ref, kf_ref, vf_ref, kh_ref, vh_ref, h_ref, *, use_a, d_model, n_ctx_tiles):
    q_dim = d_model
    kv_dim = N_KV_HEADS * HEAD_DIM
    is_ctx = pl.program_id(0) < n_ctx_tiles
    h_ref[...] = _norm_mod(x_ref[...], g_ref[...], sh_ref[...], sc_ref[...]).astype(BF16)

    cos = cos_ref[...]
    sin = sin_ref[...]
    even = (lax.broadcasted_iota(jnp.int32, cos.shape, 1) % 2) == 0

    def project(col0, gain):
        xc = jnp.dot(h_ref[...], w_ref[:, col0:col0 + MXU_DIM], preferred_element_type=F32)
        if use_a and gain is not None:
            ss = jnp.dot((xc * xc).astype(BF16), ones_ref[...], preferred_element_type=F32)
            xc = xc * lax.rsqrt(ss * (1.0 / HEAD_DIM) + EPS) * gain
        return xc

    def rope(xc):
        cols = []
        for l0 in range(0, MXU_DIM, LANES):
            xl = xc[:, l0:l0 + LANES]
            swapped = jnp.where(even, pltpu.roll(xl, LANES - 1, 1), pltpu.roll(xl, 1, 1))
            cols.append(xl * cos + swapped * sin)
        return jnp.concatenate(cols, axis=-1)

    for c0 in range(0, q_dim, MXU_DIM):
        qc = rope(project(c0, qn_ref[:, c0:c0 + MXU_DIM]))
        q_ref[:, c0:c0 + MXU_DIM] = (qc * ATTN_SCALE).astype(BF16)
    k = rope(project(q_dim, kn_ref[...]))
    v = project(q_dim + kv_dim, None)
    for hh in range(N_KV_HEADS):
        kh_ref[hh] = k[:, hh * HEAD_DIM:(hh + 1) * HEAD_DIM].astype(BF16)
        vh_ref[hh] = v[:, hh * HEAD_DIM:(hh + 1) * HEAD_DIM].astype(BF16)

    @pl.when(is_ctx)
    def _():
        kf_ref[...] = k
        vf_ref[...] = v


def _qkv(x, mods, g_all, w_all, qn, kn, cos, sin, ones, *, layer, use_a, til):
    t, d = x.shape
    tm = til.tm
    kv_dim = N_KV_HEADS * HEAD_DIM
    n_out = w_all.shape[2]
    n_ctx_tiles = til.n_ctx_tiles
    row = lambda i: (i, 0)
    fixed = lambda i: (0, 0)
    ctx_row = lambda i: (jnp.minimum(i, n_ctx_tiles - 1), 0)
    return pl.pallas_call(
        functools.partial(_qkv_kernel, use_a=use_a, d_model=d, n_ctx_tiles=n_ctx_tiles),
        grid=(t // tm,),
        in_specs=[
            pl.BlockSpec((tm, d), row),
            til.mod_spec(layer, 0),
            til.mod_spec(layer, 1),
            _layer_vec_spec(layer, d),
            pl.BlockSpec((None, d, n_out), lambda i: (layer, 0, 0)),
            pl.BlockSpec((1, d), fixed),
            pl.BlockSpec((1, kv_dim), fixed),
            pl.BlockSpec((tm, LANES), row),
            pl.BlockSpec((tm, LANES), row),
            pl.BlockSpec((MXU_DIM, MXU_DIM), fixed),
        ],
        out_specs=[
            pl.BlockSpec((tm, d), row),
            pl.BlockSpec((tm, kv_dim), ctx_row),
            pl.BlockSpec((tm, kv_dim), ctx_row),
            pl.BlockSpec((N_KV_HEADS, tm, HEAD_DIM), lambda i: (0, i, 0)),
            pl.BlockSpec((N_KV_HEADS, tm, HEAD_DIM), lambda i: (0, i, 0)),
        ],
        out_shape=[
            jax.ShapeDtypeStruct((t, d), BF16),
            jax.ShapeDtypeStruct((n_ctx_tiles * tm, kv_dim), F32),
            jax.ShapeDtypeStruct((n_ctx_tiles * tm, kv_dim), F32),
            jax.ShapeDtypeStruct((N_KV_HEADS, t, HEAD_DIM), BF16),
            jax.ShapeDtypeStruct((N_KV_HEADS, t, HEAD_DIM), BF16),
        ],
        scratch_shapes=[pltpu.VMEM((tm, d), BF16)],
        compiler_params=_params(("arbitrary",)),
    )(x, mods, mods, g_all, w_all, qn, kn, cos, sin, ones)


def _softmax_pv(score_parts, value_parts, sink):
    m = score_parts[0].max(axis=-1, keepdims=True)
    for s in score_parts[1:]:
        m = jnp.maximum(m, s.max(axis=-1, keepdims=True))
    if sink is not None:
        m = jnp.maximum(m, sink)
    denom = None
    acc = None
    for s, v in zip(score_parts, value_parts):
        p = jnp.exp(s - m)
        ps = p.sum(axis=-1, keepdims=True)
        pv = jnp.dot(p.astype(BF16), v, preferred_element_type=F32)
        denom = ps if denom is None else denom + ps
        acc = pv if acc is None else acc + pv
    if sink is not None:
        denom = denom + jnp.exp(sink - m)
    return acc / denom


def _attn_ctx_kernel(sink_ref, q_ref, k_ref, v_ref, o_ref, *, has_sink):
    for head in range(N_HEADS):
        cols = slice(head * HEAD_DIM, (head + 1) * HEAD_DIM)
        s = lax.dot_general(q_ref[:, cols], k_ref[head // GROUP], NT_DIMS, preferred_element_type=F32)
        sink = sink_ref[head] if has_sink else None
        o_ref[:, cols] = _softmax_pv([s], [v_ref[head // GROUP]], sink).astype(o_ref.dtype)


def _attn_ctx(sink, q, kh, vh, *, n_ctx_seq, seq, has_sink):
    t, d = q.shape
    return pl.pallas_call(
        functools.partial(_attn_ctx_kernel, has_sink=has_sink),
        grid=(n_ctx_seq,),
        in_specs=[
            pl.BlockSpec(memory_space=pltpu.SMEM),
            pl.BlockSpec((seq, d), lambda b: (b, 0)),
            pl.BlockSpec((N_KV_HEADS, seq, HEAD_DIM), lambda b: (0, b, 0)),
            pl.BlockSpec((N_KV_HEADS, seq, HEAD_DIM), lambda b: (0, b, 0)),
        ],
        out_specs=pl.BlockSpec((seq, d), lambda b: (b, 0)),
        out_shape=jax.ShapeDtypeStruct((n_ctx_seq * seq, d), BF16),
        compiler_params=_params(("arbitrary",)),
    )(sink, q, kh, vh)


def _attn_global_kernel(q_ref, k_ref, v_ref, kc_ref, vc_ref, o_ref):
    k = k_ref[...]
    v = v_ref[...]
    kc = kc_ref[...]
    vc = vc_ref[...]
    for j in range(GROUP):
        qj = q_ref[:, j * HEAD_DIM:(j + 1) * HEAD_DIM]
        s_lat = lax.dot_general(qj, k, NT_DIMS, preferred_element_type=F32)
        s_ctx = lax.dot_general(qj, kc, NT_DIMS, preferred_element_type=F32)
        o = _softmax_pv([s_lat, s_ctx], [v, vc], None)
        o_ref[:, j * HEAD_DIM:(j + 1) * HEAD_DIM] = o.astype(o_ref.dtype)


def _cache_spec(layer, past):
    return pl.BlockSpec((None, None, past, HEAD_DIM), lambda b, g, i: (layer, g, b, 0))


def _attn_global(q, kh, vh, kc_all, vc_all, *, layer, n_ctx_tokens, n_lat_seq, lat_seq, past):
    t, d = q.shape
    gw = GROUP * HEAD_DIM
    q_blocks = lat_seq // Q_TILE
    q0 = n_ctx_tokens // Q_TILE
    s0 = n_ctx_tokens // lat_seq
    return pl.pallas_call(
        _attn_global_kernel,
        grid=(n_lat_seq, N_KV_HEADS, q_blocks),
        in_specs=[
            pl.BlockSpec((Q_TILE, gw), lambda b, g, i: (q0 + b * q_blocks + i, g)),
            pl.BlockSpec((None, lat_seq, HEAD_DIM), lambda b, g, i: (g, s0 + b, 0)),
            pl.BlockSpec((None, lat_seq, HEAD_DIM), lambda b, g, i: (g, s0 + b, 0)),
            _cache_spec(layer, past),
            _cache_spec(layer, past),
        ],
        out_specs=pl.BlockSpec((Q_TILE, gw), lambda b, g, i: (b * q_blocks + i, g)),
        out_shape=jax.ShapeDtypeStruct((n_lat_seq * lat_seq, d), BF16),
        compiler_params=_params(("arbitrary", "arbitrary", "arbitrary")),
    )(q, kh, vh, kc_all, vc_all)


def _attn_window_kernel(sink_ref, q_ref, kp_ref, kcur_ref, kn_ref, vp_ref, vcur_ref, vn_ref,
                        kc_ref, vc_ref, o_ref, *, lat_seq):
    grp = pl.program_id(1)
    start = pl.program_id(2) * Q_TILE
    span = Q_TILE + 2 * WINDOW
    kw = jnp.concatenate([kp_ref[...], kcur_ref[...], kn_ref[...]], axis=0)
    vw = jnp.concatenate([vp_ref[...], vcur_ref[...], vn_ref[...]], axis=0)
    kc = kc_ref[...]
    vc = vc_ref[...]
    qpos = start + lax.broadcasted_iota(jnp.int32, (Q_TILE, span), 0)
    kpos = start - WINDOW + lax.broadcasted_iota(jnp.int32, (Q_TILE, span), 1)
    valid = (kpos >= 0) & (kpos < lat_seq) & (jnp.abs(qpos - kpos) <= WINDOW)
    for j in range(GROUP):
        qj = q_ref[:, j * HEAD_DIM:(j + 1) * HEAD_DIM]
        s_loc = lax.dot_general(qj, kw, NT_DIMS, preferred_element_type=F32)
        s_loc = jnp.where(valid, s_loc, NEG_INF)
        s_ctx = lax.dot_general(qj, kc, NT_DIMS, preferred_element_type=F32)
        o = _softmax_pv([s_loc, s_ctx], [vw, vc], sink_ref[grp * GROUP + j])
        o_ref[:, j * HEAD_DIM:(j + 1) * HEAD_DIM] = o.astype(o_ref.dtype)


def _attn_window(sink, q, kh, vh, kc_all, vc_all, *, layer, n_ctx_tokens, n_lat_seq, lat_seq, past):
    t, d = q.shape
    gw = GROUP * HEAD_DIM
    q_blocks = lat_seq // Q_TILE
    q0 = n_ctx_tokens // Q_TILE
    half = Q_TILE // WINDOW
    w0 = n_ctx_tokens // WINDOW
    w_blocks = lat_seq // WINDOW
    prev = lambda b, g, i: (g, w0 + b * w_blocks + jnp.maximum(i * half - 1, 0), 0)
    cur = lambda b, g, i: (g, q0 + b * q_blocks + i, 0)
    nxt = lambda b, g, i: (g, w0 + b * w_blocks + jnp.minimum((i + 1) * half, w_blocks - 1), 0)
    kv_specs = [
        pl.BlockSpec((None, WINDOW, HEAD_DIM), prev),
        pl.BlockSpec((None, Q_TILE, HEAD_DIM), cur),
        pl.BlockSpec((None, WINDOW, HEAD_DIM), nxt),
    ]
    return pl.pallas_call(
        functools.partial(_attn_window_kernel, lat_seq=lat_seq),
        grid=(n_lat_seq, N_KV_HEADS, q_blocks),
        in_specs=[
            pl.BlockSpec(memory_space=pltpu.SMEM),
            pl.BlockSpec((Q_TILE, gw), lambda b, g, i: (q0 + b * q_blocks + i, g)),
            *kv_specs, *kv_specs,
            _cache_spec(layer, past),
            _cache_spec(layer, past),
        ],
        out_specs=pl.BlockSpec((Q_TILE, gw), lambda b, g, i: (b * q_blocks + i, g)),
        out_shape=jax.ShapeDtypeStruct((n_lat_seq * lat_seq, d), BF16),
        compiler_params=_params(("arbitrary", "arbitrary", "arbitrary")),
    )(sink, q, kh, kh, kh, vh, vh, vh, kc_all, vc_all)


def _proj_kernel(oc_ref, ol_ref, w_ref, x_ref, gate_ref, out_ref, *, n_ctx_tiles):
    def project(o_ref):
        y = jnp.dot(o_ref[...], w_ref[...], preferred_element_type=F32)
        out_ref[...] = x_ref[...] + gate_ref[...] * y

    @pl.when(pl.program_id(0) < n_ctx_tiles)
    def _():
        project(oc_ref)

    @pl.when(pl.program_id(0) >= n_ctx_tiles)
    def _():
        project(ol_ref)


def _proj(o_ctx, o_lat, w_all, x, mods, *, layer, til):
    t, d = x.shape
    tm = til.tm
    n_ctx_tiles = til.n_ctx_tiles
    row = lambda i: (i, 0)
    return pl.pallas_call(
        functools.partial(_proj_kernel, n_ctx_tiles=n_ctx_tiles),
        grid=(t // tm,),
        in_specs=[
            pl.BlockSpec((tm, d), lambda i: (jnp.minimum(i, n_ctx_tiles - 1), 0)),
            pl.BlockSpec((tm, d), lambda i: (jnp.maximum(i - n_ctx_tiles, 0), 0)),
            pl.BlockSpec((None, d, d), lambda i: (layer, 0, 0)),
            pl.BlockSpec((tm, d), row),
            til.mod_spec(layer, 2),
        ],
        out_specs=pl.BlockSpec((tm, d), row),
        out_shape=jax.ShapeDtypeStruct((t, d), F32),
        compiler_params=_params(("arbitrary",)),
    )(o_ctx, o_lat, w_all, x, mods)


def _swiglu_act(h, wg, wu):
    g = jnp.dot(h, wg, preferred_element_type=F32)
    u = jnp.dot(h, wu, preferred_element_type=F32)
    return (g / (1.0 + jnp.exp(-g))) * u


def _ffn_kernel(x_ref, sh_ref, sc_ref, gate_ref, g_ref, wg_ref, wu_ref, wd_ref, out_ref, h_ref, acc_ref):
    j = pl.program_id(1)

    @pl.when(j == 0)
    def _():
        h_ref[...] = _norm_mod(x_ref[...], g_ref[...], sh_ref[...], sc_ref[...]).astype(BF16)
        acc_ref[...] = jnp.zeros_like(acc_ref)

    a = _swiglu_act(h_ref[...], wg_ref[...], wu_ref[...])
    acc_ref[...] += jnp.dot(a.astype(BF16), wd_ref[...], preferred_element_type=F32)

    @pl.when(j == pl.num_programs(1) - 1)
    def _():
        out_ref[...] = x_ref[...] + gate_ref[...] * acc_ref[...]


def _ffn(x, mods, g_all, w_up_all, w_down_all, *, layer, idx, til, tf):
    t, d = x.shape
    tm = til.tm
    d_ff = w_down_all.shape[1]
    nf = d_ff // tf
    row = lambda i, j: (i, 0)
    return pl.pallas_call(
        _ffn_kernel,
        grid=(t // tm, nf),
        in_specs=[
            pl.BlockSpec((tm, d), row),
            til.mod_spec(layer, 3),
            til.mod_spec(layer, 4),
            til.mod_spec(layer, 5),
            _layer_vec_spec(layer, d),
            pl.BlockSpec((None, d, tf), lambda i, j: (idx, 0, j)),
            pl.BlockSpec((None, d, tf), lambda i, j: (idx, 0, nf + j)),
            pl.BlockSpec((None, tf, d), lambda i, j: (idx, j, 0)),
        ],
        out_specs=pl.BlockSpec((tm, d), row),
        out_shape=jax.ShapeDtypeStruct((t, d), F32),
        scratch_shapes=[pltpu.VMEM((tm, d), BF16), pltpu.VMEM((tm, d), F32)],
        compiler_params=_params(("arbitrary", "arbitrary")),
    )(x, mods, mods, mods, g_all, w_up_all, w_up_all, w_down_all)


META_IDX, META_GATE, META_RANK = 0, 2, 4


def _router_kernel(x_ref, sh_ref, sc_ref, g_ref, wr_ref, tri_ref, meta_ref, meta_t_ref, cnt_ref, run_ref,
                   *, n_experts):
    @pl.when(pl.program_id(0) == 0)
    def _():
        run_ref[...] = jnp.zeros_like(run_ref)

    h = _norm_mod(x_ref[...], g_ref[...], sh_ref[...], sc_ref[...])
    logits = jnp.dot(h, wr_ref[...], preferred_element_type=F32, precision=lax.Precision.HIGHEST)
    lane = lax.broadcasted_iota(jnp.int32, logits.shape, 1).astype(F32)
    logits = jnp.where(lane < n_experts, logits, -jnp.inf)
    top1 = logits.max(axis=-1, keepdims=True)
    idx1 = jnp.where(logits == top1, lane, float(LANES)).min(axis=-1, keepdims=True)
    rest = jnp.where(lane == idx1, -jnp.inf, logits)
    top2 = rest.max(axis=-1, keepdims=True)
    idx2 = jnp.where(rest == top2, lane, float(LANES)).min(axis=-1, keepdims=True)
    e = jnp.exp(top2 - top1)
    g1 = 1.0 / (1.0 + e)
    g2 = e / (1.0 + e)

    sel1 = lane == idx1
    sel2 = lane == idx2
    sel = jnp.where(sel1, 1.0, 0.0) + jnp.where(sel2, 1.0, 0.0)
    before = jnp.dot(tri_ref[...], sel.astype(BF16), preferred_element_type=F32) + run_ref[...]
    r1 = jnp.where(sel1, before, 0.0).sum(axis=-1, keepdims=True)
    r2 = jnp.where(sel2, before, 0.0).sum(axis=-1, keepdims=True)
    run_ref[...] += sel.sum(axis=0, keepdims=True)
    cnt_ref[...] = run_ref[...]

    meta = jnp.zeros_like(logits)
    for off, (a, b) in ((META_IDX, (idx1, idx2)), (META_GATE, (g1, g2)), (META_RANK, (r1, r2))):
        meta = jnp.where(lane == off, a, meta)
        meta = jnp.where(lane == off + 1, b, meta)
    meta_ref[...] = meta
    meta_t_ref[...] = meta.T[:SUBLANES]


def _router(x, mods, g_all, wr, tri, *, layer, n_experts, til):
    t, d = x.shape
    tm = til.tm
    row = lambda i: (i, 0)
    fixed = lambda i: (0, 0)
    return pl.pallas_call(
        functools.partial(_router_kernel, n_experts=n_experts),
        grid=(t // tm,),
        in_specs=[
            pl.BlockSpec((tm, d), row),
            til.mod_spec(layer, 3),
            til.mod_spec(layer, 4),
            _layer_vec_spec(layer, d),
            pl.BlockSpec((d, LANES), fixed),
            pl.BlockSpec((tm, tm), fixed),
        ],
        out_specs=[pl.BlockSpec((tm, LANES), row), pl.BlockSpec((SUBLANES, tm), lambda i: (0, i)),
                   pl.BlockSpec((1, LANES), fixed)],
        out_shape=[jax.ShapeDtypeStruct((t, LANES), F32), jax.ShapeDtypeStruct((SUBLANES, t), F32),
                   jax.ShapeDtypeStruct((1, LANES), F32)],
        scratch_shapes=[pltpu.VMEM((1, LANES), F32)],
        compiler_params=_params(("arbitrary",)),
    )(x, mods, mods, g_all, wr, tri)


def _row_copy(src_ref, src_row, dst_ref, dst_row, sem):
    return pltpu.make_async_copy(src_ref.at[pl.ds(src_row, 1)], dst_ref.at[pl.ds(dst_row, 1)], sem)


def _dispatch_kernel(pos_ref, x_ref, sh_ref, sc_ref, g_ref, hs_init_ref, hs_ref, h_ref, sem, *, tm, n_tokens):
    del hs_init_ref
    i = pl.program_id(0)
    slot = i % 2

    def wait_slot(s):
        def body(r, carry):
            for _ in range(TOP_K):
                _row_copy(h_ref.at[s], 0, hs_ref, 0, sem.at[s]).wait()
            return carry
        lax.fori_loop(0, tm, body, 0)

    @pl.when(i >= 2)
    def _():
        wait_slot(slot)

    h_ref[slot] = _norm_mod(x_ref[...], g_ref[...], sh_ref[...], sc_ref[...])

    def send(r, carry):
        for k in range(TOP_K):
            dst = pos_ref[k * n_tokens + i * tm + r]
            _row_copy(h_ref.at[slot], r, hs_ref, dst, sem.at[slot]).start(priority=k)
        return carry
    lax.fori_loop(0, tm, send, 0, unroll=8)

    @pl.when(i == pl.num_programs(0) - 1)
    def _():
        wait_slot(slot)

        @pl.when(i >= 1)
        def _():
            wait_slot(1 - slot)


def _dispatch(pos, x, mods, g_all, hs_init, *, layer, til):
    t, d = x.shape
    tm = til.tm
    return pl.pallas_call(
        functools.partial(_dispatch_kernel, tm=tm, n_tokens=t),
        grid_spec=pltpu.PrefetchScalarGridSpec(
            num_scalar_prefetch=1,
            grid=(t // tm,),
            in_specs=[
                pl.BlockSpec((tm, d), lambda i, pos: (i, 0)),
                til.mod_spec(layer, 3),
                til.mod_spec(layer, 4),
                _layer_vec_spec(layer, d),
                pl.BlockSpec(memory_space=pl.ANY),
            ],
            out_specs=pl.BlockSpec(memory_space=pl.ANY),
            scratch_shapes=[pltpu.VMEM((2, tm, d), F32), pltpu.SemaphoreType.DMA((2,))],
        ),
        out_shape=jax.ShapeDtypeStruct(hs_init.shape, F32),
        input_output_aliases={5: 0},
        compiler_params=_params(("arbitrary",)),
    )(pos, x, mods, mods, g_all, hs_init)


def _expert_kernel(te_ref, tv_ref, hs_ref, wg_ref, wu_ref, wd_ref, ys_ref, h_ref, acc_ref):
    del te_ref
    j = pl.program_id(1)

    @pl.when(tv_ref[pl.program_id(0)] != 0)
    def _():
        @pl.when(j == 0)
        def _():
            h_ref[...] = hs_ref[...].astype(BF16)
            acc_ref[...] = jnp.zeros_like(acc_ref)

        a = _swiglu_act(h_ref[...], wg_ref[...], wu_ref[...])
        acc_ref[...] += jnp.dot(a.astype(BF16), wd_ref[...], preferred_element_type=F32)

        @pl.when(j == pl.num_programs(1) - 1)
        def _():
            ys_ref[...] = acc_ref[...]

    @pl.when((tv_ref[pl.program_id(0)] == 0) & (j == 0))
    def _():
        ys_ref[...] = jnp.zeros_like(ys_ref)


def _experts(tile_expert, tile_valid, hs, w_up_all, w_down_all, *, idx, tm, tf):
    p, d = hs.shape
    d_ff = w_down_all.shape[2]
    nf = d_ff // tf
    return pl.pallas_call(
        _expert_kernel,
        grid_spec=pltpu.PrefetchScalarGridSpec(
            num_scalar_prefetch=2,
            grid=(p // tm, nf),
            in_specs=[
                pl.BlockSpec((tm, d), lambda r, j, te, tv: (r, 0)),
                pl.BlockSpec((None, None, d, tf), lambda r, j, te, tv: (idx, te[r], 0, j)),
                pl.BlockSpec((None, None, d, tf), lambda r, j, te, tv: (idx, te[r], 0, nf + j)),
                pl.BlockSpec((None, None, tf, d), lambda r, j, te, tv: (idx, te[r], j, 0)),
            ],
            out_specs=pl.BlockSpec((tm, d), lambda r, j, te, tv: (r, 0)),
            scratch_shapes=[pltpu.VMEM((tm, d), BF16), pltpu.VMEM((tm, d), F32)],
        ),
        out_shape=jax.ShapeDtypeStruct((p, d), F32),
        compiler_params=_params(("arbitrary", "arbitrary")),
    )(tile_expert, tile_valid, hs, w_up_all, w_up_all, w_down_all)


def _combine_kernel(pos_ref, ys_ref, meta_ref, x_ref, gate_ref, out_ref, y_ref, sem, *, tm, n_tokens):
    i = pl.program_id(0)
    slot = i % 2

    def fetch(step, s):
        def body(r, carry):
            for k in range(TOP_K):
                src = pos_ref[k * n_tokens + step * tm + r]
                _row_copy(ys_ref, src, y_ref.at[s, k], r, sem.at[s]).start(priority=k)
            return carry
        lax.fori_loop(0, tm, body, 0, unroll=8)

    @pl.when(i == 0)
    def _():
        fetch(0, 0)

    @pl.when(i + 1 < pl.num_programs(0))
    def _():
        fetch(i + 1, 1 - slot)

    def wait_row(r, carry):
        for k in range(TOP_K):
            _row_copy(ys_ref, 0, y_ref.at[slot, k], 0, sem.at[slot]).wait()
        return carry
    lax.fori_loop(0, tm, wait_row, 0)

    meta = meta_ref[...]
    f = meta[:, META_GATE:META_GATE + 1] * y_ref[slot, 0] + meta[:, META_GATE + 1:META_GATE + 2] * y_ref[slot, 1]
    out_ref[...] = x_ref[...] + gate_ref[...] * f


def _combine(pos, ys, meta, x, mods, *, layer, til):
    t, d = x.shape
    tm = til.tm
    row = lambda i, pos: (i, 0)
    return pl.pallas_call(
        functools.partial(_combine_kernel, tm=tm, n_tokens=t),
        grid_spec=pltpu.PrefetchScalarGridSpec(
            num_scalar_prefetch=1,
            grid=(t // tm,),
            in_specs=[
                pl.BlockSpec(memory_space=pl.ANY),
                pl.BlockSpec((tm, LANES), row),
                pl.BlockSpec((tm, d), row),
                til.mod_spec(layer, 5),
            ],
            out_specs=pl.BlockSpec((tm, d), row),
            scratch_shapes=[pltpu.VMEM((2, TOP_K, tm, d), F32), pltpu.SemaphoreType.DMA((2,))],
        ),
        out_shape=jax.ShapeDtypeStruct((t, d), F32),
        compiler_params=_params(("arbitrary",)),
    )(pos, ys, meta, x, mods)


def _moe(x, mods, g_all, w_router, w_up_all, w_down_all, *, layer, idx, til, tf):
    t, d = x.shape
    n_experts = w_router.shape[-1]
    tm = til.tm
    wr = jnp.zeros((d, LANES), F32).at[:, :n_experts].set(w_router)
    tri = (jnp.arange(tm)[:, None] > jnp.arange(tm)[None, :]).astype(BF16)
    meta, meta_t, counts = _router(x, mods, g_all, wr, tri, layer=layer, n_experts=n_experts, til=til)

    counts = counts[0, :n_experts].astype(jnp.int32)
    padded = ((counts + tm - 1) // tm) * tm
    ends = jnp.cumsum(padded)
    starts = ends - padded
    choice = meta_t[META_IDX:META_IDX + TOP_K].astype(jnp.int32)
    rank = meta_t[META_RANK:META_RANK + TOP_K].astype(jnp.int32)
    pos = (starts[choice] + rank).reshape(TOP_K * t)
    n_tiles = (t * TOP_K) // tm + n_experts
    tile_start = jnp.arange(n_tiles, dtype=jnp.int32) * tm
    tile_expert = jnp.minimum((tile_start[:, None] >= ends[None, :]).sum(axis=-1), n_experts - 1).astype(jnp.int32)
    tile_valid = (tile_start < ends[-1]).astype(jnp.int32)

    hs = _dispatch(pos, x, mods, g_all, jnp.zeros((n_tiles * tm, d), F32), layer=layer, til=til)
    ys = _experts(tile_expert, tile_valid, hs, w_up_all, w_down_all, idx=idx, tm=tm, tf=tf)
    return _combine(pos, ys, meta, x, mods, layer=layer, til=til)


def _final_norm_kernel(x_ref, g_ref, o_ref):
    x = x_ref[...]
    ms = jnp.mean(x * x, axis=-1, keepdims=True)
    o_ref[...] = (x * lax.rsqrt(ms + EPS)) * g_ref[...]


def _final_norm(x, g, *, tm, first_tile, n_tiles):
    d = x.shape[1]
    return pl.pallas_call(
        _final_norm_kernel,
        grid=(n_tiles,),
        in_specs=[pl.BlockSpec((tm, d), lambda i: (first_tile + i, 0)), pl.BlockSpec((1, d), lambda i: (0, 0))],
        out_specs=pl.BlockSpec((tm, d), lambda i: (i, 0)),
        out_shape=jax.ShapeDtypeStruct((n_tiles * tm, d), F32),
        compiler_params=_params(("arbitrary",)),
    )(x, g)


def _rope_tables(n_ctx_tokens, n_lat_seq, lat_seq):
    pos = jnp.arange(lat_seq)
    rows = (pos // GRID_W).astype(F32)
    cols = (pos % GRID_W).astype(F32)
    pairs_per_axis = HEAD_DIM // 4
    inv = ROPE_THETA ** (-jnp.arange(pairs_per_axis, dtype=F32) / pairs_per_axis)
    ang = jnp.concatenate([rows[:, None] * inv, cols[:, None] * inv], axis=-1)
    cos = jnp.repeat(jnp.cos(ang), 2, axis=-1)
    sin = jnp.repeat(jnp.sin(ang), 2, axis=-1) * jnp.tile(jnp.array([-1.0, 1.0], F32), HEAD_DIM // 2)
    reps = LANES // HEAD_DIM
    cos = jnp.tile(cos, (n_lat_seq, reps))
    sin = jnp.tile(sin, (n_lat_seq, reps))
    cos = jnp.concatenate([jnp.ones((n_ctx_tokens, LANES), F32), cos], axis=0)
    sin = jnp.concatenate([jnp.zeros((n_ctx_tokens, LANES), F32), sin], axis=0)
    return cos, sin


def kernel(x_prompt, x_sample, cache_k, cache_v, c, c_ctx, w_ada, b_ada, norm_mix, norm_ffn, norm_final,
           w_qkv, w_o, q_norm, k_norm, sinks, w_ffn_up, w_ffn_down, w_router, w_exp_up, w_exp_down):
    n_ctx_seq, seq, d = x_prompt.shape
    n_lat_seq, lat_seq, _ = x_sample.shape
    depth = w_ada.shape[0]
    past = cache_k.shape[2]
    n_experts = w_router.shape[-1]
    kv_dim = N_KV_HEADS * HEAD_DIM
    n_ctx_tokens = n_ctx_seq * seq
    n_lat_tokens = n_lat_seq * lat_seq
    assert d == N_HEADS * HEAD_DIM and seq == Q_TILE and lat_seq % Q_TILE == 0 and kv_dim == MXU_DIM
    assert n_ctx_tokens % lat_seq == 0 and n_lat_seq < MOD_ROWS and n_experts <= LANES

    tm = 2 * Q_TILE if (lat_seq % (2 * Q_TILE) == 0 and n_ctx_tokens % (2 * Q_TILE) == 0) else Q_TILE
    til = _Tiling(tm, n_ctx_tokens, lat_seq, n_lat_seq, d)

    x = jnp.concatenate([x_prompt.reshape(n_ctx_tokens, d), x_sample.reshape(n_lat_tokens, d)], axis=0)

    cvecs = jnp.zeros((MOD_ROWS, d), F32).at[:n_lat_seq].set(c).at[n_lat_seq].set(c_ctx)
    mods = _ada_mods(cvecs, w_ada, b_ada).reshape(depth, MOD_ROWS * N_MOD, 1, d)

    cos, sin = _rope_tables(n_ctx_tokens, n_lat_seq, lat_seq)
    head_id = jnp.arange(MXU_DIM) // HEAD_DIM
    ones = (head_id[:, None] == head_id[None, :]).astype(BF16)

    kc_all = cache_k.transpose(1, 3, 0, 2, 4).reshape(depth, N_KV_HEADS, n_lat_seq * past, HEAD_DIM).astype(BF16)
    vc_all = cache_v.transpose(1, 3, 0, 2, 4).reshape(depth, N_KV_HEADS, n_lat_seq * past, HEAD_DIM).astype(BF16)

    w_qkv_b = w_qkv.astype(BF16)
    w_o_b = w_o.astype(BF16)
    w_ffn_up_b = w_ffn_up.astype(BF16)
    w_ffn_down_b = w_ffn_down.astype(BF16)
    w_exp_up_b = w_exp_up.astype(BF16)
    w_exp_down_b = w_exp_down.astype(BF16)
    g_mix = norm_mix.reshape(depth, 1, d)
    g_ffn = norm_ffn.reshape(depth, 1, d)

    tf_dense = _pick_tile(w_ffn_down.shape[1], (1408, 1024, 512, 256, 128))
    tf_expert = _pick_tile(w_exp_down.shape[2], (1792, 1024, 512, 256, 128))
    lat = dict(n_ctx_tokens=n_ctx_tokens, n_lat_seq=n_lat_seq, lat_seq=lat_seq, past=past)

    new_k = []
    new_v = []
    for l in range(depth):
        idx = l // 2
        use_a = (l % 2 == 0)
        if use_a:
            qn = jnp.tile(q_norm[idx], d // HEAD_DIM).reshape(1, d)
            kn = jnp.tile(k_norm[idx], N_KV_HEADS).reshape(1, kv_dim)
            sink = jnp.zeros((N_HEADS,), F32)
        else:
            qn = jnp.ones((1, d), F32)
            kn = jnp.ones((1, kv_dim), F32)
            sink = sinks[idx].astype(F32)

        q, kf, vf, kh, vh = _qkv(x, mods, g_mix, w_qkv_b, qn, kn, cos, sin, ones, layer=l, use_a=use_a, til=til)
        new_k.append(kf.reshape(n_ctx_seq, seq, N_KV_HEADS, HEAD_DIM))
        new_v.append(vf.reshape(n_ctx_seq, seq, N_KV_HEADS, HEAD_DIM))

        o_ctx = _attn_ctx(sink, q, kh, vh, n_ctx_seq=n_ctx_seq, seq=seq, has_sink=not use_a)
        if use_a:
            o_lat = _attn_global(q, kh, vh, kc_all, vc_all, layer=l, **lat)
        else:
            o_lat = _attn_window(sink, q, kh, vh, kc_all, vc_all, layer=l, **lat)
        x = _proj(o_ctx, o_lat, w_o_b, x, mods, layer=l, til=til)

        if use_a:
            x = _ffn(x, mods, g_ffn, w_ffn_up_b, w_ffn_down_b, layer=l, idx=idx, til=til, tf=tf_dense)
        else:
            x = _moe(x, mods, g_ffn, w_router[idx], w_exp_up_b, w_exp_down_b, layer=l, idx=idx, til=til,
                     tf=tf_expert)

    gf = norm_final.reshape(1, d)
    y_prompt = _final_norm(x, gf, tm=tm, first_tile=0, n_tiles=til.n_ctx_tiles).reshape(n_ctx_seq, seq, d)
    y_sample = _final_norm(x, gf, tm=tm, first_tile=til.n_ctx_tiles, n_tiles=n_lat_tokens // tm)
    y_sample = y_sample.reshape(n_lat_seq, lat_seq, d)
    return (y_prompt, y_sample, jnp.stack(new_k, axis=1), jnp.stack(new_v, axis=1))
```

```python
import functools

import jax
import jax.numpy as jnp
from jax import lax
from jax.experimental import pallas as pl
from jax.experimental.pallas import tpu as pltpu

N_HEADS = 16
N_KV_HEADS = 4
GROUP = N_HEADS // N_KV_HEADS
HEAD_DIM = 64
GRID_W = 64
WINDOW = 128
ROPE_THETA = 10000.0
N_MOD = 6
TOP_K = 2
EPS = 1e-6
NEG_INF = -1e30
ATTN_SCALE = HEAD_DIM ** -0.5

LANES = 128
SUBLANES = 8
MXU_DIM = 256
MOD_ROWS = 16
Q_TILE = 256
VMEM_LIMIT = 56 * 1024 * 1024

F32 = jnp.float32
BF16 = jnp.bfloat16
NT_DIMS = (((1,), (1,)), ((), ()))


def _params(sem, vmem=VMEM_LIMIT):
    return pltpu.CompilerParams(dimension_semantics=sem, vmem_limit_bytes=vmem)


def _pick_tile(n, candidates):
    for c in candidates:
        if n % c == 0:
            return c
    return n


def _ada_kernel(c_ref, w_ref, b_ref, o_ref):
    c = c_ref[...]
    a = c / (1.0 + jnp.exp(-c))
    o_ref[...] = jnp.dot(a, w_ref[...], preferred_element_type=F32,
                         precision=lax.Precision.HIGHEST) + b_ref[...]


def _ada_mods(cvecs, w_ada, b_ada):
    depth, d, n = w_ada.shape
    tn = _pick_tile(n, (1536, 1024, 512, 256, 128))
    return pl.pallas_call(
        _ada_kernel,
        grid=(depth, n // tn),
        in_specs=[
            pl.BlockSpec((MOD_ROWS, d), lambda l, j: (0, 0)),
            pl.BlockSpec((None, d, tn), lambda l, j: (l, 0, j)),
            pl.BlockSpec((None, 1, tn), lambda l, j: (l, 0, j)),
        ],
        out_specs=pl.BlockSpec((None, MOD_ROWS, tn), lambda l, j: (l, 0, j)),
        out_shape=jax.ShapeDtypeStruct((depth, MOD_ROWS, n), F32),
        compiler_params=_params(("arbitrary", "arbitrary")),
    )(cvecs, w_ada, b_ada.reshape(depth, 1, n))


def _norm_mod(x, g, shift, scale):
    ms = jnp.mean(x * x, axis=-1, keepdims=True)
    y = (x * lax.rsqrt(ms + EPS)) * g
    return y * (1.0 + scale) + shift


class _Tiling:
    def __init__(self, tm, n_ctx_tokens, lat_seq, n_lat_seq, d):
        self.tm = tm
        self.n_ctx_tiles = n_ctx_tokens // tm
        self.tiles_per_seq = lat_seq // tm
        self.ctx_row = n_lat_seq
        self.d = d

    def mod_spec(self, layer, which):
        def index(i, *_):
            row = jnp.where(i < self.n_ctx_tiles, self.ctx_row, (i - self.n_ctx_tiles) // self.tiles_per_seq)
            return (layer, row * N_MOD + which, 0, 0)
        return pl.BlockSpec((None, None, 1, self.d), index)


def _layer_vec_spec(layer, d):
    return pl.BlockSpec((None, 1, d), lambda *_: (layer, 0, 0))


def _qkv_kernel(x_ref, sh_ref, sc_ref, g_ref, w_ref, qn_ref, kn_ref, cos_ref, sin_ref, ones_ref,
                q_ref, kf_ref, vf_ref, kh_ref, vh_ref, h_ref, *, use_a, d_model, n_ctx_tiles):
    q_dim = d_model
    kv_dim = N_KV_HEADS * HEAD_DIM
    is_ctx = pl.program_id(0) < n_ctx_tiles
    h_ref[...] = _norm_mod(x_ref[...], g_ref[...], sh_ref[...], sc_ref[...]).astype(BF16)

    cos = cos_ref[...]
    sin = sin_ref[...]
    even = (lax.broadcasted_iota(jnp.int32, cos.shape, 1) % 2) == 0

    if use_a:
        qkv = jnp.dot(h_ref[...], w_ref[...], preferred_element_type=F32)

    def project(col0, gain):
        if use_a:
            xc = qkv[:, col0:col0 + MXU_DIM]
        else:
            xc = jnp.dot(h_ref[...], w_ref[:, col0:col0 + MXU_DIM], preferred_element_type=F32)
        if use_a and gain is not None:
            ss = jnp.dot((xc * xc).astype(BF16), ones_ref[...], preferred_element_type=F32)
            xc = xc * lax.rsqrt(ss * (1.0 / HEAD_DIM) + EPS) * gain
        return xc

    def rope(xc):
        cols = []
        for l0 in range(0, MXU_DIM, LANES):
            xl = xc[:, l0:l0 + LANES]
            swapped = jnp.where(even, pltpu.roll(xl, LANES - 1, 1), pltpu.roll(xl, 1, 1))
            cols.append(xl * cos + swapped * sin)
        return jnp.concatenate(cols, axis=-1)

    for c0 in range(0, q_dim, MXU_DIM):
        qc = rope(project(c0, qn_ref[:, c0:c0 + MXU_DIM]))
        q_ref[:, c0:c0 + MXU_DIM] = (qc * ATTN_SCALE).astype(BF16)
    k = rope(project(q_dim, kn_ref[...]))
    v = project(q_dim + kv_dim, None)
    for hh in range(N_KV_HEADS):
        kh_ref[hh] = k[:, hh * HEAD_DIM:(hh + 1) * HEAD_DIM].astype(BF16)
        vh_ref[hh] = v[:, hh * HEAD_DIM:(hh + 1) * HEAD_DIM].astype(BF16)

    @pl.when(is_ctx)
    def _():
        kf_ref[...] = k
        vf_ref[...] = v


def _qkv(x, mods, g_all, w_all, qn, kn, cos, sin, ones, *, layer, use_a, til):
    t, d = x.shape
    tm = til.tm
    kv_dim = N_KV_HEADS * HEAD_DIM
    n_out = w_all.shape[2]
    n_ctx_tiles = til.n_ctx_tiles
    row = lambda i: (i, 0)
    fixed = lambda i: (0, 0)
    ctx_row = lambda i: (jnp.minimum(i, n_ctx_tiles - 1), 0)
    return pl.pallas_call(
        functools.partial(_qkv_kernel, use_a=use_a, d_model=d, n_ctx_tiles=n_ctx_tiles),
        grid=(t // tm,),
        in_specs=[
            pl.BlockSpec((tm, d), row),
            til.mod_spec(layer, 0),
            til.mod_spec(layer, 1),
            _layer_vec_spec(layer, d),
            pl.BlockSpec((None, d, n_out), lambda i: (layer, 0, 0)),
            pl.BlockSpec((1, d), fixed),
            pl.BlockSpec((1, kv_dim), fixed),
            pl.BlockSpec((tm, LANES), row),
            pl.BlockSpec((tm, LANES), row),
            pl.BlockSpec((MXU_DIM, MXU_DIM), fixed),
        ],
        out_specs=[
            pl.BlockSpec((tm, d), row),
            pl.BlockSpec((tm, kv_dim), ctx_row),
            pl.BlockSpec((tm, kv_dim), ctx_row),
            pl.BlockSpec((N_KV_HEADS, tm, HEAD_DIM), lambda i: (0, i, 0)),
            pl.BlockSpec((N_KV_HEADS, tm, HEAD_DIM), lambda i: (0, i, 0)),
        ],
        out_shape=[
            jax.ShapeDtypeStruct((t, d), BF16),
            jax.ShapeDtypeStruct((n_ctx_tiles * tm, kv_dim), F32),
            jax.ShapeDtypeStruct((n_ctx_tiles * tm, kv_dim), F32),
            jax.ShapeDtypeStruct((N_KV_HEADS, t, HEAD_DIM), BF16),
            jax.ShapeDtypeStruct((N_KV_HEADS, t, HEAD_DIM), BF16),
        ],
        scratch_shapes=[pltpu.VMEM((tm, d), BF16)],
        compiler_params=_params(("arbitrary",)),
    )(x, mods, mods, g_all, w_all, qn, kn, cos, sin, ones)


def _softmax_pv(score_parts, value_parts, sink):
    m = score_parts[0].max(axis=-1, keepdims=True)
    for s in score_parts[1:]:
        m = jnp.maximum(m, s.max(axis=-1, keepdims=True))
    if sink is not None:
        m = jnp.maximum(m, sink)
    denom = None
    acc = None
    for s, v in zip(score_parts, value_parts):
        p = jnp.exp(s - m)
        ps = p.sum(axis=-1, keepdims=True)
        pv = jnp.dot(p.astype(BF16), v, preferred_element_type=F32)
        denom = ps if denom is None else denom + ps
        acc = pv if acc is None else acc + pv
    if sink is not None:
        denom = denom + jnp.exp(sink - m)
    return acc / denom


def _attn_ctx_kernel(sink_ref, q_ref, k_ref, v_ref, o_ref, *, has_sink):
    for head in range(N_HEADS):
        cols = slice(head * HEAD_DIM, (head + 1) * HEAD_DIM)
        s = lax.dot_general(q_ref[:, cols], k_ref[head // GROUP], NT_DIMS, preferred_element_type=F32)
        sink = sink_ref[head] if has_sink else None
        o_ref[:, cols] = _softmax_pv([s], [v_ref[head // GROUP]], sink).astype(o_ref.dtype)


def _attn_ctx(sink, q, kh, vh, *, n_ctx_seq, seq, has_sink):
    t, d = q.shape
    return pl.pallas_call(
        functools.partial(_attn_ctx_kernel, has_sink=has_sink),
        grid=(n_ctx_seq,),
        in_specs=[
            pl.BlockSpec(memory_space=pltpu.SMEM),
            pl.BlockSpec((seq, d), lambda b: (b, 0)),
            pl.BlockSpec((N_KV_HEADS, seq, HEAD_DIM), lambda b: (0, b, 0)),
            pl.BlockSpec((N_KV_HEADS, seq, HEAD_DIM), lambda b: (0, b, 0)),
        ],
        out_specs=pl.BlockSpec((seq, d), lambda b: (b, 0)),
        out_shape=jax.ShapeDtypeStruct((n_ctx_seq * seq, d), BF16),
        compiler_params=_params(("arbitrary",)),
    )(sink, q, kh, vh)


def _attn_global_kernel(q_ref, k_ref, v_ref, kc_ref, vc_ref, o_ref):
    k = k_ref[...]
    v = v_ref[...]
    kc = kc_ref[...]
    vc = vc_ref[...]
    for j in range(GROUP):
        qj = q_ref[:, j * HEAD_DIM:(j + 1) * HEAD_DIM]
        s_lat = lax.dot_general(qj, k, NT_DIMS, preferred_element_type=F32)
        s_ctx = lax.dot_general(qj, kc, NT_DIMS, preferred_element_type=F32)
        o = _softmax_pv([s_lat, s_ctx], [v, vc], None)
        o_ref[:, j * HEAD_DIM:(j + 1) * HEAD_DIM] = o.astype(o_ref.dtype)


def _cache_spec(layer, past):
    return pl.BlockSpec((None, None, past, HEAD_DIM), lambda b, g, i: (layer, g, b, 0))


def _attn_global(q, kh, vh, kc_all, vc_all, *, layer, n_ctx_tokens, n_lat_seq, lat_seq, past):
    t, d = q.shape
    gw = GROUP * HEAD_DIM
    q_blocks = lat_seq // Q_TILE
    q0 = n_ctx_tokens // Q_TILE
    s0 = n_ctx_tokens // lat_seq
    return pl.pallas_call(
        _attn_global_kernel,
        grid=(n_lat_seq, N_KV_HEADS, q_blocks),
        in_specs=[
            pl.BlockSpec((Q_TILE, gw), lambda b, g, i: (q0 + b * q_blocks + i, g)),
            pl.BlockSpec((None, lat_seq, HEAD_DIM), lambda b, g, i: (g, s0 + b, 0)),
            pl.BlockSpec((None, lat_seq, HEAD_DIM), lambda b, g, i: (g, s0 + b, 0)),
            _cache_spec(layer, past),
            _cache_spec(layer, past),
        ],
        out_specs=pl.BlockSpec((Q_TILE, gw), lambda b, g, i: (b * q_blocks + i, g)),
        out_shape=jax.ShapeDtypeStruct((n_lat_seq * lat_seq, d), BF16),
        compiler_params=_params(("arbitrary", "arbitrary", "arbitrary")),
    )(q, kh, vh, kc_all, vc_all)


def _attn_window_kernel(sink_ref, q_ref, kp_ref, kcur_ref, kn_ref, vp_ref, vcur_ref, vn_ref,
                        kc_ref, vc_ref, o_ref, *, lat_seq):
    grp = pl.program_id(1)
    start = pl.program_id(2) * Q_TILE
    span = Q_TILE + 2 * WINDOW
    kw = jnp.concatenate([kp_ref[...], kcur_ref[...], kn_ref[...]], axis=0)
    vw = jnp.concatenate([vp_ref[...], vcur_ref[...], vn_ref[...]], axis=0)
    kc = kc_ref[...]
    vc = vc_ref[...]
    qpos = start + lax.broadcasted_iota(jnp.int32, (Q_TILE, span), 0)
    kpos = start - WINDOW + lax.broadcasted_iota(jnp.int32, (Q_TILE, span), 1)
    valid = (kpos >= 0) & (kpos < lat_seq) & (jnp.abs(qpos - kpos) <= WINDOW)
    for j in range(GROUP):
        qj = q_ref[:, j * HEAD_DIM:(j + 1) * HEAD_DIM]
        s_loc = lax.dot_general(qj, kw, NT_DIMS, preferred_element_type=F32)
        s_loc = jnp.where(valid, s_loc, NEG_INF)
        s_ctx = lax.dot_general(qj, kc, NT_DIMS, preferred_element_type=F32)
        o = _softmax_pv([s_loc, s_ctx], [vw, vc], sink_ref[grp * GROUP + j])
        o_ref[:, j * HEAD_DIM:(j + 1) * HEAD_DIM] = o.astype(o_ref.dtype)


def _attn_window(sink, q, kh, vh, kc_all, vc_all, *, layer, n_ctx_tokens, n_lat_seq, lat_seq, past):
    t, d = q.shape
    gw = GROUP * HEAD_DIM
    q_blocks = lat_seq // Q_TILE
    q0 = n_ctx_tokens // Q_TILE
    half = Q_TILE // WINDOW
    w0 = n_ctx_tokens // WINDOW
    w_blocks = lat_seq // WINDOW
    prev = lambda b, g, i: (g, w0 + b * w_blocks + jnp.maximum(i * half - 1, 0), 0)
    cur = lambda b, g, i: (g, q0 + b * q_blocks + i, 0)
    nxt = lambda b, g, i: (g, w0 + b * w_blocks + jnp.minimum((i + 1) * half, w_blocks - 1), 0)
    kv_specs = [
        pl.BlockSpec((None, WINDOW, HEAD_DIM), prev),
        pl.BlockSpec((None, Q_TILE, HEAD_DIM), cur),
        pl.BlockSpec((None, WINDOW, HEAD_DIM), nxt),
    ]
    return pl.pallas_call(
        functools.partial(_attn_window_kernel, lat_seq=lat_seq),
        grid=(n_lat_seq, N_KV_HEADS, q_blocks),
        in_specs=[
            pl.BlockSpec(memory_space=pltpu.SMEM),
            pl.BlockSpec((Q_TILE, gw), lambda b, g, i: (q0 + b * q_blocks + i, g)),
            *kv_specs, *kv_specs,
            _cache_spec(layer, past),
            _cache_spec(layer, past),
        ],
        out_specs=pl.BlockSpec((Q_TILE, gw), lambda b, g, i: (b * q_blocks + i, g)),
        out_shape=jax.ShapeDtypeStruct((n_lat_seq * lat_seq, d), BF16),
        compiler_params=_params(("arbitrary", "arbitrary", "arbitrary")),
    )(sink, q, kh, kh, kh, vh, vh, vh, kc_all, vc_all)


def _proj_kernel(oc_ref, ol_ref, w_ref, x_ref, gate_ref, out_ref, *, n_ctx_tiles):
    def project(o_ref):
        y = jnp.dot(o_ref[...], w_ref[...], preferred_element_type=F32)
        out_ref[...] = x_ref[...] + gate_ref[...] * y

    @pl.when(pl.program_id(0) < n_ctx_tiles)
    def _():
        project(oc_ref)

    @pl.when(pl.program_id(0) >= n_ctx_tiles)
    def _():
        project(ol_ref)


def _proj(o_ctx, o_lat, w_all, x, mods, *, layer, til):
    t, d = x.shape
    tm = til.tm
    n_ctx_tiles = til.n_ctx_tiles
    row = lambda i: (i, 0)
    return pl.pallas_call(
        functools.partial(_proj_kernel, n_ctx_tiles=n_ctx_tiles),
        grid=(t // tm,),
        in_specs=[
            pl.BlockSpec((tm, d), lambda i: (jnp.minimum(i, n_ctx_tiles - 1), 0)),
            pl.BlockSpec((tm, d), lambda i: (jnp.maximum(i - n_ctx_tiles, 0), 0)),
            pl.BlockSpec((None, d, d), lambda i: (layer, 0, 0)),
            pl.BlockSpec((tm, d), row),
            til.mod_spec(layer, 2),
        ],
        out_specs=pl.BlockSpec((tm, d), row),
        out_shape=jax.ShapeDtypeStruct((t, d), F32),
        compiler_params=_params(("arbitrary",)),
    )(o_ctx, o_lat, w_all, x, mods)


def _swiglu_act(h, wg, wu):
    g = jnp.dot(h, wg, preferred_element_type=F32)
    u = jnp.dot(h, wu, preferred_element_type=F32)
    return (g / (1.0 + jnp.exp(-g))) * u


def _ffn_kernel(x_ref, sh_ref, sc_ref, gate_ref, g_ref, wg_ref, wu_ref, wd_ref, out_ref):
    x = x_ref[...]
    h = _norm_mod(x, g_ref[...], sh_ref[...], sc_ref[...]).astype(BF16)
    a = _swiglu_act(h, wg_ref[...], wu_ref[...])
    f = jnp.dot(a.astype(BF16), wd_ref[...], preferred_element_type=F32)
    out_ref[...] = x + gate_ref[...] * f


def _ffn(x, mods, g_all, w_up_all, w_down_all, *, layer, idx, til):
    t, d = x.shape
    tm = til.tm
    d_ff = w_down_all.shape[1]
    row = lambda i: (i, 0)
    resident = pl.Buffered(1)
    return pl.pallas_call(
        _ffn_kernel,
        grid=(t // tm,),
        in_specs=[
            pl.BlockSpec((tm, d), row),
            til.mod_spec(layer, 3),
            til.mod_spec(layer, 4),
            til.mod_spec(layer, 5),
            _layer_vec_spec(layer, d),
            pl.BlockSpec((None, d, d_ff), lambda i: (idx, 0, 0), pipeline_mode=resident),
            pl.BlockSpec((None, d, d_ff), lambda i: (idx, 0, 1), pipeline_mode=resident),
            pl.BlockSpec((None, d_ff, d), lambda i: (idx, 0, 0), pipeline_mode=resident),
        ],
        out_specs=pl.BlockSpec((tm, d), row),
        out_shape=jax.ShapeDtypeStruct((t, d), F32),
        compiler_params=_params(("arbitrary",)),
    )(x, mods, mods, mods, g_all, w_up_all, w_up_all, w_down_all)


META_IDX, META_GATE, META_RANK = 0, 2, 4


def _router_kernel(x_ref, sh_ref, sc_ref, g_ref, wr_ref, tri_ref, meta_ref, meta_t_ref, cnt_ref, run_ref,
                   *, n_experts):
    @pl.when(pl.program_id(0) == 0)
    def _():
        run_ref[...] = jnp.zeros_like(run_ref)

    h = _norm_mod(x_ref[...], g_ref[...], sh_ref[...], sc_ref[...])
    logits = jnp.dot(h, wr_ref[...], preferred_element_type=F32, precision=lax.Precision.HIGHEST)
    lane = lax.broadcasted_iota(jnp.int32, logits.shape, 1).astype(F32)
    logits = jnp.where(lane < n_experts, logits, -jnp.inf)
    top1 = logits.max(axis=-1, keepdims=True)
    idx1 = jnp.where(logits == top1, lane, float(LANES)).min(axis=-1, keepdims=True)
    rest = jnp.where(lane == idx1, -jnp.inf, logits)
    top2 = rest.max(axis=-1, keepdims=True)
    idx2 = jnp.where(rest == top2, lane, float(LANES)).min(axis=-1, keepdims=True)
    e = jnp.exp(top2 - top1)
    g1 = 1.0 / (1.0 + e)
    g2 = e / (1.0 + e)

    sel1 = lane == idx1
    sel2 = lane == idx2
    sel = jnp.where(sel1, 1.0, 0.0) + jnp.where(sel2, 1.0, 0.0)
    before = jnp.dot(tri_ref[...], sel.astype(BF16), preferred_element_type=F32) + run_ref[...]
    r1 = jnp.where(sel1, before, 0.0).sum(axis=-1, keepdims=True)
    r2 = jnp.where(sel2, before, 0.0).sum(axis=-1, keepdims=True)
    run_ref[...] += sel.sum(axis=0, keepdims=True)
    cnt_ref[...] = run_ref[...]

    meta = jnp.zeros_like(logits)
    for off, (a, b) in ((META_IDX, (idx1, idx2)), (META_GATE, (g1, g2)), (META_RANK, (r1, r2))):
        meta = jnp.where(lane == off, a, meta)
        meta = jnp.where(lane == off + 1, b, meta)
    meta_ref[...] = meta
    meta_t_ref[...] = meta.T[:SUBLANES]


def _router(x, mods, g_all, wr, tri, *, layer, n_experts, til):
    t, d = x.shape
    tm = til.tm
    row = lambda i: (i, 0)
    fixed = lambda i: (0, 0)
    return pl.pallas_call(
        functools.partial(_router_kernel, n_experts=n_experts),
        grid=(t // tm,),
        in_specs=[
            pl.BlockSpec((tm, d), row),
            til.mod_spec(layer, 3),
            til.mod_spec(layer, 4),
            _layer_vec_spec(layer, d),
            pl.BlockSpec((d, LANES), fixed),
            pl.BlockSpec((tm, tm), fixed),
        ],
        out_specs=[pl.BlockSpec((tm, LANES), row), pl.BlockSpec((SUBLANES, tm), lambda i: (0, i)),
                   pl.BlockSpec((1, LANES), fixed)],
        out_shape=[jax.ShapeDtypeStruct((t, LANES), F32), jax.ShapeDtypeStruct((SUBLANES, t), F32),
                   jax.ShapeDtypeStruct((1, LANES), F32)],
        scratch_shapes=[pltpu.VMEM((1, LANES), F32)],
        compiler_params=_params(("arbitrary",)),
    )(x, mods, mods, g_all, wr, tri)


def _row_copy(src_ref, src_row, dst_ref, dst_row, sem):
    return pltpu.make_async_copy(src_ref.at[pl.ds(src_row, 1)], dst_ref.at[pl.ds(dst_row, 1)], sem)


def _dispatch_kernel(pos_ref, x_ref, sh_ref, sc_ref, g_ref, hs_init_ref, hs_ref, h_ref, sem, *, tm, n_tokens):
    del hs_init_ref
    i = pl.program_id(0)
    slot = i % 2

    def wait_slot(s):
        for _ in range(TOP_K * tm):
            _row_copy(h_ref.at[s], 0, hs_ref, 0, sem.at[s]).wait()

    @pl.when(i >= 2)
    def _():
        wait_slot(slot)

    h_ref[slot] = _norm_mod(x_ref[...], g_ref[...], sh_ref[...], sc_ref[...])

    def send(r, carry):
        for k in range(TOP_K):
            dst = pos_ref[k * n_tokens + i * tm + r]
            _row_copy(h_ref.at[slot], r, hs_ref, dst, sem.at[slot]).start(priority=k)
        return carry
    lax.fori_loop(0, tm, send, 0, unroll=8)

    @pl.when(i == pl.num_programs(0) - 1)
    def _():
        wait_slot(slot)

        @pl.when(i >= 1)
        def _():
            wait_slot(1 - slot)


def _dispatch(pos, x, mods, g_all, hs_init, *, layer, til):
    t, d = x.shape
    tm = til.tm
    return pl.pallas_call(
        functools.partial(_dispatch_kernel, tm=tm, n_tokens=t),
        grid_spec=pltpu.PrefetchScalarGridSpec(
            num_scalar_prefetch=1,
            grid=(t // tm,),
            in_specs=[
                pl.BlockSpec((tm, d), lambda i, pos: (i, 0)),
                til.mod_spec(layer, 3),
                til.mod_spec(layer, 4),
                _layer_vec_spec(layer, d),
                pl.BlockSpec(memory_space=pl.ANY),
            ],
            out_specs=pl.BlockSpec(memory_space=pl.ANY),
            scratch_shapes=[pltpu.VMEM((2, tm, d), F32), pltpu.SemaphoreType.DMA((2,))],
        ),
        out_shape=jax.ShapeDtypeStruct(hs_init.shape, F32),
        input_output_aliases={5: 0},
        compiler_params=_params(("arbitrary",)),
    )(pos, x, mods, mods, g_all, hs_init)


def _expert_kernel(te_ref, tv_ref, hs_ref, wg_ref, wu_ref, wd_ref, ys_ref, h_ref, acc_ref):
    del te_ref
    j = pl.program_id(1)

    @pl.when(tv_ref[pl.program_id(0)] != 0)
    def _():
        @pl.when(j == 0)
        def _():
            h_ref[...] = hs_ref[...].astype(BF16)
            acc_ref[...] = jnp.zeros_like(acc_ref)

        a = _swiglu_act(h_ref[...], wg_ref[...], wu_ref[...])
        acc_ref[...] += jnp.dot(a.astype(BF16), wd_ref[...], preferred_element_type=F32)

        @pl.when(j == pl.num_programs(1) - 1)
        def _():
            ys_ref[...] = acc_ref[...]

    @pl.when((tv_ref[pl.program_id(0)] == 0) & (j == 0))
    def _():
        ys_ref[...] = jnp.zeros_like(ys_ref)


def _experts(tile_expert, tile_valid, hs, w_up_all, w_down_all, *, idx, tm, tf):
    p, d = hs.shape
    d_ff = w_down_all.shape[2]
    nf = d_ff // tf
    return pl.pallas_call(
        _expert_kernel,
        grid_spec=pltpu.PrefetchScalarGridSpec(
            num_scalar_prefetch=2,
            grid=(p // tm, nf),
            in_specs=[
                pl.BlockSpec((tm, d), lambda r, j, te, tv: (r, 0)),
                pl.BlockSpec((None, None, d, tf), lambda r, j, te, tv: (idx, te[r], 0, j)),
                pl.BlockSpec((None, None, d, tf), lambda r, j, te, tv: (idx, te[r], 0, nf + j)),
                pl.BlockSpec((None, None, tf, d), lambda r, j, te, tv: (idx, te[r], j, 0)),
            ],
            out_specs=pl.BlockSpec((tm, d), lambda r, j, te, tv: (r, 0)),
            scratch_shapes=[pltpu.VMEM((tm, d), BF16), pltpu.VMEM((tm, d), F32)],
        ),
        out_shape=jax.ShapeDtypeStruct((p, d), F32),
        compiler_params=_params(("arbitrary", "arbitrary")),
    )(tile_expert, tile_valid, hs, w_up_all, w_up_all, w_down_all)


def _combine_kernel(pos_ref, ys_ref, meta_ref, x_ref, gate_ref, out_ref, y_ref, sem, *, tm, n_tokens):
    i = pl.program_id(0)
    slot = i % 2

    def fetch(step, s):
        def body(r, carry):
            for k in range(TOP_K):
                src = pos_ref[k * n_tokens + step * tm + r]
                _row_copy(ys_ref, src, y_ref.at[s, k], r, sem.at[s]).start(priority=k)
            return carry
        lax.fori_loop(0, tm, body, 0, unroll=8)

    @pl.when(i == 0)
    def _():
        fetch(0, 0)

    @pl.when(i + 1 < pl.num_programs(0))
    def _():
        fetch(i + 1, 1 - slot)

    for _ in range(tm):
        for k in range(TOP_K):
            _row_copy(ys_ref, 0, y_ref.at[slot, k], 0, sem.at[slot]).wait()

    meta = meta_ref[...]
    f = meta[:, META_GATE:META_GATE + 1] * y_ref[slot, 0] + meta[:, META_GATE + 1:META_GATE + 2] * y_ref[slot, 1]
    out_ref[...] = x_ref[...] + gate_ref[...] * f


def _combine(pos, ys, meta, x, mods, *, layer, til):
    t, d = x.shape
    tm = til.tm
    row = lambda i, pos: (i, 0)
    return pl.pallas_call(
        functools.partial(_combine_kernel, tm=tm, n_tokens=t),
        grid_spec=pltpu.PrefetchScalarGridSpec(
            num_scalar_prefetch=1,
            grid=(t // tm,),
            in_specs=[
                pl.BlockSpec(memory_space=pl.ANY),
                pl.BlockSpec((tm, LANES), row),
                pl.BlockSpec((tm, d), row),
                til.mod_spec(layer, 5),
            ],
            out_specs=pl.BlockSpec((tm, d), row),
            scratch_shapes=[pltpu.VMEM((2, TOP_K, tm, d), F32), pltpu.SemaphoreType.DMA((2,))],
        ),
        out_shape=jax.ShapeDtypeStruct((t, d), F32),
        compiler_params=_params(("arbitrary",)),
    )(pos, ys, meta, x, mods)


def _moe(x, mods, g_all, w_router, w_up_all, w_down_all, *, layer, idx, til, tf):
    t, d = x.shape
    n_experts = w_router.shape[-1]
    tm = til.tm
    wr = jnp.zeros((d, LANES), F32).at[:, :n_experts].set(w_router)
    tri = (jnp.arange(tm)[:, None] > jnp.arange(tm)[None, :]).astype(BF16)
    meta, meta_t, counts = _router(x, mods, g_all, wr, tri, layer=layer, n_experts=n_experts, til=til)

    counts = counts[0, :n_experts].astype(jnp.int32)
    padded = ((counts + tm - 1) // tm) * tm
    ends = jnp.cumsum(padded)
    starts = ends - padded
    choice = meta_t[META_IDX:META_IDX + TOP_K].astype(jnp.int32)
    rank = meta_t[META_RANK:META_RANK + TOP_K].astype(jnp.int32)
    group_start = sum(jnp.where(choice == e, starts[e], 0) for e in range(n_experts))
    pos = (group_start + rank).reshape(TOP_K * t)
    n_tiles = (t * TOP_K) // tm + n_experts
    tile_start = jnp.arange(n_tiles, dtype=jnp.int32) * tm
    tile_expert = jnp.minimum((tile_start[:, None] >= ends[None, :]).sum(axis=-1), n_experts - 1).astype(jnp.int32)
    tile_valid = (tile_start < ends[-1]).astype(jnp.int32)

    hs = _dispatch(pos, x, mods, g_all, jnp.zeros((n_tiles * tm, d), F32), layer=layer, til=til)
    ys = _experts(tile_expert, tile_valid, hs, w_up_all, w_down_all, idx=idx, tm=tm, tf=tf)
    return _combine(pos, ys, meta, x, mods, layer=layer, til=til)


def _final_norm_kernel(x_ref, g_ref, o_ref):
    x = x_ref[...]
    ms = jnp.mean(x * x, axis=-1, keepdims=True)
    o_ref[...] = (x * lax.rsqrt(ms + EPS)) * g_ref[...]


def _final_norm(x, g, *, tm, first_tile, n_tiles):
    d = x.shape[1]
    return pl.pallas_call(
        _final_norm_kernel,
        grid=(n_tiles,),
        in_specs=[pl.BlockSpec((tm, d), lambda i: (first_tile + i, 0)), pl.BlockSpec((1, d), lambda i: (0, 0))],
        out_specs=pl.BlockSpec((tm, d), lambda i: (i, 0)),
        out_shape=jax.ShapeDtypeStruct((n_tiles * tm, d), F32),
        compiler_params=_params(("arbitrary",)),
    )(x, g)


def _rope_tables(n_ctx_tokens, n_lat_seq, lat_seq):
    pos = jnp.arange(lat_seq)
    rows = (pos // GRID_W).astype(F32)
    cols = (pos % GRID_W).astype(F32)
    pairs_per_axis = HEAD_DIM // 4
    inv = ROPE_THETA ** (-jnp.arange(pairs_per_axis, dtype=F32) / pairs_per_axis)
    ang = jnp.concatenate([rows[:, None] * inv, cols[:, None] * inv], axis=-1)
    cos = jnp.repeat(jnp.cos(ang), 2, axis=-1)
    sin = jnp.repeat(jnp.sin(ang), 2, axis=-1) * jnp.tile(jnp.array([-1.0, 1.0], F32), HEAD_DIM // 2)
    reps = LANES // HEAD_DIM
    cos = jnp.tile(cos, (n_lat_seq, reps))
    sin = jnp.tile(sin, (n_lat_seq, reps))
    cos = jnp.concatenate([jnp.ones((n_ctx_tokens, LANES), F32), cos], axis=0)
    sin = jnp.concatenate([jnp.zeros((n_ctx_tokens, LANES), F32), sin], axis=0)
    return cos, sin


def kernel(x_prompt, x_sample, cache_k, cache_v, c, c_ctx, w_ada, b_ada, norm_mix, norm_ffn, norm_final,
           w_qkv, w_o, q_norm, k_norm, sinks, w_ffn_up, w_ffn_down, w_router, w_exp_up, w_exp_down):
    n_ctx_seq, seq, d = x_prompt.shape
    n_lat_seq, lat_seq, _ = x_sample.shape
    depth = w_ada.shape[0]
    past = cache_k.shape[2]
    n_experts = w_router.shape[-1]
    kv_dim = N_KV_HEADS * HEAD_DIM
    n_ctx_tokens = n_ctx_seq * seq
    n_lat_tokens = n_lat_seq * lat_seq
    assert d == N_HEADS * HEAD_DIM and seq == Q_TILE and lat_seq % Q_TILE == 0 and kv_dim == MXU_DIM
    assert n_ctx_tokens % lat_seq == 0 and n_lat_seq < MOD_ROWS and n_experts <= LANES

    tm = 2 * Q_TILE if (lat_seq % (2 * Q_TILE) == 0 and n_ctx_tokens % (2 * Q_TILE) == 0) else Q_TILE
    til = _Tiling(tm, n_ctx_tokens, lat_seq, n_lat_seq, d)

    x = jnp.concatenate([x_prompt.reshape(n_ctx_tokens, d), x_sample.reshape(n_lat_tokens, d)], axis=0)

    cvecs = jnp.zeros((MOD_ROWS, d), F32).at[:n_lat_seq].set(c).at[n_lat_seq].set(c_ctx)
    mods = _ada_mods(cvecs, w_ada, b_ada).reshape(depth, MOD_ROWS * N_MOD, 1, d)

    cos, sin = _rope_tables(n_ctx_tokens, n_lat_seq, lat_seq)
    head_id = jnp.arange(MXU_DIM) // HEAD_DIM
    ones = (head_id[:, None] == head_id[None, :]).astype(BF16)

    kc_all = cache_k.transpose(1, 3, 0, 2, 4).reshape(depth, N_KV_HEADS, n_lat_seq * past, HEAD_DIM).astype(BF16)
    vc_all = cache_v.transpose(1, 3, 0, 2, 4).reshape(depth, N_KV_HEADS, n_lat_seq * past, HEAD_DIM).astype(BF16)

    w_qkv_b = w_qkv.astype(BF16)
    w_o_b = w_o.astype(BF16)
    w_ffn_up_b = w_ffn_up.astype(BF16)
    w_ffn_down_b = w_ffn_down.astype(BF16)
    w_exp_up_b = w_exp_up.astype(BF16)
    w_exp_down_b = w_exp_down.astype(BF16)
    g_mix = norm_mix.reshape(depth, 1, d)
    g_ffn = norm_ffn.reshape(depth, 1, d)

    tf_expert = _pick_tile(w_exp_down.shape[2], (1792, 1024, 512, 256, 128))
    lat = dict(n_ctx_tokens=n_ctx_tokens, n_lat_seq=n_lat_seq, lat_seq=lat_seq, past=past)

    new_k = []
    new_v = []
    for l in range(depth):
        idx = l // 2
        use_a = (l % 2 == 0)
        if use_a:
            qn = jnp.tile(q_norm[idx], d // HEAD_DIM).reshape(1, d)
            kn = jnp.tile(k_norm[idx], N_KV_HEADS).reshape(1, kv_dim)
            sink = jnp.zeros((N_HEADS,), F32)
        else:
            qn = jnp.ones((1, d), F32)
            kn = jnp.ones((1, kv_dim), F32)
            sink = sinks[idx].astype(F32)

        q, kf, vf, kh, vh = _qkv(x, mods, g_mix, w_qkv_b, qn, kn, cos, sin, ones, layer=l, use_a=use_a, til=til)
        new_k.append(kf.reshape(n_ctx_seq, seq, N_KV_HEADS, HEAD_DIM))
        new_v.append(vf.reshape(n_ctx_seq, seq, N_KV_HEADS, HEAD_DIM))

        o_ctx = _attn_ctx(sink, q, kh, vh, n_ctx_seq=n_ctx_seq, seq=seq, has_sink=not use_a)
        if use_a:
            o_lat = _attn_global(q, kh, vh, kc_all, vc_all, layer=l, **lat)
        else:
            o_lat = _attn_window(sink, q, kh, vh, kc_all, vc_all, layer=l, **lat)
        x = _proj(o_ctx, o_lat, w_o_b, x, mods, layer=l, til=til)

        if use_a:
            x = _ffn(x, mods, g_ffn, w_ffn_up_b, w_ffn_down_b, layer=l, idx=idx, til=til)
        else:
            x = _moe(x, mods, g_ffn, w_router[idx], w_exp_up_b, w_exp_down_b, layer=l, idx=idx, til=til,
                     tf=tf_expert)

    gf = norm_final.reshape(1, d)
    y_prompt = _final_norm(x, gf, tm=tm, first_tile=0, n_tiles=til.n_ctx_tiles).reshape(n_ctx_seq, seq, d)
    y_sample = _final_norm(x, gf, tm=tm, first_tile=til.n_ctx_tiles, n_tiles=n_lat_tokens // tm)
    y_sample = y_sample.reshape(n_lat_seq, lat_seq, d)
    return (y_prompt, y_sample, jnp.stack(new_k, axis=1), jnp.stack(new_v, axis=1))
```

```python
import functools

import jax
import jax.numpy as jnp
from jax import lax
from jax.experimental import pallas as pl
from jax.experimental.pallas import tpu as pltpu

N_HEADS = 16
N_KV_HEADS = 4
GROUP = N_HEADS // N_KV_HEADS
HEAD_DIM = 64
GRID_W = 64
WINDOW = 128
ROPE_THETA = 10000.0
N_MOD = 6
TOP_K = 2
EPS = 1e-6
NEG_INF = -1e30
ATTN_SCALE = HEAD_DIM ** -0.5
LOG2E = 1.4426950408889634
V_WIDTH = 2 * HEAD_DIM

LANES = 128
SUBLANES = 8
MXU_DIM = 256
MOD_ROWS = 16
Q_TILE = 256
VMEM_LIMIT = 56 * 1024 * 1024

F32 = jnp.float32
BF16 = jnp.bfloat16
NT_DIMS = (((1,), (1,)), ((), ()))


def _params(sem, vmem=VMEM_LIMIT):
    return pltpu.CompilerParams(dimension_semantics=sem, vmem_limit_bytes=vmem)


def _pick_tile(n, candidates):
    for c in candidates:
        if n % c == 0:
            return c
    return n


def _ada_kernel(c_ref, w_ref, b_ref, o_ref):
    c = c_ref[...]
    a = c / (1.0 + jnp.exp(-c))
    o_ref[...] = jnp.dot(a, w_ref[...], preferred_element_type=F32,
                         precision=lax.Precision.HIGHEST) + b_ref[...]


def _ada_mods(cvecs, w_ada, b_ada):
    depth, d, n = w_ada.shape
    tn = _pick_tile(n, (1536, 1024, 512, 256, 128))
    return pl.pallas_call(
        _ada_kernel,
        grid=(depth, n // tn),
        in_specs=[
            pl.BlockSpec((MOD_ROWS, d), lambda l, j: (0, 0)),
            pl.BlockSpec((None, d, tn), lambda l, j: (l, 0, j)),
            pl.BlockSpec((None, 1, tn), lambda l, j: (l, 0, j)),
        ],
        out_specs=pl.BlockSpec((None, MOD_ROWS, tn), lambda l, j: (l, 0, j)),
        out_shape=jax.ShapeDtypeStruct((depth, MOD_ROWS, n), F32),
        compiler_params=_params(("arbitrary", "arbitrary")),
    )(cvecs, w_ada, b_ada.reshape(depth, 1, n))


def _norm_mod(x, g, shift, scale):
    ms = jnp.mean(x * x, axis=-1, keepdims=True)
    y = (x * lax.rsqrt(ms + EPS)) * g
    return y * (1.0 + scale) + shift


class _Tiling:
    def __init__(self, tm, n_ctx_tokens, lat_seq, n_lat_seq, d):
        self.tm = tm
        self.n_ctx_tiles = n_ctx_tokens // tm
        self.tiles_per_seq = lat_seq // tm
        self.ctx_row = n_lat_seq
        self.d = d

    def mod_spec(self, layer, which):
        def index(i, *_):
            row = jnp.where(i < self.n_ctx_tiles, self.ctx_row, (i - self.n_ctx_tiles) // self.tiles_per_seq)
            return (layer, row * N_MOD + which, 0, 0)
        return pl.BlockSpec((None, None, 1, self.d), index)


def _layer_vec_spec(layer, d):
    return pl.BlockSpec((None, 1, d), lambda *_: (layer, 0, 0))


def _qkv_kernel(x_ref, sh_ref, sc_ref, g_ref, w_ref, qn_ref, kn_ref, cos_ref, sin_ref, ones_ref,
                q_ref, kf_ref, vf_ref, kh_ref, vh_ref, h_ref, y_ref, ss_ref, *, use_a, d_model, n_ctx_tiles):
    q_dim = d_model
    kv_dim = N_KV_HEADS * HEAD_DIM
    is_ctx = pl.program_id(0) < n_ctx_tiles
    h_ref[...] = _norm_mod(x_ref[...], g_ref[...], sh_ref[...], sc_ref[...]).astype(BF16)

    cos = cos_ref[...]
    sin = sin_ref[...]
    even = (lax.broadcasted_iota(jnp.int32, cos.shape, 1) % 2) == 0

    for c0 in range(0, q_dim + 2 * kv_dim, MXU_DIM):
        y_ref[:, c0:c0 + MXU_DIM] = jnp.dot(h_ref[...], w_ref[:, c0:c0 + MXU_DIM], preferred_element_type=F32)
    if use_a:
        for c0 in range(0, q_dim + kv_dim, MXU_DIM):
            yc = y_ref[:, c0:c0 + MXU_DIM]
            ss_ref[:, c0:c0 + MXU_DIM] = jnp.dot((yc * yc).astype(BF16), ones_ref[...], preferred_element_type=F32)

    def project(col0, gain):
        xc = y_ref[:, col0:col0 + MXU_DIM]
        if use_a and gain is not None:
            ss = ss_ref[:, col0:col0 + MXU_DIM]
            xc = xc * lax.rsqrt(ss * (1.0 / HEAD_DIM) + EPS) * gain
        return xc

    def rope(xc):
        cols = []
        for l0 in range(0, MXU_DIM, LANES):
            xl = xc[:, l0:l0 + LANES]
            swapped = jnp.where(even, pltpu.roll(xl, LANES - 1, 1), pltpu.roll(xl, 1, 1))
            cols.append(xl * cos + swapped * sin)
        return jnp.concatenate(cols, axis=-1)

    for c0 in range(0, q_dim, MXU_DIM):
        qc = rope(project(c0, qn_ref[:, c0:c0 + MXU_DIM]))
        q_ref[:, c0:c0 + MXU_DIM] = (qc * (ATTN_SCALE * LOG2E)).astype(BF16)
    k = rope(project(q_dim, kn_ref[...]))
    v = project(q_dim + kv_dim, None)
    lane = lax.broadcasted_iota(jnp.int32, cos.shape, 1)
    ones_col = jnp.where(lane == HEAD_DIM, 1.0, 0.0)
    for hh in range(N_KV_HEADS):
        kh_ref[hh] = k[:, hh * HEAD_DIM:(hh + 1) * HEAD_DIM].astype(BF16)
        pair = v[:, (hh // 2) * LANES:(hh // 2 + 1) * LANES]
        if hh % 2:
            pair = pltpu.roll(pair, HEAD_DIM, 1)
        vh_ref[hh] = jnp.where(lane < HEAD_DIM, pair, ones_col).astype(BF16)

    @pl.when(is_ctx)
    def _():
        kf_ref[...] = k
        vf_ref[...] = v


def _qkv(x, mods, g_all, w_all, qn, kn, cos, sin, ones, *, layer, use_a, til):
    t, d = x.shape
    tm = til.tm
    kv_dim = N_KV_HEADS * HEAD_DIM
    n_out = w_all.shape[2]
    n_ctx_tiles = til.n_ctx_tiles
    row = lambda i: (i, 0)
    fixed = lambda i: (0, 0)
    ctx_row = lambda i: (jnp.minimum(i, n_ctx_tiles - 1), 0)
    return pl.pallas_call(
        functools.partial(_qkv_kernel, use_a=use_a, d_model=d, n_ctx_tiles=n_ctx_tiles),
        grid=(t // tm,),
        in_specs=[
            pl.BlockSpec((tm, d), row),
            til.mod_spec(layer, 0),
            til.mod_spec(layer, 1),
            _layer_vec_spec(layer, d),
            pl.BlockSpec((None, d, n_out), lambda i: (layer, 0, 0)),
            pl.BlockSpec((1, d), fixed),
            pl.BlockSpec((1, kv_dim), fixed),
            pl.BlockSpec((tm, LANES), row),
            pl.BlockSpec((tm, LANES), row),
            pl.BlockSpec((MXU_DIM, MXU_DIM), fixed),
        ],
        out_specs=[
            pl.BlockSpec((tm, d), row),
            pl.BlockSpec((tm, kv_dim), ctx_row),
            pl.BlockSpec((tm, kv_dim), ctx_row),
            pl.BlockSpec((N_KV_HEADS, tm, HEAD_DIM), lambda i: (0, i, 0)),
            pl.BlockSpec((N_KV_HEADS, tm, V_WIDTH), lambda i: (0, i, 0)),
        ],
        out_shape=[
            jax.ShapeDtypeStruct((t, d), BF16),
            jax.ShapeDtypeStruct((n_ctx_tiles * tm, kv_dim), F32),
            jax.ShapeDtypeStruct((n_ctx_tiles * tm, kv_dim), F32),
            jax.ShapeDtypeStruct((N_KV_HEADS, t, HEAD_DIM), BF16),
            jax.ShapeDtypeStruct((N_KV_HEADS, t, V_WIDTH), BF16),
        ],
        scratch_shapes=[pltpu.VMEM((tm, d), BF16), pltpu.VMEM((tm, n_out), F32),
                        pltpu.VMEM((tm, d + kv_dim), F32)],
        compiler_params=_params(("arbitrary",)),
    )(x, mods, mods, g_all, w_all, qn, kn, cos, sin, ones)


def _softmax_pv(score_parts, value_parts, sink):
    m = score_parts[0].max(axis=-1, keepdims=True)
    for s in score_parts[1:]:
        m = jnp.maximum(m, s.max(axis=-1, keepdims=True))
    if sink is not None:
        m = jnp.maximum(m, sink)
    acc = None
    for s, v in zip(score_parts, value_parts):
        pv = jnp.dot(jnp.exp2(s - m).astype(BF16), v, preferred_element_type=F32)
        acc = pv if acc is None else acc + pv
    denom = acc[:, HEAD_DIM:HEAD_DIM + 1]
    if sink is not None:
        denom = denom + jnp.exp2(sink - m)
    return acc[:, :HEAD_DIM] / denom


def _attn_ctx_kernel(sink_ref, q_ref, k_ref, v_ref, o_ref, s_ref, p_ref, m_ref, *, has_sink):
    def cols(head):
        return slice(head * HEAD_DIM, (head + 1) * HEAD_DIM)

    def scores(head):
        s_ref[head] = lax.dot_general(q_ref[:, cols(head)], k_ref[head // GROUP], NT_DIMS,
                                      preferred_element_type=F32)

    def numerators(head):
        s = s_ref[head]
        m = s.max(axis=-1, keepdims=True)
        if has_sink:
            m = jnp.maximum(m, sink_ref[head])
            m_ref[head] = jnp.broadcast_to(jnp.exp2(sink_ref[head] - m), m_ref.shape[1:])
        p_ref[head] = jnp.exp2(s - m).astype(BF16)

    def outputs(head):
        acc = jnp.dot(p_ref[head], v_ref[head // GROUP], preferred_element_type=F32)
        denom = acc[:, HEAD_DIM:HEAD_DIM + 1]
        if has_sink:
            denom = denom + m_ref[head][:, :1]
        o_ref[:, cols(head)] = (acc[:, :HEAD_DIM] / denom).astype(o_ref.dtype)

    def fused_outputs(head):
        sink = sink_ref[head] if has_sink else None
        o_ref[:, cols(head)] = _softmax_pv([s_ref[head]], [v_ref[head // GROUP]], sink).astype(o_ref.dtype)

    if has_sink:
        _phased(N_HEADS, (scores, fused_outputs))
    else:
        _phased(N_HEADS, (scores, numerators, outputs))


def _attn_ctx(sink, q, kh, vh, *, n_ctx_seq, seq, has_sink):
    t, d = q.shape
    return pl.pallas_call(
        functools.partial(_attn_ctx_kernel, has_sink=has_sink),
        grid=(n_ctx_seq,),
        in_specs=[
            pl.BlockSpec(memory_space=pltpu.SMEM),
            pl.BlockSpec((seq, d), lambda b: (b, 0)),
            pl.BlockSpec((N_KV_HEADS, seq, HEAD_DIM), lambda b: (0, b, 0)),
            pl.BlockSpec((N_KV_HEADS, seq, V_WIDTH), lambda b: (0, b, 0)),
        ],
        out_specs=pl.BlockSpec((seq, d), lambda b: (b, 0)),
        out_shape=jax.ShapeDtypeStruct((n_ctx_seq * seq, d), BF16),
        scratch_shapes=[pltpu.VMEM((N_HEADS, seq, seq), F32), pltpu.VMEM((N_HEADS, seq, seq), BF16),
                        pltpu.VMEM((N_HEADS, seq, LANES), F32)],
        compiler_params=_params(("arbitrary",)),
    )(sink, q, kh, vh)


def _phased(n, stages):
    for stage in stages:
        for j in range(n):
            stage(j)


def _attn_global_kernel(q_ref, k_ref, v_ref, kc_ref, vc_ref, o_ref, sl_ref, sc_ref):
    def cols(j):
        return slice(j * HEAD_DIM, (j + 1) * HEAD_DIM)

    def scores(j):
        qj = q_ref[:, cols(j)]
        sl_ref[j] = lax.dot_general(qj, k_ref[...], NT_DIMS, preferred_element_type=F32)
        sc_ref[j] = lax.dot_general(qj, kc_ref[...], NT_DIMS, preferred_element_type=F32)

    def outputs(j):
        o = _softmax_pv([sl_ref[j], sc_ref[j]], [v_ref[...], vc_ref[...]], None)
        o_ref[:, cols(j)] = o.astype(o_ref.dtype)

    _phased(GROUP, (scores, outputs))


def _cache_spec(layer, past, width):
    return pl.BlockSpec((None, None, past, width), lambda b, g, i: (layer, g, b, 0))


def _attn_global(q, kh, vh, kc_all, vc_all, *, layer, n_ctx_tokens, n_lat_seq, lat_seq, past):
    t, d = q.shape
    gw = GROUP * HEAD_DIM
    q_blocks = lat_seq // Q_TILE
    q0 = n_ctx_tokens // Q_TILE
    s0 = n_ctx_tokens // lat_seq
    return pl.pallas_call(
        _attn_global_kernel,
        grid=(n_lat_seq, N_KV_HEADS, q_blocks),
        in_specs=[
            pl.BlockSpec((Q_TILE, gw), lambda b, g, i: (q0 + b * q_blocks + i, g)),
            pl.BlockSpec((None, lat_seq, HEAD_DIM), lambda b, g, i: (g, s0 + b, 0)),
            pl.BlockSpec((None, lat_seq, V_WIDTH), lambda b, g, i: (g, s0 + b, 0)),
            _cache_spec(layer, past, HEAD_DIM),
            _cache_spec(layer, past, V_WIDTH),
        ],
        out_specs=pl.BlockSpec((Q_TILE, gw), lambda b, g, i: (b * q_blocks + i, g)),
        out_shape=jax.ShapeDtypeStruct((n_lat_seq * lat_seq, d), BF16),
        scratch_shapes=[pltpu.VMEM((GROUP, Q_TILE, lat_seq), F32), pltpu.VMEM((GROUP, Q_TILE, past), F32)],
        compiler_params=_params(("arbitrary", "arbitrary", "arbitrary")),
    )(q, kh, vh, kc_all, vc_all)


def _attn_window_kernel(sink_ref, q_ref, kp_ref, kcur_ref, kn_ref, vp_ref, vcur_ref, vn_ref,
                        kc_ref, vc_ref, o_ref, sl_ref, sc_ref, *, lat_seq):
    grp = pl.program_id(1)
    start = pl.program_id(2) * Q_TILE
    span = Q_TILE + 2 * WINDOW
    kw = jnp.concatenate([kp_ref[...], kcur_ref[...], kn_ref[...]], axis=0)
    vw = jnp.concatenate([vp_ref[...], vcur_ref[...], vn_ref[...]], axis=0)
    qpos = start + lax.broadcasted_iota(jnp.int32, (Q_TILE, span), 0)
    kpos = start - WINDOW + lax.broadcasted_iota(jnp.int32, (Q_TILE, span), 1)
    valid = (kpos >= 0) & (kpos < lat_seq) & (jnp.abs(qpos - kpos) <= WINDOW)

    def cols(j):
        return slice(j * HEAD_DIM, (j + 1) * HEAD_DIM)

    def scores(j):
        qj = q_ref[:, cols(j)]
        s_loc = lax.dot_general(qj, kw, NT_DIMS, preferred_element_type=F32)
        sl_ref[j] = jnp.where(valid, s_loc, NEG_INF)
        sc_ref[j] = lax.dot_general(qj, kc_ref[...], NT_DIMS, preferred_element_type=F32)

    def outputs(j):
        o = _softmax_pv([sl_ref[j], sc_ref[j]], [vw, vc_ref[...]], sink_ref[grp * GROUP + j])
        o_ref[:, cols(j)] = o.astype(o_ref.dtype)

    _phased(GROUP, (scores, outputs))


def _attn_window(sink, q, kh, vh, kc_all, vc_all, *, layer, n_ctx_tokens, n_lat_seq, lat_seq, past):
    t, d = q.shape
    gw = GROUP * HEAD_DIM
    q_blocks = lat_seq // Q_TILE
    q0 = n_ctx_tokens // Q_TILE
    span = Q_TILE + 2 * WINDOW
    half = Q_TILE // WINDOW
    w0 = n_ctx_tokens // WINDOW
    w_blocks = lat_seq // WINDOW
    prev = lambda b, g, i: (g, w0 + b * w_blocks + jnp.maximum(i * half - 1, 0), 0)
    cur = lambda b, g, i: (g, q0 + b * q_blocks + i, 0)
    nxt = lambda b, g, i: (g, w0 + b * w_blocks + jnp.minimum((i + 1) * half, w_blocks - 1), 0)
    def band_specs(width):
        return [
            pl.BlockSpec((None, WINDOW, width), prev),
            pl.BlockSpec((None, Q_TILE, width), cur),
            pl.BlockSpec((None, WINDOW, width), nxt),
        ]
    return pl.pallas_call(
        functools.partial(_attn_window_kernel, lat_seq=lat_seq),
        grid=(n_lat_seq, N_KV_HEADS, q_blocks),
        in_specs=[
            pl.BlockSpec(memory_space=pltpu.SMEM),
            pl.BlockSpec((Q_TILE, gw), lambda b, g, i: (q0 + b * q_blocks + i, g)),
            *band_specs(HEAD_DIM), *band_specs(V_WIDTH),
            _cache_spec(layer, past, HEAD_DIM),
            _cache_spec(layer, past, V_WIDTH),
        ],
        out_specs=pl.BlockSpec((Q_TILE, gw), lambda b, g, i: (b * q_blocks + i, g)),
        out_shape=jax.ShapeDtypeStruct((n_lat_seq * lat_seq, d), BF16),
        scratch_shapes=[pltpu.VMEM((GROUP, Q_TILE, span), F32), pltpu.VMEM((GROUP, Q_TILE, past), F32)],
        compiler_params=_params(("arbitrary", "arbitrary", "arbitrary")),
    )(sink, q, kh, kh, kh, vh, vh, vh, kc_all, vc_all)


def _proj_kernel(oc_ref, ol_ref, w_ref, x_ref, gate_ref, out_ref, *, n_ctx_tiles):
    def project(o_ref):
        y = jnp.dot(o_ref[...], w_ref[...], preferred_element_type=F32)
        out_ref[...] = x_ref[...] + gate_ref[...] * y

    @pl.when(pl.program_id(0) < n_ctx_tiles)
    def _():
        project(oc_ref)

    @pl.when(pl.program_id(0) >= n_ctx_tiles)
    def _():
        project(ol_ref)


def _proj(o_ctx, o_lat, w_all, x, mods, *, layer, til):
    t, d = x.shape
    tm = til.tm
    n_ctx_tiles = til.n_ctx_tiles
    row = lambda i: (i, 0)
    return pl.pallas_call(
        functools.partial(_proj_kernel, n_ctx_tiles=n_ctx_tiles),
        grid=(t // tm,),
        in_specs=[
            pl.BlockSpec((tm, d), lambda i: (jnp.minimum(i, n_ctx_tiles - 1), 0)),
            pl.BlockSpec((tm, d), lambda i: (jnp.maximum(i - n_ctx_tiles, 0), 0)),
            pl.BlockSpec((None, d, d), lambda i: (layer, 0, 0)),
            pl.BlockSpec((tm, d), row),
            til.mod_spec(layer, 2),
        ],
        out_specs=pl.BlockSpec((tm, d), row),
        out_shape=jax.ShapeDtypeStruct((t, d), F32),
        compiler_params=_params(("arbitrary",)),
    )(o_ctx, o_lat, w_all, x, mods)


def _swiglu_act(h, wg, wu):
    g = jnp.dot(h, wg, preferred_element_type=F32)
    u = jnp.dot(h, wu, preferred_element_type=F32)
    return (g / (1.0 + jnp.exp(-g))) * u


def _ffn_kernel(x_ref, sh_ref, sc_ref, gate_ref, g_ref, wg_ref, wu_ref, wd_ref, out_ref):
    x = x_ref[...]
    h = _norm_mod(x, g_ref[...], sh_ref[...], sc_ref[...]).astype(BF16)
    a = _swiglu_act(h, wg_ref[...], wu_ref[...])
    f = jnp.dot(a.astype(BF16), wd_ref[...], preferred_element_type=F32)
    out_ref[...] = x + gate_ref[...] * f


def _ffn(x, mods, g_all, w_up_all, w_down_all, *, layer, idx, til):
    t, d = x.shape
    tm = til.tm
    d_ff = w_down_all.shape[1]
    row = lambda i: (i, 0)
    resident = pl.Buffered(1)
    return pl.pallas_call(
        _ffn_kernel,
        grid=(t // tm,),
        in_specs=[
            pl.BlockSpec((tm, d), row),
            til.mod_spec(layer, 3),
            til.mod_spec(layer, 4),
            til.mod_spec(layer, 5),
            _layer_vec_spec(layer, d),
            pl.BlockSpec((None, d, d_ff), lambda i: (idx, 0, 0), pipeline_mode=resident),
            pl.BlockSpec((None, d, d_ff), lambda i: (idx, 0, 1), pipeline_mode=resident),
            pl.BlockSpec((None, d_ff, d), lambda i: (idx, 0, 0), pipeline_mode=resident),
        ],
        out_specs=pl.BlockSpec((tm, d), row),
        out_shape=jax.ShapeDtypeStruct((t, d), F32),
        compiler_params=_params(("arbitrary",)),
    )(x, mods, mods, mods, g_all, w_up_all, w_up_all, w_down_all)


META_IDX, META_GATE, META_RANK = 0, 2, 4


def _router_kernel(x_ref, sh_ref, sc_ref, g_ref, wr_ref, tri_ref, meta_ref, meta_t_ref, cnt_ref, run_ref,
                   *, n_experts):
    @pl.when(pl.program_id(0) == 0)
    def _():
        run_ref[...] = jnp.zeros_like(run_ref)

    h = _norm_mod(x_ref[...], g_ref[...], sh_ref[...], sc_ref[...])
    logits = jnp.dot(h, wr_ref[...], preferred_element_type=F32, precision=lax.Precision.HIGHEST)
    lane = lax.broadcasted_iota(jnp.int32, logits.shape, 1).astype(F32)
    logits = jnp.where(lane < n_experts, logits, -jnp.inf)
    top1 = logits.max(axis=-1, keepdims=True)
    idx1 = jnp.where(logits == top1, lane, float(LANES)).min(axis=-1, keepdims=True)
    rest = jnp.where(lane == idx1, -jnp.inf, logits)
    top2 = rest.max(axis=-1, keepdims=True)
    idx2 = jnp.where(rest == top2, lane, float(LANES)).min(axis=-1, keepdims=True)
    e = jnp.exp(top2 - top1)
    g1 = 1.0 / (1.0 + e)
    g2 = e / (1.0 + e)

    sel1 = lane == idx1
    sel2 = lane == idx2
    sel = jnp.where(sel1, 1.0, 0.0) + jnp.where(sel2, 1.0, 0.0)
    before = jnp.dot(tri_ref[...], sel.astype(BF16), preferred_element_type=F32) + run_ref[...]
    r1 = jnp.where(sel1, before, 0.0).sum(axis=-1, keepdims=True)
    r2 = jnp.where(sel2, before, 0.0).sum(axis=-1, keepdims=True)
    run_ref[...] += sel.sum(axis=0, keepdims=True)
    cnt_ref[...] = run_ref[...]

    meta = jnp.zeros_like(logits)
    for off, (a, b) in ((META_IDX, (idx1, idx2)), (META_GATE, (g1, g2)), (META_RANK, (r1, r2))):
        meta = jnp.where(lane == off, a, meta)
        meta = jnp.where(lane == off + 1, b, meta)
    meta_ref[...] = meta
    meta_t_ref[...] = meta.T[:SUBLANES]


def _router(x, mods, g_all, wr, tri, *, layer, n_experts, til):
    t, d = x.shape
    tm = til.tm
    row = lambda i: (i, 0)
    fixed = lambda i: (0, 0)
    return pl.pallas_call(
        functools.partial(_router_kernel, n_experts=n_experts),
        grid=(t // tm,),
        in_specs=[
            pl.BlockSpec((tm, d), row),
            til.mod_spec(layer, 3),
            til.mod_spec(layer, 4),
            _layer_vec_spec(layer, d),
            pl.BlockSpec((d, LANES), fixed),
            pl.BlockSpec((tm, tm), fixed),
        ],
        out_specs=[pl.BlockSpec((tm, LANES), row), pl.BlockSpec((SUBLANES, tm), lambda i: (0, i)),
                   pl.BlockSpec((1, LANES), fixed)],
        out_shape=[jax.ShapeDtypeStruct((t, LANES), F32), jax.ShapeDtypeStruct((SUBLANES, t), F32),
                   jax.ShapeDtypeStruct((1, LANES), F32)],
        scratch_shapes=[pltpu.VMEM((1, LANES), F32)],
        compiler_params=_params(("arbitrary",)),
    )(x, mods, mods, g_all, wr, tri)


def _row_copy(src_ref, src_row, dst_ref, dst_row, sem):
    return pltpu.make_async_copy(src_ref.at[pl.ds(src_row, 1)], dst_ref.at[pl.ds(dst_row, 1)], sem)


def _dispatch_kernel(pos_ref, x_ref, sh_ref, sc_ref, g_ref, hs_init_ref, hs_ref, h_ref, sem, *, tm, n_tokens):
    del hs_init_ref
    i = pl.program_id(0)
    slot = i % 2

    def wait_slot(s):
        for _ in range(TOP_K * tm):
            _row_copy(h_ref.at[s], 0, hs_ref, 0, sem.at[s]).wait()

    @pl.when(i >= 2)
    def _():
        wait_slot(slot)

    h_ref[slot] = _norm_mod(x_ref[...], g_ref[...], sh_ref[...], sc_ref[...])

    def send(r, carry):
        for k in range(TOP_K):
            dst = pos_ref[k * n_tokens + i * tm + r]
            _row_copy(h_ref.at[slot], r, hs_ref, dst, sem.at[slot]).start(priority=k)
        return carry
    lax.fori_loop(0, tm, send, 0, unroll=8)

    @pl.when(i == pl.num_programs(0) - 1)
    def _():
        wait_slot(slot)

        @pl.when(i >= 1)
        def _():
            wait_slot(1 - slot)


def _dispatch(pos, x, mods, g_all, hs_init, *, layer, til):
    t, d = x.shape
    tm = til.tm
    return pl.pallas_call(
        functools.partial(_dispatch_kernel, tm=tm, n_tokens=t),
        grid_spec=pltpu.PrefetchScalarGridSpec(
            num_scalar_prefetch=1,
            grid=(t // tm,),
            in_specs=[
                pl.BlockSpec((tm, d), lambda i, pos: (i, 0)),
                til.mod_spec(layer, 3),
                til.mod_spec(layer, 4),
                _layer_vec_spec(layer, d),
                pl.BlockSpec(memory_space=pl.ANY),
            ],
            out_specs=pl.BlockSpec(memory_space=pl.ANY),
            scratch_shapes=[pltpu.VMEM((2, tm, d), F32), pltpu.SemaphoreType.DMA((2,))],
        ),
        out_shape=jax.ShapeDtypeStruct(hs_init.shape, F32),
        input_output_aliases={5: 0},
        compiler_params=_params(("arbitrary",)),
    )(pos, x, mods, mods, g_all, hs_init)


def _expert_kernel(te_ref, tv_ref, hs_ref, wg_ref, wu_ref, wd_ref, ys_ref, h_ref, acc_ref):
    del te_ref
    j = pl.program_id(1)

    @pl.when(tv_ref[pl.program_id(0)] != 0)
    def _():
        @pl.when(j == 0)
        def _():
            h_ref[...] = hs_ref[...].astype(BF16)
            acc_ref[...] = jnp.zeros_like(acc_ref)

        a = _swiglu_act(h_ref[...], wg_ref[...], wu_ref[...])
        acc_ref[...] += jnp.dot(a.astype(BF16), wd_ref[...], preferred_element_type=F32)

        @pl.when(j == pl.num_programs(1) - 1)
        def _():
            ys_ref[...] = acc_ref[...]

    @pl.when((tv_ref[pl.program_id(0)] == 0) & (j == 0))
    def _():
        ys_ref[...] = jnp.zeros_like(ys_ref)


def _experts(tile_expert, tile_valid, hs, w_up_all, w_down_all, *, idx, tm, tf):
    p, d = hs.shape
    d_ff = w_down_all.shape[2]
    nf = d_ff // tf
    return pl.pallas_call(
        _expert_kernel,
        grid_spec=pltpu.PrefetchScalarGridSpec(
            num_scalar_prefetch=2,
            grid=(p // tm, nf),
            in_specs=[
                pl.BlockSpec((tm, d), lambda r, j, te, tv: (r, 0)),
                pl.BlockSpec((None, None, d, tf), lambda r, j, te, tv: (idx, te[r], 0, j)),
                pl.BlockSpec((None, None, d, tf), lambda r, j, te, tv: (idx, te[r], 0, nf + j)),
                pl.BlockSpec((None, None, tf, d), lambda r, j, te, tv: (idx, te[r], j, 0)),
            ],
            out_specs=pl.BlockSpec((tm, d), lambda r, j, te, tv: (r, 0)),
            scratch_shapes=[pltpu.VMEM((tm, d), BF16), pltpu.VMEM((tm, d), F32)],
        ),
        out_shape=jax.ShapeDtypeStruct((p, d), F32),
        compiler_params=_params(("arbitrary", "arbitrary")),
    )(tile_expert, tile_valid, hs, w_up_all, w_up_all, w_down_all)


def _combine_kernel(pos_ref, ys_ref, meta_ref, x_ref, gate_ref, out_ref, y_ref, sem, *, tm, n_tokens):
    i = pl.program_id(0)
    slot = i % 2

    def fetch(step, s):
        def body(r, carry):
            for k in range(TOP_K):
                src = pos_ref[k * n_tokens + step * tm + r]
                _row_copy(ys_ref, src, y_ref.at[s, k], r, sem.at[s]).start(priority=k)
            return carry
        lax.fori_loop(0, tm, body, 0, unroll=8)

    @pl.when(i == 0)
    def _():
        fetch(0, 0)

    @pl.when(i + 1 < pl.num_programs(0))
    def _():
        fetch(i + 1, 1 - slot)

    for _ in range(tm):
        for k in range(TOP_K):
            _row_copy(ys_ref, 0, y_ref.at[slot, k], 0, sem.at[slot]).wait()

    meta = meta_ref[...]
    f = meta[:, META_GATE:META_GATE + 1] * y_ref[slot, 0] + meta[:, META_GATE + 1:META_GATE + 2] * y_ref[slot, 1]
    out_ref[...] = x_ref[...] + gate_ref[...] * f


def _combine(pos, ys, meta, x, mods, *, layer, til):
    t, d = x.shape
    tm = til.tm
    row = lambda i, pos: (i, 0)
    return pl.pallas_call(
        functools.partial(_combine_kernel, tm=tm, n_tokens=t),
        grid_spec=pltpu.PrefetchScalarGridSpec(
            num_scalar_prefetch=1,
            grid=(t // tm,),
            in_specs=[
                pl.BlockSpec(memory_space=pl.ANY),
                pl.BlockSpec((tm, LANES), row),
                pl.BlockSpec((tm, d), row),
                til.mod_spec(layer, 5),
            ],
            out_specs=pl.BlockSpec((tm, d), row),
            scratch_shapes=[pltpu.VMEM((2, TOP_K, tm, d), F32), pltpu.SemaphoreType.DMA((2,))],
        ),
        out_shape=jax.ShapeDtypeStruct((t, d), F32),
        compiler_params=_params(("arbitrary",)),
    )(pos, ys, meta, x, mods)


def _moe(x, mods, g_all, w_router, w_up_all, w_down_all, *, layer, idx, til, tf):
    t, d = x.shape
    n_experts = w_router.shape[-1]
    tm = til.tm
    wr = jnp.zeros((d, LANES), F32).at[:, :n_experts].set(w_router)
    tri = (jnp.arange(tm)[:, None] > jnp.arange(tm)[None, :]).astype(BF16)
    meta, meta_t, counts = _router(x, mods, g_all, wr, tri, layer=layer, n_experts=n_experts, til=til)

    counts = counts[0, :n_experts].astype(jnp.int32)
    padded = ((counts + tm - 1) // tm) * tm
    ends = jnp.cumsum(padded)
    starts = ends - padded
    choice = meta_t[META_IDX:META_IDX + TOP_K].astype(jnp.int32)
    rank = meta_t[META_RANK:META_RANK + TOP_K].astype(jnp.int32)
    group_start = sum(jnp.where(choice == e, starts[e], 0) for e in range(n_experts))
    pos = (group_start + rank).reshape(TOP_K * t)
    n_tiles = (t * TOP_K) // tm + n_experts
    tile_start = jnp.arange(n_tiles, dtype=jnp.int32) * tm
    tile_expert = jnp.minimum((tile_start[:, None] >= ends[None, :]).sum(axis=-1), n_experts - 1).astype(jnp.int32)
    tile_valid = (tile_start < ends[-1]).astype(jnp.int32)

    hs = _dispatch(pos, x, mods, g_all, jnp.zeros((n_tiles * tm, d), F32), layer=layer, til=til)
    ys = _experts(tile_expert, tile_valid, hs, w_up_all, w_down_all, idx=idx, tm=tm, tf=tf)
    return _combine(pos, ys, meta, x, mods, layer=layer, til=til)


def _final_norm_kernel(x_ref, g_ref, o_ref):
    x = x_ref[...]
    ms = jnp.mean(x * x, axis=-1, keepdims=True)
    o_ref[...] = (x * lax.rsqrt(ms + EPS)) * g_ref[...]


def _final_norm(x, g, *, tm, first_tile, n_tiles):
    d = x.shape[1]
    return pl.pallas_call(
        _final_norm_kernel,
        grid=(n_tiles,),
        in_specs=[pl.BlockSpec((tm, d), lambda i: (first_tile + i, 0)), pl.BlockSpec((1, d), lambda i: (0, 0))],
        out_specs=pl.BlockSpec((tm, d), lambda i: (i, 0)),
        out_shape=jax.ShapeDtypeStruct((n_tiles * tm, d), F32),
        compiler_params=_params(("arbitrary",)),
    )(x, g)


def _rope_tables(n_ctx_tokens, n_lat_seq, lat_seq):
    pos = jnp.arange(lat_seq)
    rows = (pos // GRID_W).astype(F32)
    cols = (pos % GRID_W).astype(F32)
    pairs_per_axis = HEAD_DIM // 4
    inv = ROPE_THETA ** (-jnp.arange(pairs_per_axis, dtype=F32) / pairs_per_axis)
    ang = jnp.concatenate([rows[:, None] * inv, cols[:, None] * inv], axis=-1)
    cos = jnp.repeat(jnp.cos(ang), 2, axis=-1)
    sin = jnp.repeat(jnp.sin(ang), 2, axis=-1) * jnp.tile(jnp.array([-1.0, 1.0], F32), HEAD_DIM // 2)
    reps = LANES // HEAD_DIM
    cos = jnp.tile(cos, (n_lat_seq, reps))
    sin = jnp.tile(sin, (n_lat_seq, reps))
    cos = jnp.concatenate([jnp.ones((n_ctx_tokens, LANES), F32), cos], axis=0)
    sin = jnp.concatenate([jnp.zeros((n_ctx_tokens, LANES), F32), sin], axis=0)
    return cos, sin


def kernel(x_prompt, x_sample, cache_k, cache_v, c, c_ctx, w_ada, b_ada, norm_mix, norm_ffn, norm_final,
           w_qkv, w_o, q_norm, k_norm, sinks, w_ffn_up, w_ffn_down, w_router, w_exp_up, w_exp_down):
    n_ctx_seq, seq, d = x_prompt.shape
    n_lat_seq, lat_seq, _ = x_sample.shape
    depth = w_ada.shape[0]
    past = cache_k.shape[2]
    n_experts = w_router.shape[-1]
    kv_dim = N_KV_HEADS * HEAD_DIM
    n_ctx_tokens = n_ctx_seq * seq
    n_lat_tokens = n_lat_seq * lat_seq
    assert d == N_HEADS * HEAD_DIM and seq == Q_TILE and lat_seq % Q_TILE == 0 and kv_dim == MXU_DIM
    assert n_ctx_tokens % lat_seq == 0 and n_lat_seq < MOD_ROWS and n_experts <= LANES

    tm = 2 * Q_TILE if (lat_seq % (2 * Q_TILE) == 0 and n_ctx_tokens % (2 * Q_TILE) == 0) else Q_TILE
    til = _Tiling(tm, n_ctx_tokens, lat_seq, n_lat_seq, d)

    x = jnp.concatenate([x_prompt.reshape(n_ctx_tokens, d), x_sample.reshape(n_lat_tokens, d)], axis=0)

    cvecs = jnp.zeros((MOD_ROWS, d), F32).at[:n_lat_seq].set(c).at[n_lat_seq].set(c_ctx)
    mods = _ada_mods(cvecs, w_ada, b_ada).reshape(depth, MOD_ROWS * N_MOD, 1, d)

    cos, sin = _rope_tables(n_ctx_tokens, n_lat_seq, lat_seq)
    head_id = jnp.arange(MXU_DIM) // HEAD_DIM
    ones = (head_id[:, None] == head_id[None, :]).astype(BF16)

    kc_all = cache_k.transpose(1, 3, 0, 2, 4).reshape(depth, N_KV_HEADS, n_lat_seq * past, HEAD_DIM).astype(BF16)
    vc_all = cache_v.transpose(1, 3, 0, 2, 4).reshape(depth, N_KV_HEADS, n_lat_seq * past, HEAD_DIM).astype(BF16)
    ones_col = (jnp.arange(V_WIDTH - HEAD_DIM) == 0).astype(BF16)
    vc_all = jnp.concatenate([vc_all, jnp.broadcast_to(ones_col, vc_all.shape[:-1] + ones_col.shape)], axis=-1)

    w_qkv_b = w_qkv.astype(BF16)
    w_o_b = w_o.astype(BF16)
    w_ffn_up_b = w_ffn_up.astype(BF16)
    w_ffn_down_b = w_ffn_down.astype(BF16)
    w_exp_up_b = w_exp_up.astype(BF16)
    w_exp_down_b = w_exp_down.astype(BF16)
    g_mix = norm_mix.reshape(depth, 1, d)
    g_ffn = norm_ffn.reshape(depth, 1, d)

    tf_expert = _pick_tile(w_exp_down.shape[2], (1792, 1024, 512, 256, 128))
    lat = dict(n_ctx_tokens=n_ctx_tokens, n_lat_seq=n_lat_seq, lat_seq=lat_seq, past=past)

    new_k = []
    new_v = []
    for l in range(depth):
        idx = l // 2
        use_a = (l % 2 == 0)
        if use_a:
            qn = jnp.tile(q_norm[idx], d // HEAD_DIM).reshape(1, d)
            kn = jnp.tile(k_norm[idx], N_KV_HEADS).reshape(1, kv_dim)
            sink = jnp.zeros((N_HEADS,), F32)
        else:
            qn = jnp.ones((1, d), F32)
            kn = jnp.ones((1, kv_dim), F32)
            sink = sinks[idx].astype(F32) * LOG2E

        q, kf, vf, kh, vh = _qkv(x, mods, g_mix, w_qkv_b, qn, kn, cos, sin, ones, layer=l, use_a=use_a, til=til)
        new_k.append(kf.reshape(n_ctx_seq, seq, N_KV_HEADS, HEAD_DIM))
        new_v.append(vf.reshape(n_ctx_seq, seq, N_KV_HEADS, HEAD_DIM))

        o_ctx = _attn_ctx(sink, q, kh, vh, n_ctx_seq=n_ctx_seq, seq=seq, has_sink=not use_a)
        if use_a:
            o_lat = _attn_global(q, kh, vh, kc_all, vc_all, layer=l, **lat)
        else:
            o_lat = _attn_window(sink, q, kh, vh, kc_all, vc_all, layer=l, **lat)
        x = _proj(o_ctx, o_lat, w_o_b, x, mods, layer=l, til=til)

        if use_a:
            x = _ffn(x, mods, g_ffn, w_ffn_up_b, w_ffn_down_b, layer=l, idx=idx, til=til)
        else:
            x = _moe(x, mods, g_ffn, w_router[idx], w_exp_up_b, w_exp_down_b, layer=l, idx=idx, til=til,
                     tf=tf_expert)

    gf = norm_final.reshape(1, d)
    y_prompt = _final_norm(x, gf, tm=tm, first_tile=0, n_tiles=til.n_ctx_tiles).reshape(n_ctx_seq, seq, d)
    y_sample = _final_norm(x, gf, tm=tm, first_tile=til.n_ctx_tiles, n_tiles=n_lat_tokens // tm)
    y_sample = y_sample.reshape(n_lat_seq, lat_seq, d)
    return (y_prompt, y_sample, jnp.stack(new_k, axis=1), jnp.stack(new_v, axis=1))
```

```python
import functools

import jax
import jax.numpy as jnp
from jax import lax
from jax.experimental import pallas as pl
from jax.experimental.pallas import tpu as pltpu

N_HEADS = 16
N_KV_HEADS = 4
GROUP = N_HEADS // N_KV_HEADS
HEAD_DIM = 64
GRID_W = 64
WINDOW = 128
ROPE_THETA = 10000.0
N_MOD = 6
TOP_K = 2
EPS = 1e-6
NEG_INF = -1e30
ATTN_SCALE = HEAD_DIM ** -0.5
LOG2E = 1.4426950408889634
V_WIDTH = 2 * HEAD_DIM

LANES = 128
SUBLANES = 8
MXU_DIM = 256
MOD_ROWS = 16
Q_TILE = 256
VMEM_LIMIT = 56 * 1024 * 1024

F32 = jnp.float32
BF16 = jnp.bfloat16
NT_DIMS = (((1,), (1,)), ((), ()))


def _params(sem, vmem=VMEM_LIMIT):
    return pltpu.CompilerParams(dimension_semantics=sem, vmem_limit_bytes=vmem)


def _pick_tile(n, candidates):
    for c in candidates:
        if n % c == 0:
            return c
    return n


def _ada_kernel(c_ref, w_ref, b_ref, o_ref):
    c = c_ref[...]
    a = c / (1.0 + jnp.exp(-c))
    o_ref[...] = jnp.dot(a, w_ref[...], preferred_element_type=F32,
                         precision=lax.Precision.HIGHEST) + b_ref[...]


def _ada_mods(cvecs, w_ada, b_ada):
    depth, d, n = w_ada.shape
    tn = _pick_tile(n, (1536, 1024, 512, 256, 128))
    return pl.pallas_call(
        _ada_kernel,
        grid=(depth, n // tn),
        in_specs=[
            pl.BlockSpec((MOD_ROWS, d), lambda l, j: (0, 0)),
            pl.BlockSpec((None, d, tn), lambda l, j: (l, 0, j)),
            pl.BlockSpec((None, 1, tn), lambda l, j: (l, 0, j)),
        ],
        out_specs=pl.BlockSpec((None, MOD_ROWS, tn), lambda l, j: (l, 0, j)),
        out_shape=jax.ShapeDtypeStruct((depth, MOD_ROWS, n), F32),
        compiler_params=_params(("arbitrary", "arbitrary")),
    )(cvecs, w_ada, b_ada.reshape(depth, 1, n))


def _norm_mod(x, g, shift, scale):
    ms = jnp.mean(x * x, axis=-1, keepdims=True)
    y = (x * lax.rsqrt(ms + EPS)) * g
    return y * (1.0 + scale) + shift


class _Tiling:
    def __init__(self, tm, n_ctx_tokens, lat_seq, n_lat_seq, d):
        self.tm = tm
        self.n_ctx_tiles = n_ctx_tokens // tm
        self.tiles_per_seq = lat_seq // tm
        self.ctx_row = n_lat_seq
        self.d = d

    def mod_spec(self, layer, which):
        def index(i, *_):
            row = jnp.where(i < self.n_ctx_tiles, self.ctx_row, (i - self.n_ctx_tiles) // self.tiles_per_seq)
            return (layer, row * N_MOD + which, 0, 0)
        return pl.BlockSpec((None, None, 1, self.d), index)


def _layer_vec_spec(layer, d):
    return pl.BlockSpec((None, 1, d), lambda *_: (layer, 0, 0))


def _qkv_kernel(x_ref, sh_ref, sc_ref, g_ref, w_ref, qn_ref, kn_ref, cos_ref, sin_ref, ones_ref,
                q_ref, kf_ref, vf_ref, kh_ref, vh_ref, h_ref, y_ref, ss_ref, *, use_a, d_model, n_ctx_tiles):
    q_dim = d_model
    kv_dim = N_KV_HEADS * HEAD_DIM
    is_ctx = pl.program_id(0) < n_ctx_tiles
    h_ref[...] = _norm_mod(x_ref[...], g_ref[...], sh_ref[...], sc_ref[...]).astype(BF16)

    cos = cos_ref[...]
    sin = sin_ref[...]
    even = (lax.broadcasted_iota(jnp.int32, cos.shape, 1) % 2) == 0

    for c0 in range(0, q_dim + 2 * kv_dim, MXU_DIM):
        y_ref[:, c0:c0 + MXU_DIM] = jnp.dot(h_ref[...], w_ref[:, c0:c0 + MXU_DIM], preferred_element_type=F32)
    if use_a:
        for c0 in range(0, q_dim + kv_dim, MXU_DIM):
            yc = y_ref[:, c0:c0 + MXU_DIM]
            ss_ref[:, c0:c0 + MXU_DIM] = jnp.dot((yc * yc).astype(BF16), ones_ref[...], preferred_element_type=F32)

    def project(col0, gain):
        xc = y_ref[:, col0:col0 + MXU_DIM]
        if use_a and gain is not None:
            ss = ss_ref[:, col0:col0 + MXU_DIM]
            xc = xc * lax.rsqrt(ss * (1.0 / HEAD_DIM) + EPS) * gain
        return xc

    def rope(xc):
        cols = []
        for l0 in range(0, MXU_DIM, LANES):
            xl = xc[:, l0:l0 + LANES]
            swapped = jnp.where(even, pltpu.roll(xl, LANES - 1, 1), pltpu.roll(xl, 1, 1))
            cols.append(xl * cos + swapped * sin)
        return jnp.concatenate(cols, axis=-1)

    for c0 in range(0, q_dim, MXU_DIM):
        qc = rope(project(c0, qn_ref[:, c0:c0 + MXU_DIM]))
        q_ref[:, c0:c0 + MXU_DIM] = (qc * (ATTN_SCALE * LOG2E)).astype(BF16)
    k = rope(project(q_dim, kn_ref[...]))
    v = project(q_dim + kv_dim, None)
    lane = lax.broadcasted_iota(jnp.int32, cos.shape, 1)
    ones_col = jnp.where(lane == HEAD_DIM, 1.0, 0.0)
    for hh in range(N_KV_HEADS):
        kh_ref[hh] = k[:, hh * HEAD_DIM:(hh + 1) * HEAD_DIM].astype(BF16)
        pair = v[:, (hh // 2) * LANES:(hh // 2 + 1) * LANES]
        if hh % 2:
            pair = pltpu.roll(pair, HEAD_DIM, 1)
        vh_ref[hh] = jnp.where(lane < HEAD_DIM, pair, ones_col).astype(BF16)

    @pl.when(is_ctx)
    def _():
        kf_ref[...] = k
        vf_ref[...] = v


def _qkv(x, mods, g_all, w_all, qn, kn, cos, sin, ones, *, layer, use_a, til):
    t, d = x.shape
    tm = til.tm
    kv_dim = N_KV_HEADS * HEAD_DIM
    n_out = w_all.shape[2]
    n_ctx_tiles = til.n_ctx_tiles
    row = lambda i: (i, 0)
    fixed = lambda i: (0, 0)
    ctx_row = lambda i: (jnp.minimum(i, n_ctx_tiles - 1), 0)
    return pl.pallas_call(
        functools.partial(_qkv_kernel, use_a=use_a, d_model=d, n_ctx_tiles=n_ctx_tiles),
        grid=(t // tm,),
        in_specs=[
            pl.BlockSpec((tm, d), row),
            til.mod_spec(layer, 0),
            til.mod_spec(layer, 1),
            _layer_vec_spec(layer, d),
            pl.BlockSpec((None, d, n_out), lambda i: (layer, 0, 0)),
            pl.BlockSpec((1, d), fixed),
            pl.BlockSpec((1, kv_dim), fixed),
            pl.BlockSpec((tm, LANES), row),
            pl.BlockSpec((tm, LANES), row),
            pl.BlockSpec((MXU_DIM, MXU_DIM), fixed),
        ],
        out_specs=[
            pl.BlockSpec((tm, d), row),
            pl.BlockSpec((tm, kv_dim), ctx_row),
            pl.BlockSpec((tm, kv_dim), ctx_row),
            pl.BlockSpec((N_KV_HEADS, tm, HEAD_DIM), lambda i: (0, i, 0)),
            pl.BlockSpec((N_KV_HEADS, tm, V_WIDTH), lambda i: (0, i, 0)),
        ],
        out_shape=[
            jax.ShapeDtypeStruct((t, d), BF16),
            jax.ShapeDtypeStruct((n_ctx_tiles * tm, kv_dim), F32),
            jax.ShapeDtypeStruct((n_ctx_tiles * tm, kv_dim), F32),
            jax.ShapeDtypeStruct((N_KV_HEADS, t, HEAD_DIM), BF16),
            jax.ShapeDtypeStruct((N_KV_HEADS, t, V_WIDTH), BF16),
        ],
        scratch_shapes=[pltpu.VMEM((tm, d), BF16), pltpu.VMEM((tm, n_out), F32),
                        pltpu.VMEM((tm, d + kv_dim), F32)],
        compiler_params=_params(("arbitrary",)),
    )(x, mods, mods, g_all, w_all, qn, kn, cos, sin, ones)


def _softmax_pv(score_parts, value_parts, sink):
    m = score_parts[0].max(axis=-1, keepdims=True)
    for s in score_parts[1:]:
        m = jnp.maximum(m, s.max(axis=-1, keepdims=True))
    if sink is not None:
        m = jnp.maximum(m, sink)
    acc = None
    for s, v in zip(score_parts, value_parts):
        pv = jnp.dot(jnp.exp2(s - m).astype(BF16), v, preferred_element_type=F32)
        acc = pv if acc is None else acc + pv
    denom = acc[:, HEAD_DIM:HEAD_DIM + 1]
    if sink is not None:
        denom = denom + jnp.exp2(sink - m)
    return acc[:, :HEAD_DIM] / denom


def _attn_ctx_kernel(sink_ref, q_ref, k_ref, v_ref, o_ref, s_ref, p_ref, m_ref, *, has_sink):
    def cols(head):
        return slice(head * HEAD_DIM, (head + 1) * HEAD_DIM)

    def scores(head):
        s_ref[head] = lax.dot_general(q_ref[:, cols(head)], k_ref[head // GROUP], NT_DIMS,
                                      preferred_element_type=F32)

    def numerators(head):
        s = s_ref[head]
        m = s.max(axis=-1, keepdims=True)
        if has_sink:
            m = jnp.maximum(m, sink_ref[head])
            m_ref[head] = jnp.broadcast_to(jnp.exp2(sink_ref[head] - m), m_ref.shape[1:])
        p_ref[head] = jnp.exp2(s - m).astype(BF16)

    def outputs(head):
        acc = jnp.dot(p_ref[head], v_ref[head // GROUP], preferred_element_type=F32)
        denom = acc[:, HEAD_DIM:HEAD_DIM + 1]
        if has_sink:
            denom = denom + m_ref[head][:, :1]
        o_ref[:, cols(head)] = (acc[:, :HEAD_DIM] / denom).astype(o_ref.dtype)

    def fused_outputs(head):
        sink = sink_ref[head] if has_sink else None
        o_ref[:, cols(head)] = _softmax_pv([s_ref[head]], [v_ref[head // GROUP]], sink).astype(o_ref.dtype)

    if has_sink:
        _phased(N_HEADS, (scores, fused_outputs))
    else:
        _phased(N_HEADS, (scores, numerators, outputs))


def _attn_ctx(sink, q, kh, vh, *, n_ctx_seq, seq, has_sink):
    t, d = q.shape
    return pl.pallas_call(
        functools.partial(_attn_ctx_kernel, has_sink=has_sink),
        grid=(n_ctx_seq,),
        in_specs=[
            pl.BlockSpec(memory_space=pltpu.SMEM),
            pl.BlockSpec((seq, d), lambda b: (b, 0)),
            pl.BlockSpec((N_KV_HEADS, seq, HEAD_DIM), lambda b: (0, b, 0)),
            pl.BlockSpec((N_KV_HEADS, seq, V_WIDTH), lambda b: (0, b, 0)),
        ],
        out_specs=pl.BlockSpec((seq, d), lambda b: (b, 0)),
        out_shape=jax.ShapeDtypeStruct((n_ctx_seq * seq, d), BF16),
        scratch_shapes=[pltpu.VMEM((N_HEADS, seq, seq), F32), pltpu.VMEM((N_HEADS, seq, seq), BF16),
                        pltpu.VMEM((N_HEADS, seq, LANES), F32)],
        compiler_params=_params(("arbitrary",)),
    )(sink, q, kh, vh)


def _phased(n, stages):
    for stage in stages:
        for j in range(n):
            stage(j)


def _attn_global_kernel(q_ref, k_ref, v_ref, kc_ref, vc_ref, o_ref, sl_ref, sc_ref):
    def cols(j):
        return slice(j * HEAD_DIM, (j + 1) * HEAD_DIM)

    def scores(j):
        qj = q_ref[:, cols(j)]
        sl_ref[j] = lax.dot_general(qj, k_ref[...], NT_DIMS, preferred_element_type=F32)
        sc_ref[j] = lax.dot_general(qj, kc_ref[...], NT_DIMS, preferred_element_type=F32)

    def outputs(j):
        o = _softmax_pv([sl_ref[j], sc_ref[j]], [v_ref[...], vc_ref[...]], None)
        o_ref[:, cols(j)] = o.astype(o_ref.dtype)

    _phased(GROUP, (scores, outputs))


def _cache_spec(layer, past, width):
    return pl.BlockSpec((None, None, past, width), lambda b, g, i: (layer, g, b, 0))


def _attn_global(q, kh, vh, kc_all, vc_all, *, layer, n_ctx_tokens, n_lat_seq, lat_seq, past):
    t, d = q.shape
    gw = GROUP * HEAD_DIM
    q_blocks = lat_seq // Q_TILE
    q0 = n_ctx_tokens // Q_TILE
    s0 = n_ctx_tokens // lat_seq
    return pl.pallas_call(
        _attn_global_kernel,
        grid=(n_lat_seq, N_KV_HEADS, q_blocks),
        in_specs=[
            pl.BlockSpec((Q_TILE, gw), lambda b, g, i: (q0 + b * q_blocks + i, g)),
            pl.BlockSpec((None, lat_seq, HEAD_DIM), lambda b, g, i: (g, s0 + b, 0)),
            pl.BlockSpec((None, lat_seq, V_WIDTH), lambda b, g, i: (g, s0 + b, 0)),
            _cache_spec(layer, past, HEAD_DIM),
            _cache_spec(layer, past, V_WIDTH),
        ],
        out_specs=pl.BlockSpec((Q_TILE, gw), lambda b, g, i: (b * q_blocks + i, g)),
        out_shape=jax.ShapeDtypeStruct((n_lat_seq * lat_seq, d), BF16),
        scratch_shapes=[pltpu.VMEM((GROUP, Q_TILE, lat_seq), F32), pltpu.VMEM((GROUP, Q_TILE, past), F32)],
        compiler_params=_params(("arbitrary", "arbitrary", "arbitrary")),
    )(q, kh, vh, kc_all, vc_all)


def _attn_window_kernel(sink_ref, q_ref, kp_ref, kcur_ref, kn_ref, vp_ref, vcur_ref, vn_ref,
                        kc_ref, vc_ref, o_ref, sl_ref, sc_ref, *, lat_seq):
    grp = pl.program_id(1)
    start = pl.program_id(2) * Q_TILE
    span = Q_TILE + 2 * WINDOW
    kw = jnp.concatenate([kp_ref[...], kcur_ref[...], kn_ref[...]], axis=0)
    vw = jnp.concatenate([vp_ref[...], vcur_ref[...], vn_ref[...]], axis=0)
    qpos = start + lax.broadcasted_iota(jnp.int32, (Q_TILE, span), 0)
    kpos = start - WINDOW + lax.broadcasted_iota(jnp.int32, (Q_TILE, span), 1)
    valid = (kpos >= 0) & (kpos < lat_seq) & (jnp.abs(qpos - kpos) <= WINDOW)

    def cols(j):
        return slice(j * HEAD_DIM, (j + 1) * HEAD_DIM)

    def scores(j):
        qj = q_ref[:, cols(j)]
        s_loc = lax.dot_general(qj, kw, NT_DIMS, preferred_element_type=F32)
        sl_ref[j] = jnp.where(valid, s_loc, NEG_INF)
        sc_ref[j] = lax.dot_general(qj, kc_ref[...], NT_DIMS, preferred_element_type=F32)

    def outputs(j):
        o = _softmax_pv([sl_ref[j], sc_ref[j]], [vw, vc_ref[...]], sink_ref[grp * GROUP + j])
        o_ref[:, cols(j)] = o.astype(o_ref.dtype)

    _phased(GROUP, (scores, outputs))


def _attn_window(sink, q, kh, vh, kc_all, vc_all, *, layer, n_ctx_tokens, n_lat_seq, lat_seq, past):
    t, d = q.shape
    gw = GROUP * HEAD_DIM
    q_blocks = lat_seq // Q_TILE
    q0 = n_ctx_tokens // Q_TILE
    span = Q_TILE + 2 * WINDOW
    half = Q_TILE // WINDOW
    w0 = n_ctx_tokens // WINDOW
    w_blocks = lat_seq // WINDOW
    prev = lambda b, g, i: (g, w0 + b * w_blocks + jnp.maximum(i * half - 1, 0), 0)
    cur = lambda b, g, i: (g, q0 + b * q_blocks + i, 0)
    nxt = lambda b, g, i: (g, w0 + b * w_blocks + jnp.minimum((i + 1) * half, w_blocks - 1), 0)
    def band_specs(width):
        return [
            pl.BlockSpec((None, WINDOW, width), prev),
            pl.BlockSpec((None, Q_TILE, width), cur),
            pl.BlockSpec((None, WINDOW, width), nxt),
        ]
    return pl.pallas_call(
        functools.partial(_attn_window_kernel, lat_seq=lat_seq),
        grid=(n_lat_seq, N_KV_HEADS, q_blocks),
        in_specs=[
            pl.BlockSpec(memory_space=pltpu.SMEM),
            pl.BlockSpec((Q_TILE, gw), lambda b, g, i: (q0 + b * q_blocks + i, g)),
            *band_specs(HEAD_DIM), *band_specs(V_WIDTH),
            _cache_spec(layer, past, HEAD_DIM),
            _cache_spec(layer, past, V_WIDTH),
        ],
        out_specs=pl.BlockSpec((Q_TILE, gw), lambda b, g, i: (b * q_blocks + i, g)),
        out_shape=jax.ShapeDtypeStruct((n_lat_seq * lat_seq, d), BF16),
        scratch_shapes=[pltpu.VMEM((GROUP, Q_TILE, span), F32), pltpu.VMEM((GROUP, Q_TILE, past), F32)],
        compiler_params=_params(("arbitrary", "arbitrary", "arbitrary")),
    )(sink, q, kh, kh, kh, vh, vh, vh, kc_all, vc_all)


def _attn_residual(oc_ref, ol_ref, wo_ref, x_ref, gate_ref, dst_ref, n_ctx_tiles):
    def project(o_ref):
        y = jnp.dot(o_ref[...], wo_ref[...], preferred_element_type=F32)
        dst_ref[...] = x_ref[...] + gate_ref[...] * y

    @pl.when(pl.program_id(0) < n_ctx_tiles)
    def _():
        project(oc_ref)

    @pl.when(pl.program_id(0) >= n_ctx_tiles)
    def _():
        project(ol_ref)


def _attn_residual_specs(layer, til):
    tm, d, n_ctx_tiles = til.tm, til.d, til.n_ctx_tiles
    return [
        pl.BlockSpec((tm, d), lambda i, *_: (jnp.minimum(i, n_ctx_tiles - 1), 0)),
        pl.BlockSpec((tm, d), lambda i, *_: (jnp.maximum(i - n_ctx_tiles, 0), 0)),
        pl.BlockSpec((None, d, d), lambda i, *_: (layer, 0, 0), pipeline_mode=pl.Buffered(1)),
        pl.BlockSpec((tm, d), lambda i, *_: (i, 0)),
        til.mod_spec(layer, 2),
    ]


def _swiglu_act(h, wg, wu):
    g = jnp.dot(h, wg, preferred_element_type=F32)
    u = jnp.dot(h, wu, preferred_element_type=F32)
    return (g / (1.0 + jnp.exp(-g))) * u


def _ffn_kernel(oc_ref, ol_ref, wo_ref, x_ref, g1_ref, sh_ref, sc_ref, gate_ref, g_ref, wg_ref, wu_ref, wd_ref,
                out_ref, x1_ref, *, n_ctx_tiles):
    _attn_residual(oc_ref, ol_ref, wo_ref, x_ref, g1_ref, x1_ref, n_ctx_tiles)
    x = x1_ref[...]
    h = _norm_mod(x, g_ref[...], sh_ref[...], sc_ref[...]).astype(BF16)
    a = _swiglu_act(h, wg_ref[...], wu_ref[...])
    f = jnp.dot(a.astype(BF16), wd_ref[...], preferred_element_type=F32)
    out_ref[...] = x + gate_ref[...] * f


def _ffn(o_ctx, o_lat, w_o_all, x, mods, g_all, w_up_all, w_down_all, *, layer, idx, til):
    t, d = x.shape
    tm = til.tm
    d_ff = w_down_all.shape[1]
    row = lambda i: (i, 0)
    resident = pl.Buffered(1)
    return pl.pallas_call(
        functools.partial(_ffn_kernel, n_ctx_tiles=til.n_ctx_tiles),
        grid=(t // tm,),
        in_specs=[
            *_attn_residual_specs(layer, til),
            til.mod_spec(layer, 3),
            til.mod_spec(layer, 4),
            til.mod_spec(layer, 5),
            _layer_vec_spec(layer, d),
            pl.BlockSpec((None, d, d_ff), lambda i: (idx, 0, 0), pipeline_mode=resident),
            pl.BlockSpec((None, d, d_ff), lambda i: (idx, 0, 1), pipeline_mode=resident),
            pl.BlockSpec((None, d_ff, d), lambda i: (idx, 0, 0), pipeline_mode=resident),
        ],
        out_specs=pl.BlockSpec((tm, d), row),
        out_shape=jax.ShapeDtypeStruct((t, d), F32),
        scratch_shapes=[pltpu.VMEM((tm, d), F32)],
        compiler_params=_params(("arbitrary",)),
    )(o_ctx, o_lat, w_o_all, x, mods, mods, mods, mods, g_all, w_up_all, w_up_all, w_down_all)


META_IDX, META_GATE, META_RANK = 0, 2, 4


def _router_kernel(oc_ref, ol_ref, wo_ref, x_ref, g1_ref, sh_ref, sc_ref, g_ref, wr_ref, tri_ref,
                   x1_ref, meta_ref, meta_t_ref, cnt_ref, run_ref, *, n_experts, n_ctx_tiles):
    @pl.when(pl.program_id(0) == 0)
    def _():
        run_ref[...] = jnp.zeros_like(run_ref)

    _attn_residual(oc_ref, ol_ref, wo_ref, x_ref, g1_ref, x1_ref, n_ctx_tiles)
    h = _norm_mod(x1_ref[...], g_ref[...], sh_ref[...], sc_ref[...])
    logits = jnp.dot(h, wr_ref[...], preferred_element_type=F32, precision=lax.Precision.HIGHEST)
    lane = lax.broadcasted_iota(jnp.int32, logits.shape, 1).astype(F32)
    logits = jnp.where(lane < n_experts, logits, -jnp.inf)
    top1 = logits.max(axis=-1, keepdims=True)
    idx1 = jnp.where(logits == top1, lane, float(LANES)).min(axis=-1, keepdims=True)
    rest = jnp.where(lane == idx1, -jnp.inf, logits)
    top2 = rest.max(axis=-1, keepdims=True)
    idx2 = jnp.where(rest == top2, lane, float(LANES)).min(axis=-1, keepdims=True)
    e = jnp.exp(top2 - top1)
    g1 = 1.0 / (1.0 + e)
    g2 = e / (1.0 + e)

    sel1 = lane == idx1
    sel2 = lane == idx2
    sel = jnp.where(sel1, 1.0, 0.0) + jnp.where(sel2, 1.0, 0.0)
    before = jnp.dot(tri_ref[...], sel.astype(BF16), preferred_element_type=F32) + run_ref[...]
    r1 = jnp.where(sel1, before, 0.0).sum(axis=-1, keepdims=True)
    r2 = jnp.where(sel2, before, 0.0).sum(axis=-1, keepdims=True)
    run_ref[...] += sel.sum(axis=0, keepdims=True)
    cnt_ref[...] = run_ref[...]

    meta = jnp.zeros_like(logits)
    for off, (a, b) in ((META_IDX, (idx1, idx2)), (META_GATE, (g1, g2)), (META_RANK, (r1, r2))):
        meta = jnp.where(lane == off, a, meta)
        meta = jnp.where(lane == off + 1, b, meta)
    meta_ref[...] = meta
    meta_t_ref[...] = meta.T[:SUBLANES]


def _router(o_ctx, o_lat, w_o_all, x, mods, g_all, wr, tri, *, layer, n_experts, til):
    t, d = x.shape
    tm = til.tm
    row = lambda i: (i, 0)
    fixed = lambda i: (0, 0)
    return pl.pallas_call(
        functools.partial(_router_kernel, n_experts=n_experts, n_ctx_tiles=til.n_ctx_tiles),
        grid=(t // tm,),
        in_specs=[
            *_attn_residual_specs(layer, til),
            til.mod_spec(layer, 3),
            til.mod_spec(layer, 4),
            _layer_vec_spec(layer, d),
            pl.BlockSpec((d, LANES), fixed),
            pl.BlockSpec((tm, tm), fixed),
        ],
        out_specs=[pl.BlockSpec((tm, d), row), pl.BlockSpec((tm, LANES), row),
                   pl.BlockSpec((SUBLANES, tm), lambda i: (0, i)), pl.BlockSpec((1, LANES), fixed)],
        out_shape=[jax.ShapeDtypeStruct((t, d), F32), jax.ShapeDtypeStruct((t, LANES), F32),
                   jax.ShapeDtypeStruct((SUBLANES, t), F32), jax.ShapeDtypeStruct((1, LANES), F32)],
        scratch_shapes=[pltpu.VMEM((1, LANES), F32)],
        compiler_params=_params(("arbitrary",)),
    )(o_ctx, o_lat, w_o_all, x, mods, mods, mods, g_all, wr, tri)


def _row_copy(src_ref, src_row, dst_ref, dst_row, sem):
    return pltpu.make_async_copy(src_ref.at[pl.ds(src_row, 1)], dst_ref.at[pl.ds(dst_row, 1)], sem)


def _dispatch_kernel(pos_ref, x_ref, sh_ref, sc_ref, g_ref, hs_init_ref, hs_ref, h_ref, sem, *, tm, n_tokens):
    del hs_init_ref
    i = pl.program_id(0)
    slot = i % 2

    def wait_slot(s):
        for _ in range(TOP_K * tm):
            _row_copy(h_ref.at[s], 0, hs_ref, 0, sem.at[s]).wait()

    @pl.when(i >= 2)
    def _():
        wait_slot(slot)

    h_ref[slot] = _norm_mod(x_ref[...], g_ref[...], sh_ref[...], sc_ref[...])

    def send(r, carry):
        for k in range(TOP_K):
            dst = pos_ref[k * n_tokens + i * tm + r]
            _row_copy(h_ref.at[slot], r, hs_ref, dst, sem.at[slot]).start(priority=k)
        return carry
    lax.fori_loop(0, tm, send, 0, unroll=8)

    @pl.when(i == pl.num_programs(0) - 1)
    def _():
        wait_slot(slot)

        @pl.when(i >= 1)
        def _():
            wait_slot(1 - slot)


def _dispatch(pos, x, mods, g_all, hs_init, *, layer, til):
    t, d = x.shape
    tm = til.tm
    return pl.pallas_call(
        functools.partial(_dispatch_kernel, tm=tm, n_tokens=t),
        grid_spec=pltpu.PrefetchScalarGridSpec(
            num_scalar_prefetch=1,
            grid=(t // tm,),
            in_specs=[
                pl.BlockSpec((tm, d), lambda i, pos: (i, 0)),
                til.mod_spec(layer, 3),
                til.mod_spec(layer, 4),
                _layer_vec_spec(layer, d),
                pl.BlockSpec(memory_space=pl.ANY),
            ],
            out_specs=pl.BlockSpec(memory_space=pl.ANY),
            scratch_shapes=[pltpu.VMEM((2, tm, d), F32), pltpu.SemaphoreType.DMA((2,))],
        ),
        out_shape=jax.ShapeDtypeStruct(hs_init.shape, F32),
        input_output_aliases={5: 0},
        compiler_params=_params(("arbitrary",)),
    )(pos, x, mods, mods, g_all, hs_init)


def _expert_kernel(te_ref, tv_ref, hs_ref, wg_ref, wu_ref, wd_ref, ys_ref, h_ref, acc_ref):
    del te_ref
    j = pl.program_id(1)

    @pl.when(tv_ref[pl.program_id(0)] != 0)
    def _():
        @pl.when(j == 0)
        def _():
            h_ref[...] = hs_ref[...].astype(BF16)
            acc_ref[...] = jnp.zeros_like(acc_ref)

        a = _swiglu_act(h_ref[...], wg_ref[...], wu_ref[...])
        acc_ref[...] += jnp.dot(a.astype(BF16), wd_ref[...], preferred_element_type=F32)

        @pl.when(j == pl.num_programs(1) - 1)
        def _():
            ys_ref[...] = acc_ref[...]

    @pl.when((tv_ref[pl.program_id(0)] == 0) & (j == 0))
    def _():
        ys_ref[...] = jnp.zeros_like(ys_ref)


def _experts(tile_expert, tile_valid, hs, w_up_all, w_down_all, *, idx, tm, tf):
    p, d = hs.shape
    d_ff = w_down_all.shape[2]
    nf = d_ff // tf
    return pl.pallas_call(
        _expert_kernel,
        grid_spec=pltpu.PrefetchScalarGridSpec(
            num_scalar_prefetch=2,
            grid=(p // tm, nf),
            in_specs=[
                pl.BlockSpec((tm, d), lambda r, j, te, tv: (r, 0)),
                pl.BlockSpec((None, None, d, tf), lambda r, j, te, tv: (idx, te[r], 0, j)),
                pl.BlockSpec((None, None, d, tf), lambda r, j, te, tv: (idx, te[r], 0, nf + j)),
                pl.BlockSpec((None, None, tf, d), lambda r, j, te, tv: (idx, te[r], j, 0)),
            ],
            out_specs=pl.BlockSpec((tm, d), lambda r, j, te, tv: (r, 0)),
            scratch_shapes=[pltpu.VMEM((tm, d), BF16), pltpu.VMEM((tm, d), F32)],
        ),
        out_shape=jax.ShapeDtypeStruct((p, d), F32),
        compiler_params=_params(("arbitrary", "arbitrary")),
    )(tile_expert, tile_valid, hs, w_up_all, w_up_all, w_down_all)


def _combine_kernel(pos_ref, ys_ref, meta_ref, x_ref, gate_ref, *rest, tm, n_tokens, n_ctx_tiles, final):
    if final:
        gf_ref, yc_ref, yl_ref, y_ref, sem = rest
    else:
        out_ref, y_ref, sem = rest
    i = pl.program_id(0)
    slot = i % 2

    def fetch(step, s):
        def body(r, carry):
            for k in range(TOP_K):
                src = pos_ref[k * n_tokens + step * tm + r]
                _row_copy(ys_ref, src, y_ref.at[s, k], r, sem.at[s]).start(priority=k)
            return carry
        lax.fori_loop(0, tm, body, 0, unroll=8)

    @pl.when(i == 0)
    def _():
        fetch(0, 0)

    @pl.when(i + 1 < pl.num_programs(0))
    def _():
        fetch(i + 1, 1 - slot)

    for _ in range(tm):
        for k in range(TOP_K):
            _row_copy(ys_ref, 0, y_ref.at[slot, k], 0, sem.at[slot]).wait()

    meta = meta_ref[...]
    f = meta[:, META_GATE:META_GATE + 1] * y_ref[slot, 0] + meta[:, META_GATE + 1:META_GATE + 2] * y_ref[slot, 1]
    x_new = x_ref[...] + gate_ref[...] * f
    if not final:
        out_ref[...] = x_new
        return

    ms = jnp.mean(x_new * x_new, axis=-1, keepdims=True)
    y = (x_new * lax.rsqrt(ms + EPS)) * gf_ref[...]

    @pl.when(i < n_ctx_tiles)
    def _():
        yc_ref[...] = y

    @pl.when(i >= n_ctx_tiles)
    def _():
        yl_ref[...] = y


def _combine(pos, ys, meta, x, mods, final_gain, *, layer, til):
    t, d = x.shape
    tm = til.tm
    n_ctx_tiles = til.n_ctx_tiles
    final = final_gain is not None
    row = lambda i, pos: (i, 0)
    in_specs = [
        pl.BlockSpec(memory_space=pl.ANY),
        pl.BlockSpec((tm, LANES), row),
        pl.BlockSpec((tm, d), row),
        til.mod_spec(layer, 5),
    ]
    args = [pos, ys, meta, x, mods]
    if final:
        in_specs.append(pl.BlockSpec((1, d), lambda i, pos: (0, 0)))
        args.append(final_gain)
        out_specs = [pl.BlockSpec((tm, d), lambda i, pos: (jnp.minimum(i, n_ctx_tiles - 1), 0)),
                     pl.BlockSpec((tm, d), lambda i, pos: (jnp.maximum(i - n_ctx_tiles, 0), 0))]
        out_shape = [jax.ShapeDtypeStruct((n_ctx_tiles * tm, d), F32),
                     jax.ShapeDtypeStruct((t - n_ctx_tiles * tm, d), F32)]
    else:
        out_specs = pl.BlockSpec((tm, d), row)
        out_shape = jax.ShapeDtypeStruct((t, d), F32)
    return pl.pallas_call(
        functools.partial(_combine_kernel, tm=tm, n_tokens=t, n_ctx_tiles=n_ctx_tiles, final=final),
        grid_spec=pltpu.PrefetchScalarGridSpec(
            num_scalar_prefetch=1,
            grid=(t // tm,),
            in_specs=in_specs,
            out_specs=out_specs,
            scratch_shapes=[pltpu.VMEM((2, TOP_K, tm, d), F32), pltpu.SemaphoreType.DMA((2,))],
        ),
        out_shape=out_shape,
        compiler_params=_params(("arbitrary",)),
    )(*args)


def _moe(o_ctx, o_lat, w_o_all, x, mods, g_all, w_router, w_up_all, w_down_all, final_gain, sorted_buf,
         *, layer, idx, til, tf):
    t, d = x.shape
    n_experts = w_router.shape[-1]
    tm = til.tm
    wr = jnp.zeros((d, LANES), F32).at[:, :n_experts].set(w_router)
    tri = (jnp.arange(tm)[:, None] > jnp.arange(tm)[None, :]).astype(BF16)
    x, meta, meta_t, counts = _router(o_ctx, o_lat, w_o_all, x, mods, g_all, wr, tri, layer=layer,
                                      n_experts=n_experts, til=til)

    counts = counts[0, :n_experts].astype(jnp.int32)
    padded = ((counts + tm - 1) // tm) * tm
    ends = jnp.cumsum(padded)
    starts = ends - padded
    choice = meta_t[META_IDX:META_IDX + TOP_K].astype(jnp.int32)
    rank = meta_t[META_RANK:META_RANK + TOP_K].astype(jnp.int32)
    group_start = sum(jnp.where(choice == e, starts[e], 0) for e in range(n_experts))
    pos = (group_start + rank).reshape(TOP_K * t)
    n_tiles = (t * TOP_K) // tm + n_experts
    tile_start = jnp.arange(n_tiles, dtype=jnp.int32) * tm
    tile_expert = jnp.minimum((tile_start[:, None] >= ends[None, :]).sum(axis=-1), n_experts - 1).astype(jnp.int32)
    tile_valid = (tile_start < ends[-1]).astype(jnp.int32)

    if sorted_buf is None:
        sorted_buf = jnp.zeros((n_tiles * tm, d), F32)
    hs = _dispatch(pos, x, mods, g_all, sorted_buf, layer=layer, til=til)
    ys = _experts(tile_expert, tile_valid, hs, w_up_all, w_down_all, idx=idx, tm=tm, tf=tf)
    return _combine(pos, ys, meta, x, mods, final_gain, layer=layer, til=til), hs


def _final_norm_kernel(x_ref, g_ref, o_ref):
    x = x_ref[...]
    ms = jnp.mean(x * x, axis=-1, keepdims=True)
    o_ref[...] = (x * lax.rsqrt(ms + EPS)) * g_ref[...]


def _final_norm(x, g, *, tm, first_tile, n_tiles):
    d = x.shape[1]
    return pl.pallas_call(
        _final_norm_kernel,
        grid=(n_tiles,),
        in_specs=[pl.BlockSpec((tm, d), lambda i: (first_tile + i, 0)), pl.BlockSpec((1, d), lambda i: (0, 0))],
        out_specs=pl.BlockSpec((tm, d), lambda i: (i, 0)),
        out_shape=jax.ShapeDtypeStruct((n_tiles * tm, d), F32),
        compiler_params=_params(("arbitrary",)),
    )(x, g)


def _rope_tables(n_ctx_tokens, n_lat_seq, lat_seq):
    pos = jnp.arange(lat_seq)
    rows = (pos // GRID_W).astype(F32)
    cols = (pos % GRID_W).astype(F32)
    pairs_per_axis = HEAD_DIM // 4
    inv = ROPE_THETA ** (-jnp.arange(pairs_per_axis, dtype=F32) / pairs_per_axis)
    ang = jnp.concatenate([rows[:, None] * inv, cols[:, None] * inv], axis=-1)
    cos = jnp.repeat(jnp.cos(ang), 2, axis=-1)
    sin = jnp.repeat(jnp.sin(ang), 2, axis=-1) * jnp.tile(jnp.array([-1.0, 1.0], F32), HEAD_DIM // 2)
    reps = LANES // HEAD_DIM
    cos = jnp.tile(cos, (n_lat_seq, reps))
    sin = jnp.tile(sin, (n_lat_seq, reps))
    cos = jnp.concatenate([jnp.ones((n_ctx_tokens, LANES), F32), cos], axis=0)
    sin = jnp.concatenate([jnp.zeros((n_ctx_tokens, LANES), F32), sin], axis=0)
    return cos, sin


def kernel(x_prompt, x_sample, cache_k, cache_v, c, c_ctx, w_ada, b_ada, norm_mix, norm_ffn, norm_final,
           w_qkv, w_o, q_norm, k_norm, sinks, w_ffn_up, w_ffn_down, w_router, w_exp_up, w_exp_down):
    n_ctx_seq, seq, d = x_prompt.shape
    n_lat_seq, lat_seq, _ = x_sample.shape
    depth = w_ada.shape[0]
    past = cache_k.shape[2]
    n_experts = w_router.shape[-1]
    kv_dim = N_KV_HEADS * HEAD_DIM
    n_ctx_tokens = n_ctx_seq * seq
    n_lat_tokens = n_lat_seq * lat_seq
    assert d == N_HEADS * HEAD_DIM and seq == Q_TILE and lat_seq % Q_TILE == 0 and kv_dim == MXU_DIM
    assert n_ctx_tokens % lat_seq == 0 and n_lat_seq < MOD_ROWS and n_experts <= LANES

    tm = 2 * Q_TILE if (lat_seq % (2 * Q_TILE) == 0 and n_ctx_tokens % (2 * Q_TILE) == 0) else Q_TILE
    til = _Tiling(tm, n_ctx_tokens, lat_seq, n_lat_seq, d)

    x = jnp.concatenate([x_prompt.reshape(n_ctx_tokens, d), x_sample.reshape(n_lat_tokens, d)], axis=0)

    cvecs = jnp.zeros((MOD_ROWS, d), F32).at[:n_lat_seq].set(c).at[n_lat_seq].set(c_ctx)
    mods = _ada_mods(cvecs, w_ada, b_ada).reshape(depth, MOD_ROWS * N_MOD, 1, d)

    cos, sin = _rope_tables(n_ctx_tokens, n_lat_seq, lat_seq)
    head_id = jnp.arange(MXU_DIM) // HEAD_DIM
    ones = (head_id[:, None] == head_id[None, :]).astype(BF16)

    kc_all = cache_k.transpose(1, 3, 0, 2, 4).reshape(depth, N_KV_HEADS, n_lat_seq * past, HEAD_DIM).astype(BF16)
    vc_all = cache_v.transpose(1, 3, 0, 2, 4).reshape(depth, N_KV_HEADS, n_lat_seq * past, HEAD_DIM).astype(BF16)
    ones_col = (jnp.arange(V_WIDTH - HEAD_DIM) == 0).astype(BF16)
    vc_all = jnp.concatenate([vc_all, jnp.broadcast_to(ones_col, vc_all.shape[:-1] + ones_col.shape)], axis=-1)

    w_qkv_b = w_qkv.astype(BF16)
    w_o_b = w_o.astype(BF16)
    w_ffn_up_b = w_ffn_up.astype(BF16)
    w_ffn_down_b = w_ffn_down.astype(BF16)
    w_exp_up_b = w_exp_up.astype(BF16)
    w_exp_down_b = w_exp_down.astype(BF16)
    g_mix = norm_mix.reshape(depth, 1, d)
    g_ffn = norm_ffn.reshape(depth, 1, d)
    gf = norm_final.reshape(1, d)

    tf_expert = _pick_tile(w_exp_down.shape[2], (1792, 1024, 512, 256, 128))
    lat = dict(n_ctx_tokens=n_ctx_tokens, n_lat_seq=n_lat_seq, lat_seq=lat_seq, past=past)

    new_k = []
    new_v = []
    sorted_buf = None
    for l in range(depth):
        idx = l // 2
        use_a = (l % 2 == 0)
        if use_a:
            qn = jnp.tile(q_norm[idx], d // HEAD_DIM).reshape(1, d)
            kn = jnp.tile(k_norm[idx], N_KV_HEADS).reshape(1, kv_dim)
            sink = jnp.zeros((N_HEADS,), F32)
        else:
            qn = jnp.ones((1, d), F32)
            kn = jnp.ones((1, kv_dim), F32)
            sink = sinks[idx].astype(F32) * LOG2E

        q, kf, vf, kh, vh = _qkv(x, mods, g_mix, w_qkv_b, qn, kn, cos, sin, ones, layer=l, use_a=use_a, til=til)
        new_k.append(kf.reshape(n_ctx_seq, seq, N_KV_HEADS, HEAD_DIM))
        new_v.append(vf.reshape(n_ctx_seq, seq, N_KV_HEADS, HEAD_DIM))

        o_ctx = _attn_ctx(sink, q, kh, vh, n_ctx_seq=n_ctx_seq, seq=seq, has_sink=not use_a)
        if use_a:
            o_lat = _attn_global(q, kh, vh, kc_all, vc_all, layer=l, **lat)
        else:
            o_lat = _attn_window(sink, q, kh, vh, kc_all, vc_all, layer=l, **lat)
        if use_a:
            x = _ffn(o_ctx, o_lat, w_o_b, x, mods, g_ffn, w_ffn_up_b, w_ffn_down_b, layer=l, idx=idx, til=til)
        else:
            x, sorted_buf = _moe(o_ctx, o_lat, w_o_b, x, mods, g_ffn, w_router[idx], w_exp_up_b, w_exp_down_b,
                                 gf if l == depth - 1 else None, sorted_buf, layer=l, idx=idx, til=til,
                                 tf=tf_expert)

    if depth % 2 == 0:
        y_prompt, y_sample = x
    else:
        y_prompt = _final_norm(x, gf, tm=tm, first_tile=0, n_tiles=til.n_ctx_tiles)
        y_sample = _final_norm(x, gf, tm=tm, first_tile=til.n_ctx_tiles, n_tiles=n_lat_tokens // tm)
    y_prompt = y_prompt.reshape(n_ctx_seq, seq, d)
    y_sample = y_sample.reshape(n_lat_seq, lat_seq, d)
    return (y_prompt, y_sample, jnp.stack(new_k, axis=1), jnp.stack(new_v, axis=1))
```

```python
import functools

import jax
import jax.numpy as jnp
from jax import lax
from jax.experimental import pallas as pl
from jax.experimental.pallas import tpu as pltpu

N_HEADS = 16
N_KV_HEADS = 4
GROUP = N_HEADS // N_KV_HEADS
HEAD_DIM = 64
GRID_W = 64
WINDOW = 128
ROPE_THETA = 10000.0
N_MOD = 6
TOP_K = 2
EPS = 1e-6
NEG_INF = -1e30
ATTN_SCALE = HEAD_DIM ** -0.5
LOG2E = 1.4426950408889634
V_WIDTH = 2 * HEAD_DIM

LANES = 128
SUBLANES = 8
MXU_DIM = 256
MOD_ROWS = 16
Q_TILE = 256
GLOBAL_Q_TILE = 1024
WINDOW_Q_TILE = 256
VMEM_LIMIT = 56 * 1024 * 1024

F32 = jnp.float32
BF16 = jnp.bfloat16
NT_DIMS = (((1,), (1,)), ((), ()))


def _params(sem, vmem=VMEM_LIMIT):
    return pltpu.CompilerParams(dimension_semantics=sem, vmem_limit_bytes=vmem)


def _pick_tile(n, candidates):
    for c in candidates:
        if n % c == 0:
            return c
    return n


def _ada_kernel(c_ref, w_ref, b_ref, o_ref):
    c = c_ref[...]
    a = c / (1.0 + jnp.exp(-c))
    o_ref[...] = jnp.dot(a, w_ref[...], preferred_element_type=F32,
                         precision=lax.Precision.HIGHEST) + b_ref[...]


def _ada_mods(cvecs, w_ada, b_ada):
    depth, d, n = w_ada.shape
    tn = _pick_tile(n, (1536, 1024, 512, 256, 128))
    return pl.pallas_call(
        _ada_kernel,
        grid=(depth, n // tn),
        in_specs=[
            pl.BlockSpec((MOD_ROWS, d), lambda l, j: (0, 0)),
            pl.BlockSpec((None, d, tn), lambda l, j: (l, 0, j)),
            pl.BlockSpec((None, 1, tn), lambda l, j: (l, 0, j)),
        ],
        out_specs=pl.BlockSpec((None, MOD_ROWS, tn), lambda l, j: (l, 0, j)),
        out_shape=jax.ShapeDtypeStruct((depth, MOD_ROWS, n), F32),
        compiler_params=_params(("arbitrary", "arbitrary")),
    )(cvecs, w_ada, b_ada.reshape(depth, 1, n))


def _norm_mod(x, g, shift, scale):
    ms = jnp.mean(x * x, axis=-1, keepdims=True)
    y = (x * lax.rsqrt(ms + EPS)) * g
    return y * (1.0 + scale) + shift


class _Tiling:
    def __init__(self, tm, n_ctx_tokens, lat_seq, n_lat_seq, d):
        self.tm = tm
        self.n_ctx_tiles = n_ctx_tokens // tm
        self.tiles_per_seq = lat_seq // tm
        self.ctx_row = n_lat_seq
        self.d = d

    def mod_spec(self, layer, which):
        def index(i, *_):
            row = jnp.where(i < self.n_ctx_tiles, self.ctx_row, (i - self.n_ctx_tiles) // self.tiles_per_seq)
            return (layer, row * N_MOD + which, 0, 0)
        return pl.BlockSpec((None, None, 1, self.d), index)


def _layer_vec_spec(layer, d):
    return pl.BlockSpec((None, 1, d), lambda *_: (layer, 0, 0))


def _qkv_kernel(*refs, use_a, d_model, n_ctx_tiles, split_input):
    if split_input:
        (xc_ref, xl_ref, sh_ref, sc_ref, g_ref, w_ref, qn_ref, kn_ref, cos_ref, sin_ref, ones_ref, _, _,
         q_ref, kf_ref, vf_ref, kh_ref, vh_ref, x_ref, h_ref, y_ref, ss_ref) = refs
    else:
        (x_ref, sh_ref, sc_ref, g_ref, w_ref, qn_ref, kn_ref, cos_ref, sin_ref, ones_ref, _, _,
         q_ref, kf_ref, vf_ref, kh_ref, vh_ref, h_ref, y_ref, ss_ref) = refs
    q_dim = d_model
    kv_dim = N_KV_HEADS * HEAD_DIM
    is_ctx = pl.program_id(0) < n_ctx_tiles
    if split_input:
        @pl.when(is_ctx)
        def _():
            x_ref[...] = xc_ref[...]

        @pl.when(jnp.logical_not(is_ctx))
        def _():
            x_ref[...] = xl_ref[...]
    h_ref[...] = _norm_mod(x_ref[...], g_ref[...], sh_ref[...], sc_ref[...]).astype(BF16)

    cos = cos_ref[...]
    sin = sin_ref[...]
    even = (lax.broadcasted_iota(jnp.int32, cos.shape, 1) % 2) == 0

    for c0 in range(0, q_dim + 2 * kv_dim, MXU_DIM):
        y_ref[:, c0:c0 + MXU_DIM] = jnp.dot(h_ref[...], w_ref[:, c0:c0 + MXU_DIM], preferred_element_type=F32)
    if use_a:
        for c0 in range(0, q_dim + kv_dim, MXU_DIM):
            yc = y_ref[:, c0:c0 + MXU_DIM]
            ss_ref[:, c0:c0 + MXU_DIM] = jnp.dot((yc * yc).astype(BF16), ones_ref[...], preferred_element_type=F32)

    def project(col0, gain):
        xc = y_ref[:, col0:col0 + MXU_DIM]
        if use_a and gain is not None:
            ss = ss_ref[:, col0:col0 + MXU_DIM]
            xc = xc * lax.rsqrt(ss * (1.0 / HEAD_DIM) + EPS) * gain
        return xc

    def rope(xc):
        cols = []
        for l0 in range(0, MXU_DIM, LANES):
            xl = xc[:, l0:l0 + LANES]
            swapped = jnp.where(even, pltpu.roll(xl, LANES - 1, 1), pltpu.roll(xl, 1, 1))
            cols.append(xl * cos + swapped * sin)
        return jnp.concatenate(cols, axis=-1)

    for c0 in range(0, q_dim, MXU_DIM):
        qc = rope(project(c0, qn_ref[:, c0:c0 + MXU_DIM]))
        q_ref[:, c0:c0 + MXU_DIM] = (qc * (ATTN_SCALE * LOG2E)).astype(BF16)
    k = rope(project(q_dim, kn_ref[...]))
    v = project(q_dim + kv_dim, None)
    lane = lax.broadcasted_iota(jnp.int32, cos.shape, 1)
    ones_col = jnp.where(lane == HEAD_DIM, 1.0, 0.0)
    for hh in range(N_KV_HEADS):
        kh_ref[hh] = k[:, hh * HEAD_DIM:(hh + 1) * HEAD_DIM].astype(BF16)
        pair = v[:, (hh // 2) * LANES:(hh // 2 + 1) * LANES]
        if hh % 2:
            pair = pltpu.roll(pair, HEAD_DIM, 1)
        vh_ref[hh] = jnp.where(lane < HEAD_DIM, pair, ones_col).astype(BF16)

    @pl.when(is_ctx)
    def _():
        kf_ref[...] = k.reshape(kf_ref.shape)
        vf_ref[...] = v.reshape(vf_ref.shape)


def _qkv(x, mods, g_all, w_all, qn, kn, cos, sin, ones, k_state, v_state, *, layer, use_a, til):
    split_input = isinstance(x, tuple)
    seqs_per_tile = til.tm // k_state.shape[2]
    state_block = (seqs_per_tile, None) + k_state.shape[2:]
    d = til.d
    t = sum(part.shape[0] for part in x) if split_input else x.shape[0]
    tm = til.tm
    kv_dim = N_KV_HEADS * HEAD_DIM
    n_out = w_all.shape[2]
    n_ctx_tiles = til.n_ctx_tiles
    row = lambda i: (i, 0)
    fixed = lambda i: (0, 0)
    ctx_row = lambda i: (jnp.minimum(i, n_ctx_tiles - 1), 0)
    lat_row = lambda i: (jnp.maximum(i - n_ctx_tiles, 0), 0)
    state_spec = pl.BlockSpec(state_block, lambda i: (jnp.minimum(i, n_ctx_tiles - 1), layer, 0, 0))
    if split_input:
        x_specs = [pl.BlockSpec((tm, d), ctx_row), pl.BlockSpec((tm, d), lat_row)]
        x_args = list(x)
        extra_out_specs = [pl.BlockSpec((tm, d), row)]
        extra_out_shape = [jax.ShapeDtypeStruct((t, d), F32)]
    else:
        x_specs = [pl.BlockSpec((tm, d), row)]
        x_args = [x]
        extra_out_specs = []
        extra_out_shape = []
    return pl.pallas_call(
        functools.partial(_qkv_kernel, use_a=use_a, d_model=d, n_ctx_tiles=n_ctx_tiles, split_input=split_input),
        grid=(t // tm,),
        in_specs=[
            *x_specs,
            til.mod_spec(layer, 0),
            til.mod_spec(layer, 1),
            _layer_vec_spec(layer, d),
            pl.BlockSpec((None, d, n_out), lambda i: (layer, 0, 0)),
            pl.BlockSpec((1, d), fixed),
            pl.BlockSpec((1, kv_dim), fixed),
            pl.BlockSpec((tm, LANES), row),
            pl.BlockSpec((tm, LANES), row),
            pl.BlockSpec((MXU_DIM, MXU_DIM), fixed),
            pl.BlockSpec(memory_space=pl.ANY),
            pl.BlockSpec(memory_space=pl.ANY),
        ],
        out_specs=[
            pl.BlockSpec((tm, d), row),
            state_spec,
            state_spec,
            pl.BlockSpec((N_KV_HEADS, tm, HEAD_DIM), lambda i: (0, i, 0)),
            pl.BlockSpec((N_KV_HEADS, tm, V_WIDTH), lambda i: (0, i, 0)),
            *extra_out_specs,
        ],
        out_shape=[
            jax.ShapeDtypeStruct((t, d), BF16),
            jax.ShapeDtypeStruct(k_state.shape, F32),
            jax.ShapeDtypeStruct(v_state.shape, F32),
            jax.ShapeDtypeStruct((N_KV_HEADS, t, HEAD_DIM), BF16),
            jax.ShapeDtypeStruct((N_KV_HEADS, t, V_WIDTH), BF16),
            *extra_out_shape,
        ],
        scratch_shapes=[pltpu.VMEM((tm, d), BF16), pltpu.VMEM((tm, n_out), F32),
                        pltpu.VMEM((tm, d + kv_dim), F32)],
        input_output_aliases={len(x_args) + 9: 1, len(x_args) + 10: 2},
        compiler_params=_params(("arbitrary",)),
    )(*x_args, mods, mods, g_all, w_all, qn, kn, cos, sin, ones, k_state, v_state)


def _softmax_pv(score_parts, value_parts, sink):
    m = score_parts[0].max(axis=-1, keepdims=True)
    for s in score_parts[1:]:
        m = jnp.maximum(m, s.max(axis=-1, keepdims=True))
    if sink is not None:
        m = jnp.maximum(m, sink)
    acc = None
    for s, v in zip(score_parts, value_parts):
        pv = jnp.dot(jnp.exp2(s - m).astype(BF16), v, preferred_element_type=F32)
        acc = pv if acc is None else acc + pv
    denom = acc[:, HEAD_DIM:HEAD_DIM + 1]
    if sink is not None:
        denom = denom + jnp.exp2(sink - m)
    return acc[:, :HEAD_DIM] / denom


def _attn_ctx_kernel(sink_ref, q_ref, k_ref, v_ref, o_ref, s_ref, p_ref, m_ref, *, has_sink):
    def cols(head):
        return slice(head * HEAD_DIM, (head + 1) * HEAD_DIM)

    def scores(head):
        s_ref[head] = lax.dot_general(q_ref[:, cols(head)], k_ref[head // GROUP], NT_DIMS,
                                      preferred_element_type=F32)

    def numerators(head):
        s = s_ref[head]
        m = s.max(axis=-1, keepdims=True)
        if has_sink:
            m = jnp.maximum(m, sink_ref[head])
            m_ref[head] = jnp.broadcast_to(jnp.exp2(sink_ref[head] - m), m_ref.shape[1:])
        p_ref[head] = jnp.exp2(s - m).astype(BF16)

    def outputs(head):
        acc = jnp.dot(p_ref[head], v_ref[head // GROUP], preferred_element_type=F32)
        denom = acc[:, HEAD_DIM:HEAD_DIM + 1]
        if has_sink:
            denom = denom + m_ref[head][:, :1]
        o_ref[:, cols(head)] = (acc[:, :HEAD_DIM] / denom).astype(o_ref.dtype)

    def fused_outputs(head):
        sink = sink_ref[head] if has_sink else None
        o_ref[:, cols(head)] = _softmax_pv([s_ref[head]], [v_ref[head // GROUP]], sink).astype(o_ref.dtype)

    if has_sink:
        _phased(N_HEADS, (scores, fused_outputs))
    else:
        _phased(N_HEADS, (scores, numerators, outputs))


def _attn_ctx(sink, q, kh, vh, *, n_ctx_seq, seq, has_sink):
    t, d = q.shape
    return pl.pallas_call(
        functools.partial(_attn_ctx_kernel, has_sink=has_sink),
        grid=(n_ctx_seq,),
        in_specs=[
            pl.BlockSpec(memory_space=pltpu.SMEM),
            pl.BlockSpec((seq, d), lambda b: (b, 0)),
            pl.BlockSpec((N_KV_HEADS, seq, HEAD_DIM), lambda b: (0, b, 0)),
            pl.BlockSpec((N_KV_HEADS, seq, V_WIDTH), lambda b: (0, b, 0)),
        ],
        out_specs=pl.BlockSpec((seq, d), lambda b: (b, 0)),
        out_shape=jax.ShapeDtypeStruct((n_ctx_seq * seq, d), BF16),
        scratch_shapes=[pltpu.VMEM((N_HEADS, seq, seq), F32), pltpu.VMEM((N_HEADS, seq, seq), BF16),
                        pltpu.VMEM((N_HEADS, seq, LANES), F32)],
        compiler_params=_params(("arbitrary",)),
    )(sink, q, kh, vh)


def _phased(n, stages):
    for stage in stages:
        for j in range(n):
            stage(j)


def _attn_global_kernel(q_ref, k_ref, v_ref, kc_ref, vc_ref, o_ref, sl_ref, sc_ref):
    def cols(j):
        return slice(j * HEAD_DIM, (j + 1) * HEAD_DIM)

    def scores(j):
        qj = q_ref[:, cols(j)]
        sl_ref[j] = lax.dot_general(qj, k_ref[...], NT_DIMS, preferred_element_type=F32)
        sc_ref[j] = lax.dot_general(qj, kc_ref[...], NT_DIMS, preferred_element_type=F32)

    def outputs(j):
        o = _softmax_pv([sl_ref[j], sc_ref[j]], [v_ref[...], vc_ref[...]], None)
        o_ref[:, cols(j)] = o.astype(o_ref.dtype)

    _phased(GROUP, (scores, outputs))


def _cache_spec(layer, past, width):
    return pl.BlockSpec((None, None, past, width), lambda b, g, i: (layer, g, b, 0))


def _attn_global(q, kh, vh, kc_all, vc_all, *, layer, n_ctx_tokens, n_lat_seq, lat_seq, past):
    t, d = q.shape
    gw = GROUP * HEAD_DIM
    tq = GLOBAL_Q_TILE if (lat_seq % GLOBAL_Q_TILE == 0 and n_ctx_tokens % GLOBAL_Q_TILE == 0) else Q_TILE
    q_blocks = lat_seq // tq
    q0 = n_ctx_tokens // tq
    s0 = n_ctx_tokens // lat_seq
    return pl.pallas_call(
        _attn_global_kernel,
        grid=(n_lat_seq, N_KV_HEADS, q_blocks),
        in_specs=[
            pl.BlockSpec((tq, gw), lambda b, g, i: (q0 + b * q_blocks + i, g)),
            pl.BlockSpec((None, lat_seq, HEAD_DIM), lambda b, g, i: (g, s0 + b, 0)),
            pl.BlockSpec((None, lat_seq, V_WIDTH), lambda b, g, i: (g, s0 + b, 0)),
            _cache_spec(layer, past, HEAD_DIM),
            _cache_spec(layer, past, V_WIDTH),
        ],
        out_specs=pl.BlockSpec((tq, gw), lambda b, g, i: (b * q_blocks + i, g)),
        out_shape=jax.ShapeDtypeStruct((n_lat_seq * lat_seq, d), BF16),
        scratch_shapes=[pltpu.VMEM((GROUP, tq, lat_seq), F32), pltpu.VMEM((GROUP, tq, past), F32)],
        compiler_params=_params(("arbitrary", "arbitrary", "arbitrary")),
    )(q, kh, vh, kc_all, vc_all)


def _attn_window_kernel(sink_ref, q_ref, kp_ref, kcur_ref, kn_ref, vp_ref, vcur_ref, vn_ref,
                        kc_ref, vc_ref, o_ref, sl_ref, sc_ref, *, lat_seq, tq):
    grp = pl.program_id(1)
    start = pl.program_id(2) * tq
    span = tq + 2 * WINDOW
    kw = jnp.concatenate([kp_ref[...], kcur_ref[...], kn_ref[...]], axis=0)
    vw = jnp.concatenate([vp_ref[...], vcur_ref[...], vn_ref[...]], axis=0)
    qpos = start + lax.broadcasted_iota(jnp.int32, (tq, span), 0)
    kpos = start - WINDOW + lax.broadcasted_iota(jnp.int32, (tq, span), 1)
    valid = (kpos >= 0) & (kpos < lat_seq) & (jnp.abs(qpos - kpos) <= WINDOW)

    def cols(j):
        return slice(j * HEAD_DIM, (j + 1) * HEAD_DIM)

    def scores(j):
        qj = q_ref[:, cols(j)]
        s_loc = lax.dot_general(qj, kw, NT_DIMS, preferred_element_type=F32)
        sl_ref[j] = jnp.where(valid, s_loc, NEG_INF)
        sc_ref[j] = lax.dot_general(qj, kc_ref[...], NT_DIMS, preferred_element_type=F32)

    def outputs(j):
        o = _softmax_pv([sl_ref[j], sc_ref[j]], [vw, vc_ref[...]], sink_ref[grp * GROUP + j])
        o_ref[:, cols(j)] = o.astype(o_ref.dtype)

    _phased(GROUP, (scores, outputs))


def _attn_window(sink, q, kh, vh, kc_all, vc_all, *, layer, n_ctx_tokens, n_lat_seq, lat_seq, past):
    t, d = q.shape
    gw = GROUP * HEAD_DIM
    tq = WINDOW_Q_TILE if (lat_seq % WINDOW_Q_TILE == 0 and n_ctx_tokens % WINDOW_Q_TILE == 0) else Q_TILE
    q_blocks = lat_seq // tq
    q0 = n_ctx_tokens // tq
    span = tq + 2 * WINDOW
    half = tq // WINDOW
    w0 = n_ctx_tokens // WINDOW
    w_blocks = lat_seq // WINDOW
    prev = lambda b, g, i: (g, w0 + b * w_blocks + jnp.maximum(i * half - 1, 0), 0)
    cur = lambda b, g, i: (g, q0 + b * q_blocks + i, 0)
    nxt = lambda b, g, i: (g, w0 + b * w_blocks + jnp.minimum((i + 1) * half, w_blocks - 1), 0)
    def band_specs(width):
        return [
            pl.BlockSpec((None, WINDOW, width), prev),
            pl.BlockSpec((None, tq, width), cur),
            pl.BlockSpec((None, WINDOW, width), nxt),
        ]
    return pl.pallas_call(
        functools.partial(_attn_window_kernel, lat_seq=lat_seq, tq=tq),
        grid=(n_lat_seq, N_KV_HEADS, q_blocks),
        in_specs=[
            pl.BlockSpec(memory_space=pltpu.SMEM),
            pl.BlockSpec((tq, gw), lambda b, g, i: (q0 + b * q_blocks + i, g)),
            *band_specs(HEAD_DIM), *band_specs(V_WIDTH),
            _cache_spec(layer, past, HEAD_DIM),
            _cache_spec(layer, past, V_WIDTH),
        ],
        out_specs=pl.BlockSpec((tq, gw), lambda b, g, i: (b * q_blocks + i, g)),
        out_shape=jax.ShapeDtypeStruct((n_lat_seq * lat_seq, d), BF16),
        scratch_shapes=[pltpu.VMEM((GROUP, tq, span), F32), pltpu.VMEM((GROUP, tq, past), F32)],
        compiler_params=_params(("arbitrary", "arbitrary", "arbitrary")),
    )(sink, q, kh, kh, kh, vh, vh, vh, kc_all, vc_all)


def _attn_residual(oc_ref, ol_ref, wo_ref, x_ref, gate_ref, dst_ref, n_ctx_tiles):
    def project(o_ref):
        y = jnp.dot(o_ref[...], wo_ref[...], preferred_element_type=F32)
        dst_ref[...] = x_ref[...] + gate_ref[...] * y

    @pl.when(pl.program_id(0) < n_ctx_tiles)
    def _():
        project(oc_ref)

    @pl.when(pl.program_id(0) >= n_ctx_tiles)
    def _():
        project(ol_ref)


def _attn_residual_specs(layer, til):
    tm, d, n_ctx_tiles = til.tm, til.d, til.n_ctx_tiles
    return [
        pl.BlockSpec((tm, d), lambda i, *_: (jnp.minimum(i, n_ctx_tiles - 1), 0)),
        pl.BlockSpec((tm, d), lambda i, *_: (jnp.maximum(i - n_ctx_tiles, 0), 0)),
        pl.BlockSpec((None, d, d), lambda i, *_: (layer, 0, 0), pipeline_mode=pl.Buffered(1)),
        pl.BlockSpec((tm, d), lambda i, *_: (i, 0)),
        til.mod_spec(layer, 2),
    ]


def _swiglu_act(h, wg, wu):
    g = jnp.dot(h, wg, preferred_element_type=F32)
    u = jnp.dot(h, wu, preferred_element_type=F32)
    return (g / (1.0 + jnp.exp(-g))) * u


def _ffn_kernel(oc_ref, ol_ref, wo_ref, x_ref, g1_ref, sh_ref, sc_ref, gate_ref, g_ref, wg_ref, wu_ref, wd_ref,
                out_ref, x1_ref, *, n_ctx_tiles):
    _attn_residual(oc_ref, ol_ref, wo_ref, x_ref, g1_ref, x1_ref, n_ctx_tiles)
    x = x1_ref[...]
    h = _norm_mod(x, g_ref[...], sh_ref[...], sc_ref[...]).astype(BF16)
    a = _swiglu_act(h, wg_ref[...], wu_ref[...])
    f = jnp.dot(a.astype(BF16), wd_ref[...], preferred_element_type=F32)
    out_ref[...] = x + gate_ref[...] * f


def _ffn(o_ctx, o_lat, w_o_all, x, mods, g_all, w_up_all, w_down_all, *, layer, idx, til):
    t, d = x.shape
    tm = til.tm
    d_ff = w_down_all.shape[1]
    row = lambda i: (i, 0)
    resident = pl.Buffered(1)
    return pl.pallas_call(
        functools.partial(_ffn_kernel, n_ctx_tiles=til.n_ctx_tiles),
        grid=(t // tm,),
        in_specs=[
            *_attn_residual_specs(layer, til),
            til.mod_spec(layer, 3),
            til.mod_spec(layer, 4),
            til.mod_spec(layer, 5),
            _layer_vec_spec(layer, d),
            pl.BlockSpec((None, d, d_ff), lambda i: (idx, 0, 0), pipeline_mode=resident),
            pl.BlockSpec((None, d, d_ff), lambda i: (idx, 0, 1), pipeline_mode=resident),
            pl.BlockSpec((None, d_ff, d), lambda i: (idx, 0, 0), pipeline_mode=resident),
        ],
        out_specs=pl.BlockSpec((tm, d), row),
        out_shape=jax.ShapeDtypeStruct((t, d), F32),
        scratch_shapes=[pltpu.VMEM((tm, d), F32)],
        compiler_params=_params(("arbitrary",)),
    )(o_ctx, o_lat, w_o_all, x, mods, mods, mods, mods, g_all, w_up_all, w_up_all, w_down_all)


META_IDX, META_GATE, META_RANK = 0, 2, 4


def _router_kernel(oc_ref, ol_ref, wo_ref, x_ref, g1_ref, sh_ref, sc_ref, g_ref, wr_ref, tri_ref,
                   x1_ref, meta_ref, meta_t_ref, cnt_ref, run_ref, *, n_experts, n_ctx_tiles):
    @pl.when(pl.program_id(0) == 0)
    def _():
        run_ref[...] = jnp.zeros_like(run_ref)

    _attn_residual(oc_ref, ol_ref, wo_ref, x_ref, g1_ref, x1_ref, n_ctx_tiles)
    h = _norm_mod(x1_ref[...], g_ref[...], sh_ref[...], sc_ref[...])
    logits = jnp.dot(h, wr_ref[...], preferred_element_type=F32, precision=lax.Precision.HIGHEST)
    lane = lax.broadcasted_iota(jnp.int32, logits.shape, 1).astype(F32)
    logits = jnp.where(lane < n_experts, logits, -jnp.inf)
    top1 = logits.max(axis=-1, keepdims=True)
    idx1 = jnp.where(logits == top1, lane, float(LANES)).min(axis=-1, keepdims=True)
    rest = jnp.where(lane == idx1, -jnp.inf, logits)
    top2 = rest.max(axis=-1, keepdims=True)
    idx2 = jnp.where(rest == top2, lane, float(LANES)).min(axis=-1, keepdims=True)
    e = jnp.exp(top2 - top1)
    g1 = 1.0 / (1.0 + e)
    g2 = e / (1.0 + e)

    sel1 = lane == idx1
    sel2 = lane == idx2
    sel = jnp.where(sel1, 1.0, 0.0) + jnp.where(sel2, 1.0, 0.0)
    before = jnp.dot(tri_ref[...], sel.astype(BF16), preferred_element_type=F32) + run_ref[...]
    r1 = jnp.where(sel1, before, 0.0).sum(axis=-1, keepdims=True)
    r2 = jnp.where(sel2, before, 0.0).sum(axis=-1, keepdims=True)
    run_ref[...] += sel.sum(axis=0, keepdims=True)
    cnt_ref[...] = run_ref[...]

    meta = jnp.zeros_like(logits)
    for off, (a, b) in ((META_IDX, (idx1, idx2)), (META_GATE, (g1, g2)), (META_RANK, (r1, r2))):
        meta = jnp.where(lane == off, a, meta)
        meta = jnp.where(lane == off + 1, b, meta)
    meta_ref[...] = meta
    meta_t_ref[...] = meta.T[:SUBLANES]


def _router(o_ctx, o_lat, w_o_all, x, mods, g_all, wr, tri, *, layer, n_experts, til):
    t, d = x.shape
    tm = til.tm
    row = lambda i: (i, 0)
    fixed = lambda i: (0, 0)
    return pl.pallas_call(
        functools.partial(_router_kernel, n_experts=n_experts, n_ctx_tiles=til.n_ctx_tiles),
        grid=(t // tm,),
        in_specs=[
            *_attn_residual_specs(layer, til),
            til.mod_spec(layer, 3),
            til.mod_spec(layer, 4),
            _layer_vec_spec(layer, d),
            pl.BlockSpec((d, LANES), fixed),
            pl.BlockSpec((tm, tm), fixed),
        ],
        out_specs=[pl.BlockSpec((tm, d), row), pl.BlockSpec((tm, LANES), row),
                   pl.BlockSpec((SUBLANES, tm), lambda i: (0, i)), pl.BlockSpec((1, LANES), fixed)],
        out_shape=[jax.ShapeDtypeStruct((t, d), F32), jax.ShapeDtypeStruct((t, LANES), F32),
                   jax.ShapeDtypeStruct((SUBLANES, t), F32), jax.ShapeDtypeStruct((1, LANES), F32)],
        scratch_shapes=[pltpu.VMEM((1, LANES), F32)],
        compiler_params=_params(("arbitrary",)),
    )(o_ctx, o_lat, w_o_all, x, mods, mods, mods, g_all, wr, tri)


def _row_copy(src_ref, src_row, dst_ref, dst_row, sem):
    return pltpu.make_async_copy(src_ref.at[pl.ds(src_row, 1)], dst_ref.at[pl.ds(dst_row, 1)], sem)


def _dispatch_kernel(pos_ref, x_ref, sh_ref, sc_ref, g_ref, hs_init_ref, hs_ref, h_ref, sem, *, tm, n_tokens):
    del hs_init_ref
    i = pl.program_id(0)
    slot = i % 2

    def wait_slot(s):
        for _ in range(TOP_K * tm):
            _row_copy(h_ref.at[s], 0, hs_ref, 0, sem.at[s]).wait()

    @pl.when(i >= 2)
    def _():
        wait_slot(slot)

    h_ref[slot] = _norm_mod(x_ref[...], g_ref[...], sh_ref[...], sc_ref[...])

    def send(r, carry):
        for k in range(TOP_K):
            dst = pos_ref[k * n_tokens + i * tm + r]
            _row_copy(h_ref.at[slot], r, hs_ref, dst, sem.at[slot]).start(priority=k)
        return carry
    lax.fori_loop(0, tm, send, 0, unroll=8)

    @pl.when(i == pl.num_programs(0) - 1)
    def _():
        wait_slot(slot)

        @pl.when(i >= 1)
        def _():
            wait_slot(1 - slot)


def _dispatch(pos, x, mods, g_all, hs_init, *, layer, til):
    t, d = x.shape
    tm = til.tm
    return pl.pallas_call(
        functools.partial(_dispatch_kernel, tm=tm, n_tokens=t),
        grid_spec=pltpu.PrefetchScalarGridSpec(
            num_scalar_prefetch=1,
            grid=(t // tm,),
            in_specs=[
                pl.BlockSpec((tm, d), lambda i, pos: (i, 0)),
                til.mod_spec(layer, 3),
                til.mod_spec(layer, 4),
                _layer_vec_spec(layer, d),
                pl.BlockSpec(memory_space=pl.ANY),
            ],
            out_specs=pl.BlockSpec(memory_space=pl.ANY),
            scratch_shapes=[pltpu.VMEM((2, tm, d), F32), pltpu.SemaphoreType.DMA((2,))],
        ),
        out_shape=jax.ShapeDtypeStruct(hs_init.shape, F32),
        input_output_aliases={5: 0},
        compiler_params=_params(("arbitrary",)),
    )(pos, x, mods, mods, g_all, hs_init)


def _expert_kernel(te_ref, tv_ref, hs_ref, wg_ref, wu_ref, wd_ref, ys_ref, h_ref, acc_ref):
    del te_ref
    j = pl.program_id(1)

    @pl.when(tv_ref[pl.program_id(0)] != 0)
    def _():
        @pl.when(j == 0)
        def _():
            h_ref[...] = hs_ref[...].astype(BF16)
            acc_ref[...] = jnp.zeros_like(acc_ref)

        a = _swiglu_act(h_ref[...], wg_ref[...], wu_ref[...])
        acc_ref[...] += jnp.dot(a.astype(BF16), wd_ref[...], preferred_element_type=F32)

        @pl.when(j == pl.num_programs(1) - 1)
        def _():
            ys_ref[...] = acc_ref[...]

    @pl.when((tv_ref[pl.program_id(0)] == 0) & (j == 0))
    def _():
        ys_ref[...] = jnp.zeros_like(ys_ref)


def _experts(tile_expert, tile_valid, hs, w_up_all, w_down_all, *, idx, tm, tf):
    p, d = hs.shape
    d_ff = w_down_all.shape[2]
    nf = d_ff // tf
    return pl.pallas_call(
        _expert_kernel,
        grid_spec=pltpu.PrefetchScalarGridSpec(
            num_scalar_prefetch=2,
            grid=(p // tm, nf),
            in_specs=[
                pl.BlockSpec((tm, d), lambda r, j, te, tv: (r, 0)),
                pl.BlockSpec((None, None, d, tf), lambda r, j, te, tv: (idx, te[r], 0, j)),
                pl.BlockSpec((None, None, d, tf), lambda r, j, te, tv: (idx, te[r], 0, nf + j)),
                pl.BlockSpec((None, None, tf, d), lambda r, j, te, tv: (idx, te[r], j, 0)),
            ],
            out_specs=pl.BlockSpec((tm, d), lambda r, j, te, tv: (r, 0)),
            scratch_shapes=[pltpu.VMEM((tm, d), BF16), pltpu.VMEM((tm, d), F32)],
        ),
        out_shape=jax.ShapeDtypeStruct((p, d), F32),
        compiler_params=_params(("arbitrary", "arbitrary")),
    )(tile_expert, tile_valid, hs, w_up_all, w_up_all, w_down_all)


def _combine_kernel(pos_ref, ys_ref, meta_ref, x_ref, gate_ref, *rest, tm, n_tokens, n_ctx_tiles, final):
    if final:
        gf_ref, yc_ref, yl_ref, y_ref, sem = rest
    else:
        out_ref, y_ref, sem = rest
    i = pl.program_id(0)
    slot = i % 2

    def fetch(step, s):
        def body(r, carry):
            for k in range(TOP_K):
                src = pos_ref[k * n_tokens + step * tm + r]
                _row_copy(ys_ref, src, y_ref.at[s, k], r, sem.at[s]).start(priority=k)
            return carry
        lax.fori_loop(0, tm, body, 0, unroll=8)

    @pl.when(i == 0)
    def _():
        fetch(0, 0)

    @pl.when(i + 1 < pl.num_programs(0))
    def _():
        fetch(i + 1, 1 - slot)

    for _ in range(tm):
        for k in range(TOP_K):
            _row_copy(ys_ref, 0, y_ref.at[slot, k], 0, sem.at[slot]).wait()

    meta = meta_ref[...]
    f = meta[:, META_GATE:META_GATE + 1] * y_ref[slot, 0] + meta[:, META_GATE + 1:META_GATE + 2] * y_ref[slot, 1]
    x_new = x_ref[...] + gate_ref[...] * f
    if not final:
        out_ref[...] = x_new
        return

    ms = jnp.mean(x_new * x_new, axis=-1, keepdims=True)
    y = (x_new * lax.rsqrt(ms + EPS)) * gf_ref[...]

    @pl.when(i < n_ctx_tiles)
    def _():
        yc_ref[...] = y

    @pl.when(i >= n_ctx_tiles)
    def _():
        yl_ref[...] = y


def _combine(pos, ys, meta, x, mods, final_gain, *, layer, til):
    t, d = x.shape
    tm = til.tm
    n_ctx_tiles = til.n_ctx_tiles
    final = final_gain is not None
    row = lambda i, pos: (i, 0)
    in_specs = [
        pl.BlockSpec(memory_space=pl.ANY),
        pl.BlockSpec((tm, LANES), row),
        pl.BlockSpec((tm, d), row),
        til.mod_spec(layer, 5),
    ]
    args = [pos, ys, meta, x, mods]
    if final:
        in_specs.append(pl.BlockSpec((1, d), lambda i, pos: (0, 0)))
        args.append(final_gain)
        out_specs = [pl.BlockSpec((tm, d), lambda i, pos: (jnp.minimum(i, n_ctx_tiles - 1), 0)),
                     pl.BlockSpec((tm, d), lambda i, pos: (jnp.maximum(i - n_ctx_tiles, 0), 0))]
        out_shape = [jax.ShapeDtypeStruct((n_ctx_tiles * tm, d), F32),
                     jax.ShapeDtypeStruct((t - n_ctx_tiles * tm, d), F32)]
    else:
        out_specs = pl.BlockSpec((tm, d), row)
        out_shape = jax.ShapeDtypeStruct((t, d), F32)
    return pl.pallas_call(
        functools.partial(_combine_kernel, tm=tm, n_tokens=t, n_ctx_tiles=n_ctx_tiles, final=final),
        grid_spec=pltpu.PrefetchScalarGridSpec(
            num_scalar_prefetch=1,
            grid=(t // tm,),
            in_specs=in_specs,
            out_specs=out_specs,
            scratch_shapes=[pltpu.VMEM((2, TOP_K, tm, d), F32), pltpu.SemaphoreType.DMA((2,))],
        ),
        out_shape=out_shape,
        compiler_params=_params(("arbitrary",)),
    )(*args)


def _moe(o_ctx, o_lat, w_o_all, x, mods, g_all, w_router, w_up_all, w_down_all, final_gain, sorted_buf,
         *, layer, idx, til, tf):
    t, d = x.shape
    n_experts = w_router.shape[-1]
    tm = til.tm
    wr = jnp.zeros((d, LANES), F32).at[:, :n_experts].set(w_router)
    tri = (jnp.arange(tm)[:, None] > jnp.arange(tm)[None, :]).astype(BF16)
    x, meta, meta_t, counts = _router(o_ctx, o_lat, w_o_all, x, mods, g_all, wr, tri, layer=layer,
                                      n_experts=n_experts, til=til)

    counts = counts[0, :n_experts].astype(jnp.int32)
    padded = ((counts + tm - 1) // tm) * tm
    ends = jnp.cumsum(padded)
    starts = ends - padded
    choice = meta_t[META_IDX:META_IDX + TOP_K].astype(jnp.int32)
    rank = meta_t[META_RANK:META_RANK + TOP_K].astype(jnp.int32)
    group_start = sum(jnp.where(choice == e, starts[e], 0) for e in range(n_experts))
    pos = (group_start + rank).reshape(TOP_K * t)
    n_tiles = (t * TOP_K) // tm + n_experts
    tile_start = jnp.arange(n_tiles, dtype=jnp.int32) * tm
    tile_expert = jnp.minimum((tile_start[:, None] >= ends[None, :]).sum(axis=-1), n_experts - 1).astype(jnp.int32)
    tile_valid = (tile_start < ends[-1]).astype(jnp.int32)

    if sorted_buf is None:
        sorted_buf = jnp.zeros((n_tiles * tm, d), F32)
    hs = _dispatch(pos, x, mods, g_all, sorted_buf, layer=layer, til=til)
    ys = _experts(tile_expert, tile_valid, hs, w_up_all, w_down_all, idx=idx, tm=tm, tf=tf)
    return _combine(pos, ys, meta, x, mods, final_gain, layer=layer, til=til), hs


def _final_norm_kernel(x_ref, g_ref, o_ref):
    x = x_ref[...]
    ms = jnp.mean(x * x, axis=-1, keepdims=True)
    o_ref[...] = (x * lax.rsqrt(ms + EPS)) * g_ref[...]


def _final_norm(x, g, *, tm, first_tile, n_tiles):
    d = x.shape[1]
    return pl.pallas_call(
        _final_norm_kernel,
        grid=(n_tiles,),
        in_specs=[pl.BlockSpec((tm, d), lambda i: (first_tile + i, 0)), pl.BlockSpec((1, d), lambda i: (0, 0))],
        out_specs=pl.BlockSpec((tm, d), lambda i: (i, 0)),
        out_shape=jax.ShapeDtypeStruct((n_tiles * tm, d), F32),
        compiler_params=_params(("arbitrary",)),
    )(x, g)


def _rope_tables(n_ctx_tokens, n_lat_seq, lat_seq):
    pos = jnp.arange(lat_seq)
    rows = (pos // GRID_W).astype(F32)
    cols = (pos % GRID_W).astype(F32)
    pairs_per_axis = HEAD_DIM // 4
    inv = ROPE_THETA ** (-jnp.arange(pairs_per_axis, dtype=F32) / pairs_per_axis)
    ang = jnp.concatenate([rows[:, None] * inv, cols[:, None] * inv], axis=-1)
    cos = jnp.repeat(jnp.cos(ang), 2, axis=-1)
    sin = jnp.repeat(jnp.sin(ang), 2, axis=-1) * jnp.tile(jnp.array([-1.0, 1.0], F32), HEAD_DIM // 2)
    reps = LANES // HEAD_DIM
    cos = jnp.tile(cos, (n_lat_seq, reps))
    sin = jnp.tile(sin, (n_lat_seq, reps))
    cos = jnp.concatenate([jnp.ones((n_ctx_tokens, LANES), F32), cos], axis=0)
    sin = jnp.concatenate([jnp.zeros((n_ctx_tokens, LANES), F32), sin], axis=0)
    return cos, sin


def kernel(x_prompt, x_sample, cache_k, cache_v, c, c_ctx, w_ada, b_ada, norm_mix, norm_ffn, norm_final,
           w_qkv, w_o, q_norm, k_norm, sinks, w_ffn_up, w_ffn_down, w_router, w_exp_up, w_exp_down):
    n_ctx_seq, seq, d = x_prompt.shape
    n_lat_seq, lat_seq, _ = x_sample.shape
    depth = w_ada.shape[0]
    past = cache_k.shape[2]
    n_experts = w_router.shape[-1]
    kv_dim = N_KV_HEADS * HEAD_DIM
    n_ctx_tokens = n_ctx_seq * seq
    n_lat_tokens = n_lat_seq * lat_seq
    assert d == N_HEADS * HEAD_DIM and seq == Q_TILE and lat_seq % Q_TILE == 0 and kv_dim == MXU_DIM
    assert n_ctx_tokens % lat_seq == 0 and n_lat_seq < MOD_ROWS and n_experts <= LANES

    tm = 2 * Q_TILE if (lat_seq % (2 * Q_TILE) == 0 and n_ctx_tokens % (2 * Q_TILE) == 0) else Q_TILE
    til = _Tiling(tm, n_ctx_tokens, lat_seq, n_lat_seq, d)

    x = (x_prompt.reshape(n_ctx_tokens, d), x_sample.reshape(n_lat_tokens, d))

    cvecs = jnp.zeros((MOD_ROWS, d), F32).at[:n_lat_seq].set(c).at[n_lat_seq].set(c_ctx)
    mods = _ada_mods(cvecs, w_ada, b_ada).reshape(depth, MOD_ROWS * N_MOD, 1, d)

    cos, sin = _rope_tables(n_ctx_tokens, n_lat_seq, lat_seq)
    head_id = jnp.arange(MXU_DIM) // HEAD_DIM
    ones = (head_id[:, None] == head_id[None, :]).astype(BF16)

    kc_all = cache_k.transpose(1, 3, 0, 2, 4).reshape(depth, N_KV_HEADS, n_lat_seq * past, HEAD_DIM).astype(BF16)
    vc_all = cache_v.transpose(1, 3, 0, 2, 4).reshape(depth, N_KV_HEADS, n_lat_seq * past, HEAD_DIM).astype(BF16)
    ones_col = (jnp.arange(V_WIDTH - HEAD_DIM) == 0).astype(BF16)
    vc_all = jnp.concatenate([vc_all, jnp.broadcast_to(ones_col, vc_all.shape[:-1] + ones_col.shape)], axis=-1)

    w_qkv_b = w_qkv.astype(BF16)
    w_o_b = w_o.astype(BF16)
    w_ffn_up_b = w_ffn_up.astype(BF16)
    w_ffn_down_b = w_ffn_down.astype(BF16)
    w_exp_up_b = w_exp_up.astype(BF16)
    w_exp_down_b = w_exp_down.astype(BF16)
    g_mix = norm_mix.reshape(depth, 1, d)
    g_ffn = norm_ffn.reshape(depth, 1, d)
    gf = norm_final.reshape(1, d)

    tf_expert = _pick_tile(w_exp_down.shape[2], (1792, 1024, 512, 256, 128))
    lat = dict(n_ctx_tokens=n_ctx_tokens, n_lat_seq=n_lat_seq, lat_seq=lat_seq, past=past)

    k_state = jnp.zeros((n_ctx_seq, depth, seq, kv_dim), F32)
    v_state = jnp.zeros((n_ctx_seq, depth, seq, kv_dim), F32)
    sorted_buf = None
    for l in range(depth):
        idx = l // 2
        use_a = (l % 2 == 0)
        if use_a:
            qn = jnp.tile(q_norm[idx], d // HEAD_DIM).reshape(1, d)
            kn = jnp.tile(k_norm[idx], N_KV_HEADS).reshape(1, kv_dim)
            sink = jnp.zeros((N_HEADS,), F32)
        else:
            qn = jnp.ones((1, d), F32)
            kn = jnp.ones((1, kv_dim), F32)
            sink = sinks[idx].astype(F32) * LOG2E

        q, k_state, v_state, kh, vh, *joined = _qkv(x, mods, g_mix, w_qkv_b, qn, kn, cos, sin, ones, k_state,
                                                    v_state, layer=l, use_a=use_a, til=til)
        if joined:
            x = joined[0]

        o_ctx = _attn_ctx(sink, q, kh, vh, n_ctx_seq=n_ctx_seq, seq=seq, has_sink=not use_a)
        if use_a:
            o_lat = _attn_global(q, kh, vh, kc_all, vc_all, layer=l, **lat)
        else:
            o_lat = _attn_window(sink, q, kh, vh, kc_all, vc_all, layer=l, **lat)
        if use_a:
            x = _ffn(o_ctx, o_lat, w_o_b, x, mods, g_ffn, w_ffn_up_b, w_ffn_down_b, layer=l, idx=idx, til=til)
        else:
            x, sorted_buf = _moe(o_ctx, o_lat, w_o_b, x, mods, g_ffn, w_router[idx], w_exp_up_b, w_exp_down_b,
                                 gf if l == depth - 1 else None, sorted_buf, layer=l, idx=idx, til=til,
                                 tf=tf_expert)

    if depth % 2 == 0:
        y_prompt, y_sample = x
    else:
        y_prompt = _final_norm(x, gf, tm=tm, first_tile=0, n_tiles=til.n_ctx_tiles)
        y_sample = _final_norm(x, gf, tm=tm, first_tile=til.n_ctx_tiles, n_tiles=n_lat_tokens // tm)
    y_prompt = y_prompt.reshape(n_ctx_seq, seq, d)
    y_sample = y_sample.reshape(n_lat_seq, lat_seq, d)
    state_shape = (n_ctx_seq, depth, seq, N_KV_HEADS, HEAD_DIM)
    return (y_prompt, y_sample, k_state.reshape(state_shape), v_state.reshape(state_shape))
```

```python
import functools

import jax
import jax.numpy as jnp
from jax import lax
from jax.experimental import pallas as pl
from jax.experimental.pallas import tpu as pltpu

N_HEADS = 16
N_KV_HEADS = 4
GROUP = N_HEADS // N_KV_HEADS
HEAD_DIM = 64
GRID_W = 64
WINDOW = 128
ROPE_THETA = 10000.0
N_MOD = 6
TOP_K = 2
EPS = 1e-6
NEG_INF = -1e30
ATTN_SCALE = HEAD_DIM ** -0.5
LOG2E = 1.4426950408889634
V_WIDTH = 2 * HEAD_DIM

LANES = 128
SUBLANES = 8
MXU_DIM = 256
MOD_ROWS = 16
Q_TILE = 256
GLOBAL_Q_TILE = 1024
WINDOW_Q_TILE = 256
VMEM_LIMIT = 56 * 1024 * 1024

F32 = jnp.float32
BF16 = jnp.bfloat16
NT_DIMS = (((1,), (1,)), ((), ()))


def _params(sem, vmem=VMEM_LIMIT):
    return pltpu.CompilerParams(dimension_semantics=sem, vmem_limit_bytes=vmem)


def _pick_tile(n, candidates):
    for c in candidates:
        if n % c == 0:
            return c
    return n


def _ada_kernel(c_ref, w_ref, b_ref, o_ref):
    c = c_ref[...]
    a = c / (1.0 + jnp.exp(-c))
    o_ref[...] = jnp.dot(a, w_ref[...], preferred_element_type=F32,
                         precision=lax.Precision.HIGHEST) + b_ref[...]


def _ada_mods(cvecs, w_ada, b_ada):
    depth, d, n = w_ada.shape
    tn = _pick_tile(n, (1536, 1024, 512, 256, 128))
    return pl.pallas_call(
        _ada_kernel,
        grid=(depth, n // tn),
        in_specs=[
            pl.BlockSpec((MOD_ROWS, d), lambda l, j: (0, 0)),
            pl.BlockSpec((None, d, tn), lambda l, j: (l, 0, j)),
            pl.BlockSpec((None, 1, tn), lambda l, j: (l, 0, j)),
        ],
        out_specs=pl.BlockSpec((None, MOD_ROWS, tn), lambda l, j: (l, 0, j)),
        out_shape=jax.ShapeDtypeStruct((depth, MOD_ROWS, n), F32),
        compiler_params=_params(("arbitrary", "arbitrary")),
    )(cvecs, w_ada, b_ada.reshape(depth, 1, n))


def _norm_mod(x, g, shift, scale):
    ms = jnp.mean(x * x, axis=-1, keepdims=True)
    y = (x * lax.rsqrt(ms + EPS)) * g
    return y * (1.0 + scale) + shift


class _Tiling:
    def __init__(self, tm, n_ctx_tokens, lat_seq, n_lat_seq, d):
        self.tm = tm
        self.n_ctx_tiles = n_ctx_tokens // tm
        self.tiles_per_seq = lat_seq // tm
        self.ctx_row = n_lat_seq
        self.d = d

    def mod_spec(self, layer, which):
        def index(i, *_):
            row = jnp.where(i < self.n_ctx_tiles, self.ctx_row, (i - self.n_ctx_tiles) // self.tiles_per_seq)
            return (layer, row * N_MOD + which, 0, 0)
        return pl.BlockSpec((None, None, 1, self.d), index)


def _layer_vec_spec(layer, d):
    return pl.BlockSpec((None, 1, d), lambda *_: (layer, 0, 0))


def _qkv_kernel(*refs, use_a, d_model, n_ctx_tiles, split_input):
    if split_input:
        (xc_ref, xl_ref, sh_ref, sc_ref, g_ref, w_ref, qn_ref, kn_ref, cos_ref, sin_ref, ones_ref, _, _,
         q_ref, kf_ref, vf_ref, kh_ref, vh_ref, x_ref, h_ref, y_ref, ss_ref) = refs
    else:
        (x_ref, sh_ref, sc_ref, g_ref, w_ref, qn_ref, kn_ref, cos_ref, sin_ref, ones_ref, _, _,
         q_ref, kf_ref, vf_ref, kh_ref, vh_ref, h_ref, y_ref, ss_ref) = refs
    q_dim = d_model
    kv_dim = N_KV_HEADS * HEAD_DIM
    is_ctx = pl.program_id(0) < n_ctx_tiles
    if split_input:
        @pl.when(is_ctx)
        def _():
            x_ref[...] = xc_ref[...]

        @pl.when(jnp.logical_not(is_ctx))
        def _():
            x_ref[...] = xl_ref[...]
    h_ref[...] = _norm_mod(x_ref[...], g_ref[...], sh_ref[...], sc_ref[...]).astype(BF16)

    cos = cos_ref[...]
    sin = sin_ref[...]
    even = (lax.broadcasted_iota(jnp.int32, cos.shape, 1) % 2) == 0

    for c0 in range(0, q_dim + 2 * kv_dim, MXU_DIM):
        y_ref[:, c0:c0 + MXU_DIM] = jnp.dot(h_ref[...], w_ref[:, c0:c0 + MXU_DIM], preferred_element_type=F32)
    if use_a:
        for c0 in range(0, q_dim + kv_dim, MXU_DIM):
            yc = y_ref[:, c0:c0 + MXU_DIM]
            ss_ref[:, c0:c0 + MXU_DIM] = jnp.dot((yc * yc).astype(BF16), ones_ref[...], preferred_element_type=F32)

    def project(col0, gain):
        xc = y_ref[:, col0:col0 + MXU_DIM]
        if use_a and gain is not None:
            ss = ss_ref[:, col0:col0 + MXU_DIM]
            xc = xc * lax.rsqrt(ss * (1.0 / HEAD_DIM) + EPS) * gain
        return xc

    def rope(xc):
        cols = []
        for l0 in range(0, MXU_DIM, LANES):
            xl = xc[:, l0:l0 + LANES]
            swapped = jnp.where(even, pltpu.roll(xl, LANES - 1, 1), pltpu.roll(xl, 1, 1))
            cols.append(xl * cos + swapped * sin)
        return jnp.concatenate(cols, axis=-1)

    for c0 in range(0, q_dim, MXU_DIM):
        qc = rope(project(c0, qn_ref[:, c0:c0 + MXU_DIM]))
        q_ref[:, c0:c0 + MXU_DIM] = (qc * (ATTN_SCALE * LOG2E)).astype(BF16)
    k = rope(project(q_dim, kn_ref[...]))
    v = project(q_dim + kv_dim, None)
    lane = lax.broadcasted_iota(jnp.int32, cos.shape, 1)
    ones_col = jnp.where(lane == HEAD_DIM, 1.0, 0.0)
    for hh in range(N_KV_HEADS):
        kh_ref[hh] = k[:, hh * HEAD_DIM:(hh + 1) * HEAD_DIM].astype(BF16)
        pair = v[:, (hh // 2) * LANES:(hh // 2 + 1) * LANES]
        if hh % 2:
            pair = pltpu.roll(pair, HEAD_DIM, 1)
        vh_ref[hh] = jnp.where(lane < HEAD_DIM, pair, ones_col).astype(BF16)

    @pl.when(is_ctx)
    def _():
        kf_ref[...] = k.reshape(kf_ref.shape)
        vf_ref[...] = v.reshape(vf_ref.shape)


def _qkv(x, mods, g_all, w_all, qn, kn, cos, sin, ones, k_state, v_state, *, layer, use_a, til):
    split_input = isinstance(x, tuple)
    seqs_per_tile = til.tm // k_state.shape[2]
    state_block = (seqs_per_tile, None) + k_state.shape[2:]
    d = til.d
    t = sum(part.shape[0] for part in x) if split_input else x.shape[0]
    tm = til.tm
    kv_dim = N_KV_HEADS * HEAD_DIM
    n_out = w_all.shape[2]
    n_ctx_tiles = til.n_ctx_tiles
    row = lambda i: (i, 0)
    fixed = lambda i: (0, 0)
    ctx_row = lambda i: (jnp.minimum(i, n_ctx_tiles - 1), 0)
    lat_row = lambda i: (jnp.maximum(i - n_ctx_tiles, 0), 0)
    state_spec = pl.BlockSpec(state_block, lambda i: (jnp.minimum(i, n_ctx_tiles - 1), layer, 0, 0))
    if split_input:
        x_specs = [pl.BlockSpec((tm, d), ctx_row), pl.BlockSpec((tm, d), lat_row)]
        x_args = list(x)
        extra_out_specs = [pl.BlockSpec((tm, d), row)]
        extra_out_shape = [jax.ShapeDtypeStruct((t, d), F32)]
    else:
        x_specs = [pl.BlockSpec((tm, d), row)]
        x_args = [x]
        extra_out_specs = []
        extra_out_shape = []
    return pl.pallas_call(
        functools.partial(_qkv_kernel, use_a=use_a, d_model=d, n_ctx_tiles=n_ctx_tiles, split_input=split_input),
        grid=(t // tm,),
        in_specs=[
            *x_specs,
            til.mod_spec(layer, 0),
            til.mod_spec(layer, 1),
            _layer_vec_spec(layer, d),
            pl.BlockSpec((None, d, n_out), lambda i: (layer, 0, 0)),
            pl.BlockSpec((1, d), fixed),
            pl.BlockSpec((1, kv_dim), fixed),
            pl.BlockSpec((tm, LANES), row),
            pl.BlockSpec((tm, LANES), row),
            pl.BlockSpec((MXU_DIM, MXU_DIM), fixed),
            pl.BlockSpec(memory_space=pl.ANY),
            pl.BlockSpec(memory_space=pl.ANY),
        ],
        out_specs=[
            pl.BlockSpec((tm, d), row),
            state_spec,
            state_spec,
            pl.BlockSpec((N_KV_HEADS, tm, HEAD_DIM), lambda i: (0, i, 0)),
            pl.BlockSpec((N_KV_HEADS, tm, V_WIDTH), lambda i: (0, i, 0)),
            *extra_out_specs,
        ],
        out_shape=[
            jax.ShapeDtypeStruct((t, d), BF16),
            jax.ShapeDtypeStruct(k_state.shape, F32),
            jax.ShapeDtypeStruct(v_state.shape, F32),
            jax.ShapeDtypeStruct((N_KV_HEADS, t, HEAD_DIM), BF16),
            jax.ShapeDtypeStruct((N_KV_HEADS, t, V_WIDTH), BF16),
            *extra_out_shape,
        ],
        scratch_shapes=[pltpu.VMEM((tm, d), BF16), pltpu.VMEM((tm, n_out), F32),
                        pltpu.VMEM((tm, d + kv_dim), F32)],
        input_output_aliases={len(x_args) + 9: 1, len(x_args) + 10: 2},
        compiler_params=_params(("arbitrary",)),
    )(*x_args, mods, mods, g_all, w_all, qn, kn, cos, sin, ones, k_state, v_state)


def _softmax_pv(score_parts, value_parts, sink):
    m = score_parts[0].max(axis=-1, keepdims=True)
    for s in score_parts[1:]:
        m = jnp.maximum(m, s.max(axis=-1, keepdims=True))
    if sink is not None:
        m = jnp.maximum(m, sink)
    acc = None
    for s, v in zip(score_parts, value_parts):
        pv = jnp.dot(jnp.exp2(s - m).astype(BF16), v, preferred_element_type=F32)
        acc = pv if acc is None else acc + pv
    denom = acc[:, HEAD_DIM:HEAD_DIM + 1]
    if sink is not None:
        denom = denom + jnp.exp2(sink - m)
    return acc[:, :HEAD_DIM] / denom


def _attn_ctx_kernel(sink_ref, q_ref, k_ref, v_ref, o_ref, s_ref, p_ref, m_ref, *, has_sink):
    def cols(head):
        return slice(head * HEAD_DIM, (head + 1) * HEAD_DIM)

    def scores(head):
        s_ref[head] = lax.dot_general(q_ref[:, cols(head)], k_ref[head // GROUP], NT_DIMS,
                                      preferred_element_type=F32)

    def numerators(head):
        s = s_ref[head]
        m = s.max(axis=-1, keepdims=True)
        if has_sink:
            m = jnp.maximum(m, sink_ref[head])
            m_ref[head] = jnp.broadcast_to(jnp.exp2(sink_ref[head] - m), m_ref.shape[1:])
        p_ref[head] = jnp.exp2(s - m).astype(BF16)

    def outputs(head):
        acc = jnp.dot(p_ref[head], v_ref[head // GROUP], preferred_element_type=F32)
        denom = acc[:, HEAD_DIM:HEAD_DIM + 1]
        if has_sink:
            denom = denom + m_ref[head][:, :1]
        o_ref[:, cols(head)] = (acc[:, :HEAD_DIM] / denom).astype(o_ref.dtype)

    def fused_outputs(head):
        sink = sink_ref[head] if has_sink else None
        o_ref[:, cols(head)] = _softmax_pv([s_ref[head]], [v_ref[head // GROUP]], sink).astype(o_ref.dtype)

    if has_sink:
        _phased(N_HEADS, (scores, fused_outputs))
    else:
        _phased(N_HEADS, (scores, numerators, outputs))


def _attn_ctx(sink, q, kh, vh, *, n_ctx_seq, seq, has_sink):
    t, d = q.shape
    return pl.pallas_call(
        functools.partial(_attn_ctx_kernel, has_sink=has_sink),
        grid=(n_ctx_seq,),
        in_specs=[
            pl.BlockSpec(memory_space=pltpu.SMEM),
            pl.BlockSpec((seq, d), lambda b: (b, 0)),
            pl.BlockSpec((N_KV_HEADS, seq, HEAD_DIM), lambda b: (0, b, 0)),
            pl.BlockSpec((N_KV_HEADS, seq, V_WIDTH), lambda b: (0, b, 0)),
        ],
        out_specs=pl.BlockSpec((seq, d), lambda b: (b, 0)),
        out_shape=jax.ShapeDtypeStruct((n_ctx_seq * seq, d), BF16),
        scratch_shapes=[pltpu.VMEM((N_HEADS, seq, seq), F32), pltpu.VMEM((N_HEADS, seq, seq), BF16),
                        pltpu.VMEM((N_HEADS, seq, LANES), F32)],
        compiler_params=_params(("arbitrary",)),
    )(sink, q, kh, vh)


def _phased(n, stages):
    for stage in stages:
        for j in range(n):
            stage(j)


def _attn_global_kernel(q_ref, k_ref, v_ref, kc_ref, vc_ref, o_ref, sl_ref, sc_ref):
    def cols(j):
        return slice(j * HEAD_DIM, (j + 1) * HEAD_DIM)

    def scores(j):
        qj = q_ref[:, cols(j)]
        sl_ref[j] = lax.dot_general(qj, k_ref[...], NT_DIMS, preferred_element_type=F32)
        sc_ref[j] = lax.dot_general(qj, kc_ref[...], NT_DIMS, preferred_element_type=F32)

    def outputs(j):
        o = _softmax_pv([sl_ref[j], sc_ref[j]], [v_ref[...], vc_ref[...]], None)
        o_ref[:, cols(j)] = o.astype(o_ref.dtype)

    _phased(GROUP, (scores, outputs))


def _cache_spec(layer, past, width):
    return pl.BlockSpec((None, None, past, width), lambda b, g, i: (layer, g, b, 0))


def _attn_global(q, kh, vh, kc_all, vc_all, *, layer, n_ctx_tokens, n_lat_seq, lat_seq, past):
    t, d = q.shape
    gw = GROUP * HEAD_DIM
    tq = GLOBAL_Q_TILE if (lat_seq % GLOBAL_Q_TILE == 0 and n_ctx_tokens % GLOBAL_Q_TILE == 0) else Q_TILE
    q_blocks = lat_seq // tq
    q0 = n_ctx_tokens // tq
    s0 = n_ctx_tokens // lat_seq
    return pl.pallas_call(
        _attn_global_kernel,
        grid=(n_lat_seq, N_KV_HEADS, q_blocks),
        in_specs=[
            pl.BlockSpec((tq, gw), lambda b, g, i: (q0 + b * q_blocks + i, g)),
            pl.BlockSpec((None, lat_seq, HEAD_DIM), lambda b, g, i: (g, s0 + b, 0)),
            pl.BlockSpec((None, lat_seq, V_WIDTH), lambda b, g, i: (g, s0 + b, 0)),
            _cache_spec(layer, past, HEAD_DIM),
            _cache_spec(layer, past, V_WIDTH),
        ],
        out_specs=pl.BlockSpec((tq, gw), lambda b, g, i: (b * q_blocks + i, g)),
        out_shape=jax.ShapeDtypeStruct((n_lat_seq * lat_seq, d), BF16),
        scratch_shapes=[pltpu.VMEM((GROUP, tq, lat_seq), F32), pltpu.VMEM((GROUP, tq, past), F32)],
        compiler_params=_params(("arbitrary", "arbitrary", "arbitrary")),
    )(q, kh, vh, kc_all, vc_all)


def _attn_window_kernel(sink_ref, q_ref, kp_ref, kcur_ref, kn_ref, vp_ref, vcur_ref, vn_ref,
                        kc_ref, vc_ref, o_ref, sl_ref, sc_ref, *, lat_seq, tq):
    grp = pl.program_id(1)
    start = pl.program_id(2) * tq
    span = tq + 2 * WINDOW
    kw = jnp.concatenate([kp_ref[...], kcur_ref[...], kn_ref[...]], axis=0)
    vw = jnp.concatenate([vp_ref[...], vcur_ref[...], vn_ref[...]], axis=0)
    qpos = start + lax.broadcasted_iota(jnp.int32, (tq, span), 0)
    kpos = start - WINDOW + lax.broadcasted_iota(jnp.int32, (tq, span), 1)
    valid = (kpos >= 0) & (kpos < lat_seq) & (jnp.abs(qpos - kpos) <= WINDOW)

    def cols(j):
        return slice(j * HEAD_DIM, (j + 1) * HEAD_DIM)

    def scores(j):
        qj = q_ref[:, cols(j)]
        s_loc = lax.dot_general(qj, kw, NT_DIMS, preferred_element_type=F32)
        sl_ref[j] = jnp.where(valid, s_loc, NEG_INF)
        sc_ref[j] = lax.dot_general(qj, kc_ref[...], NT_DIMS, preferred_element_type=F32)

    def outputs(j):
        o = _softmax_pv([sl_ref[j], sc_ref[j]], [vw, vc_ref[...]], sink_ref[grp * GROUP + j])
        o_ref[:, cols(j)] = o.astype(o_ref.dtype)

    _phased(GROUP, (scores, outputs))


def _attn_window(sink, q, kh, vh, kc_all, vc_all, *, layer, n_ctx_tokens, n_lat_seq, lat_seq, past):
    t, d = q.shape
    gw = GROUP * HEAD_DIM
    tq = WINDOW_Q_TILE if (lat_seq % WINDOW_Q_TILE == 0 and n_ctx_tokens % WINDOW_Q_TILE == 0) else Q_TILE
    q_blocks = lat_seq // tq
    q0 = n_ctx_tokens // tq
    span = tq + 2 * WINDOW
    half = tq // WINDOW
    w0 = n_ctx_tokens // WINDOW
    w_blocks = lat_seq // WINDOW
    prev = lambda b, g, i: (g, w0 + b * w_blocks + jnp.maximum(i * half - 1, 0), 0)
    cur = lambda b, g, i: (g, q0 + b * q_blocks + i, 0)
    nxt = lambda b, g, i: (g, w0 + b * w_blocks + jnp.minimum((i + 1) * half, w_blocks - 1), 0)
    def band_specs(width):
        return [
            pl.BlockSpec((None, WINDOW, width), prev),
            pl.BlockSpec((None, tq, width), cur),
            pl.BlockSpec((None, WINDOW, width), nxt),
        ]
    return pl.pallas_call(
        functools.partial(_attn_window_kernel, lat_seq=lat_seq, tq=tq),
        grid=(n_lat_seq, N_KV_HEADS, q_blocks),
        in_specs=[
            pl.BlockSpec(memory_space=pltpu.SMEM),
            pl.BlockSpec((tq, gw), lambda b, g, i: (q0 + b * q_blocks + i, g)),
            *band_specs(HEAD_DIM), *band_specs(V_WIDTH),
            _cache_spec(layer, past, HEAD_DIM),
            _cache_spec(layer, past, V_WIDTH),
        ],
        out_specs=pl.BlockSpec((tq, gw), lambda b, g, i: (b * q_blocks + i, g)),
        out_shape=jax.ShapeDtypeStruct((n_lat_seq * lat_seq, d), BF16),
        scratch_shapes=[pltpu.VMEM((GROUP, tq, span), F32), pltpu.VMEM((GROUP, tq, past), F32)],
        compiler_params=_params(("arbitrary", "arbitrary", "arbitrary")),
    )(sink, q, kh, kh, kh, vh, vh, vh, kc_all, vc_all)


def _attn_residual(oc_ref, ol_ref, wo_ref, x_ref, gate_ref, dst_ref, n_ctx_tiles):
    def project(o_ref):
        y = jnp.dot(o_ref[...], wo_ref[...], preferred_element_type=F32)
        dst_ref[...] = x_ref[...] + gate_ref[...] * y

    @pl.when(pl.program_id(0) < n_ctx_tiles)
    def _():
        project(oc_ref)

    @pl.when(pl.program_id(0) >= n_ctx_tiles)
    def _():
        project(ol_ref)


def _attn_residual_specs(layer, til):
    tm, d, n_ctx_tiles = til.tm, til.d, til.n_ctx_tiles
    return [
        pl.BlockSpec((tm, d), lambda i, *_: (jnp.minimum(i, n_ctx_tiles - 1), 0)),
        pl.BlockSpec((tm, d), lambda i, *_: (jnp.maximum(i - n_ctx_tiles, 0), 0)),
        pl.BlockSpec((None, d, d), lambda i, *_: (layer, 0, 0), pipeline_mode=pl.Buffered(1)),
        pl.BlockSpec((tm, d), lambda i, *_: (i, 0)),
        til.mod_spec(layer, 2),
    ]


def _swiglu_act(h, wg, wu):
    g = jnp.dot(h, wg, preferred_element_type=F32)
    u = jnp.dot(h, wu, preferred_element_type=F32)
    return (g / (1.0 + jnp.exp(-g))) * u


def _ffn_kernel(oc_ref, ol_ref, wo_ref, x_ref, g1_ref, sh_ref, sc_ref, gate_ref, g_ref, wg_ref, wu_ref, wd_ref,
                out_ref, x1_ref, *, n_ctx_tiles):
    _attn_residual(oc_ref, ol_ref, wo_ref, x_ref, g1_ref, x1_ref, n_ctx_tiles)
    x = x1_ref[...]
    h = _norm_mod(x, g_ref[...], sh_ref[...], sc_ref[...]).astype(BF16)
    a = _swiglu_act(h, wg_ref[...], wu_ref[...])
    f = jnp.dot(a.astype(BF16), wd_ref[...], preferred_element_type=F32)
    out_ref[...] = x + gate_ref[...] * f


def _ffn(o_ctx, o_lat, w_o_all, x, mods, g_all, w_up_all, w_down_all, *, layer, idx, til):
    t, d = x.shape
    tm = til.tm
    d_ff = w_down_all.shape[1]
    row = lambda i: (i, 0)
    resident = pl.Buffered(1)
    return pl.pallas_call(
        functools.partial(_ffn_kernel, n_ctx_tiles=til.n_ctx_tiles),
        grid=(t // tm,),
        in_specs=[
            *_attn_residual_specs(layer, til),
            til.mod_spec(layer, 3),
            til.mod_spec(layer, 4),
            til.mod_spec(layer, 5),
            _layer_vec_spec(layer, d),
            pl.BlockSpec((None, d, d_ff), lambda i: (idx, 0, 0), pipeline_mode=resident),
            pl.BlockSpec((None, d, d_ff), lambda i: (idx, 0, 1), pipeline_mode=resident),
            pl.BlockSpec((None, d_ff, d), lambda i: (idx, 0, 0), pipeline_mode=resident),
        ],
        out_specs=pl.BlockSpec((tm, d), row),
        out_shape=jax.ShapeDtypeStruct((t, d), F32),
        scratch_shapes=[pltpu.VMEM((tm, d), F32)],
        compiler_params=_params(("arbitrary",)),
    )(o_ctx, o_lat, w_o_all, x, mods, mods, mods, mods, g_all, w_up_all, w_up_all, w_down_all)


META_IDX, META_GATE, META_RANK = 0, 2, 4


def _router_kernel(oc_ref, ol_ref, wo_ref, x_ref, g1_ref, sh_ref, sc_ref, g_ref, wr_ref, tri_ref,
                   x1_ref, meta_ref, meta_t_ref, cnt_ref, run_ref, *, n_experts, n_ctx_tiles):
    @pl.when(pl.program_id(0) == 0)
    def _():
        run_ref[...] = jnp.zeros_like(run_ref)

    _attn_residual(oc_ref, ol_ref, wo_ref, x_ref, g1_ref, x1_ref, n_ctx_tiles)
    h = _norm_mod(x1_ref[...], g_ref[...], sh_ref[...], sc_ref[...])
    logits = jnp.dot(h, wr_ref[...], preferred_element_type=F32, precision=lax.Precision.HIGHEST)
    lane = lax.broadcasted_iota(jnp.int32, logits.shape, 1).astype(F32)
    logits = jnp.where(lane < n_experts, logits, -jnp.inf)
    top1 = logits.max(axis=-1, keepdims=True)
    idx1 = jnp.where(logits == top1, lane, float(LANES)).min(axis=-1, keepdims=True)
    rest = jnp.where(lane == idx1, -jnp.inf, logits)
    top2 = rest.max(axis=-1, keepdims=True)
    idx2 = jnp.where(rest == top2, lane, float(LANES)).min(axis=-1, keepdims=True)
    e = jnp.exp(top2 - top1)
    g1 = 1.0 / (1.0 + e)
    g2 = e / (1.0 + e)

    sel1 = lane == idx1
    sel2 = lane == idx2
    sel = jnp.where(sel1, 1.0, 0.0) + jnp.where(sel2, 1.0, 0.0)
    before = jnp.dot(tri_ref[...], sel.astype(BF16), preferred_element_type=F32) + run_ref[...]
    r1 = jnp.where(sel1, before, 0.0).sum(axis=-1, keepdims=True)
    r2 = jnp.where(sel2, before, 0.0).sum(axis=-1, keepdims=True)
    run_ref[...] += sel.sum(axis=0, keepdims=True)
    cnt_ref[...] = run_ref[...]

    meta = jnp.zeros_like(logits)
    for off, (a, b) in ((META_IDX, (idx1, idx2)), (META_GATE, (g1, g2)), (META_RANK, (r1, r2))):
        meta = jnp.where(lane == off, a, meta)
        meta = jnp.where(lane == off + 1, b, meta)
    meta_ref[...] = meta
    meta_t_ref[...] = meta.T[:SUBLANES]


def _router(o_ctx, o_lat, w_o_all, x, mods, g_all, wr, tri, *, layer, n_experts, til):
    t, d = x.shape
    tm = til.tm
    row = lambda i: (i, 0)
    fixed = lambda i: (0, 0)
    return pl.pallas_call(
        functools.partial(_router_kernel, n_experts=n_experts, n_ctx_tiles=til.n_ctx_tiles),
        grid=(t // tm,),
        in_specs=[
            *_attn_residual_specs(layer, til),
            til.mod_spec(layer, 3),
            til.mod_spec(layer, 4),
            _layer_vec_spec(layer, d),
            pl.BlockSpec((d, LANES), fixed),
            pl.BlockSpec((tm, tm), fixed),
        ],
        out_specs=[pl.BlockSpec((tm, d), row), pl.BlockSpec((tm, LANES), row),
                   pl.BlockSpec((SUBLANES, tm), lambda i: (0, i)), pl.BlockSpec((1, LANES), fixed)],
        out_shape=[jax.ShapeDtypeStruct((t, d), F32), jax.ShapeDtypeStruct((t, LANES), F32),
                   jax.ShapeDtypeStruct((SUBLANES, t), F32), jax.ShapeDtypeStruct((1, LANES), F32)],
        scratch_shapes=[pltpu.VMEM((1, LANES), F32)],
        compiler_params=_params(("arbitrary",)),
    )(o_ctx, o_lat, w_o_all, x, mods, mods, mods, g_all, wr, tri)


def _row_copy(src_ref, src_row, dst_ref, dst_row, sem):
    return pltpu.make_async_copy(src_ref.at[pl.ds(src_row, 1)], dst_ref.at[pl.ds(dst_row, 1)], sem)


def _dispatch_kernel(pos_ref, x_ref, sh_ref, sc_ref, g_ref, hs_init_ref, hs_ref, h_ref, sem, *, tm, n_tokens):
    del hs_init_ref
    i = pl.program_id(0)
    slot = i % 2

    def wait_slot(s):
        for _ in range(TOP_K * tm):
            _row_copy(h_ref.at[s], 0, hs_ref, 0, sem.at[s]).wait()

    @pl.when(i >= 2)
    def _():
        wait_slot(slot)

    h_ref[slot] = _norm_mod(x_ref[...], g_ref[...], sh_ref[...], sc_ref[...])

    for r in range(tm):
        for k in range(TOP_K):
            dst = pos_ref[k * n_tokens + i * tm + r]
            _row_copy(h_ref.at[slot], r, hs_ref, dst, sem.at[slot]).start(priority=k)

    @pl.when(i == pl.num_programs(0) - 1)
    def _():
        wait_slot(slot)

        @pl.when(i >= 1)
        def _():
            wait_slot(1 - slot)


def _dispatch(pos, x, mods, g_all, hs_init, *, layer, til):
    t, d = x.shape
    tm = til.tm
    return pl.pallas_call(
        functools.partial(_dispatch_kernel, tm=tm, n_tokens=t),
        grid_spec=pltpu.PrefetchScalarGridSpec(
            num_scalar_prefetch=1,
            grid=(t // tm,),
            in_specs=[
                pl.BlockSpec((tm, d), lambda i, pos: (i, 0)),
                til.mod_spec(layer, 3),
                til.mod_spec(layer, 4),
                _layer_vec_spec(layer, d),
                pl.BlockSpec(memory_space=pl.ANY),
            ],
            out_specs=pl.BlockSpec(memory_space=pl.ANY),
            scratch_shapes=[pltpu.VMEM((2, tm, d), F32), pltpu.SemaphoreType.DMA((2,))],
        ),
        out_shape=jax.ShapeDtypeStruct(hs_init.shape, F32),
        input_output_aliases={5: 0},
        compiler_params=_params(("arbitrary",)),
    )(pos, x, mods, mods, g_all, hs_init)


def _expert_kernel(te_ref, tv_ref, hs_ref, wg_ref, wu_ref, wd_ref, ys_ref, h_ref, acc_ref):
    del te_ref
    j = pl.program_id(1)

    @pl.when(tv_ref[pl.program_id(0)] != 0)
    def _():
        @pl.when(j == 0)
        def _():
            h_ref[...] = hs_ref[...].astype(BF16)
            acc_ref[...] = jnp.zeros_like(acc_ref)

        a = _swiglu_act(h_ref[...], wg_ref[...], wu_ref[...])
        acc_ref[...] += jnp.dot(a.astype(BF16), wd_ref[...], preferred_element_type=F32)

        @pl.when(j == pl.num_programs(1) - 1)
        def _():
            ys_ref[...] = acc_ref[...]

    @pl.when((tv_ref[pl.program_id(0)] == 0) & (j == 0))
    def _():
        ys_ref[...] = jnp.zeros_like(ys_ref)


def _experts(tile_expert, tile_valid, hs, w_up_all, w_down_all, *, idx, tm, tf):
    p, d = hs.shape
    d_ff = w_down_all.shape[2]
    nf = d_ff // tf
    return pl.pallas_call(
        _expert_kernel,
        grid_spec=pltpu.PrefetchScalarGridSpec(
            num_scalar_prefetch=2,
            grid=(p // tm, nf),
            in_specs=[
                pl.BlockSpec((tm, d), lambda r, j, te, tv: (r, 0)),
                pl.BlockSpec((None, None, d, tf), lambda r, j, te, tv: (idx, te[r], 0, j)),
                pl.BlockSpec((None, None, d, tf), lambda r, j, te, tv: (idx, te[r], 0, nf + j)),
                pl.BlockSpec((None, None, tf, d), lambda r, j, te, tv: (idx, te[r], j, 0)),
            ],
            out_specs=pl.BlockSpec((tm, d), lambda r, j, te, tv: (r, 0)),
            scratch_shapes=[pltpu.VMEM((tm, d), BF16), pltpu.VMEM((tm, d), F32)],
        ),
        out_shape=jax.ShapeDtypeStruct((p, d), F32),
        compiler_params=_params(("arbitrary", "arbitrary")),
    )(tile_expert, tile_valid, hs, w_up_all, w_up_all, w_down_all)


def _combine_kernel(pos_ref, ys_ref, meta_ref, x_ref, gate_ref, *rest, tm, n_tokens, n_ctx_tiles, final):
    if final:
        gf_ref, yc_ref, yl_ref, y_ref, sem = rest
    else:
        out_ref, y_ref, sem = rest
    i = pl.program_id(0)
    slot = i % 2

    def fetch(step, s):
        for r in range(tm):
            for k in range(TOP_K):
                src = pos_ref[k * n_tokens + step * tm + r]
                _row_copy(ys_ref, src, y_ref.at[s, k], r, sem.at[s]).start(priority=k)

    @pl.when(i == 0)
    def _():
        fetch(0, 0)

    @pl.when(i + 1 < pl.num_programs(0))
    def _():
        fetch(i + 1, 1 - slot)

    for _ in range(tm):
        for k in range(TOP_K):
            _row_copy(ys_ref, 0, y_ref.at[slot, k], 0, sem.at[slot]).wait()

    meta = meta_ref[...]
    f = meta[:, META_GATE:META_GATE + 1] * y_ref[slot, 0] + meta[:, META_GATE + 1:META_GATE + 2] * y_ref[slot, 1]
    x_new = x_ref[...] + gate_ref[...] * f
    if not final:
        out_ref[...] = x_new
        return

    ms = jnp.mean(x_new * x_new, axis=-1, keepdims=True)
    y = (x_new * lax.rsqrt(ms + EPS)) * gf_ref[...]

    @pl.when(i < n_ctx_tiles)
    def _():
        yc_ref[...] = y

    @pl.when(i >= n_ctx_tiles)
    def _():
        yl_ref[...] = y


def _combine(pos, ys, meta, x, mods, final_gain, *, layer, til):
    t, d = x.shape
    tm = til.tm
    n_ctx_tiles = til.n_ctx_tiles
    final = final_gain is not None
    row = lambda i, pos: (i, 0)
    in_specs = [
        pl.BlockSpec(memory_space=pl.ANY),
        pl.BlockSpec((tm, LANES), row),
        pl.BlockSpec((tm, d), row),
        til.mod_spec(layer, 5),
    ]
    args = [pos, ys, meta, x, mods]
    if final:
        in_specs.append(pl.BlockSpec((1, d), lambda i, pos: (0, 0)))
        args.append(final_gain)
        out_specs = [pl.BlockSpec((tm, d), lambda i, pos: (jnp.minimum(i, n_ctx_tiles - 1), 0)),
                     pl.BlockSpec((tm, d), lambda i, pos: (jnp.maximum(i - n_ctx_tiles, 0), 0))]
        out_shape = [jax.ShapeDtypeStruct((n_ctx_tiles * tm, d), F32),
                     jax.ShapeDtypeStruct((t - n_ctx_tiles * tm, d), F32)]
    else:
        out_specs = pl.BlockSpec((tm, d), row)
        out_shape = jax.ShapeDtypeStruct((t, d), F32)
    return pl.pallas_call(
        functools.partial(_combine_kernel, tm=tm, n_tokens=t, n_ctx_tiles=n_ctx_tiles, final=final),
        grid_spec=pltpu.PrefetchScalarGridSpec(
            num_scalar_prefetch=1,
            grid=(t // tm,),
            in_specs=in_specs,
            out_specs=out_specs,
            scratch_shapes=[pltpu.VMEM((2, TOP_K, tm, d), F32), pltpu.SemaphoreType.DMA((2,))],
        ),
        out_shape=out_shape,
        compiler_params=_params(("arbitrary",)),
    )(*args)


def _moe(o_ctx, o_lat, w_o_all, x, mods, g_all, w_router, w_up_all, w_down_all, final_gain, sorted_buf,
         *, layer, idx, til, tf):
    t, d = x.shape
    n_experts = w_router.shape[-1]
    tm = til.tm
    wr = jnp.zeros((d, LANES), F32).at[:, :n_experts].set(w_router)
    tri = (jnp.arange(tm)[:, None] > jnp.arange(tm)[None, :]).astype(BF16)
    x, meta, meta_t, counts = _router(o_ctx, o_lat, w_o_all, x, mods, g_all, wr, tri, layer=layer,
                                      n_experts=n_experts, til=til)

    counts = counts[0, :n_experts].astype(jnp.int32)
    padded = ((counts + tm - 1) // tm) * tm
    ends = jnp.cumsum(padded)
    starts = ends - padded
    choice = meta_t[META_IDX:META_IDX + TOP_K].astype(jnp.int32)
    rank = meta_t[META_RANK:META_RANK + TOP_K].astype(jnp.int32)
    group_start = sum(jnp.where(choice == e, starts[e], 0) for e in range(n_experts))
    pos = (group_start + rank).reshape(TOP_K * t)
    n_tiles = (t * TOP_K) // tm + n_experts
    tile_start = jnp.arange(n_tiles, dtype=jnp.int32) * tm
    tile_expert = jnp.minimum((tile_start[:, None] >= ends[None, :]).sum(axis=-1), n_experts - 1).astype(jnp.int32)
    tile_valid = (tile_start < ends[-1]).astype(jnp.int32)

    if sorted_buf is None:
        sorted_buf = jnp.zeros((n_tiles * tm, d), F32)
    hs = _dispatch(pos, x, mods, g_all, sorted_buf, layer=layer, til=til)
    ys = _experts(tile_expert, tile_valid, hs, w_up_all, w_down_all, idx=idx, tm=tm, tf=tf)
    return _combine(pos, ys, meta, x, mods, final_gain, layer=layer, til=til), hs


def _final_norm_kernel(x_ref, g_ref, o_ref):
    x = x_ref[...]
    ms = jnp.mean(x * x, axis=-1, keepdims=True)
    o_ref[...] = (x * lax.rsqrt(ms + EPS)) * g_ref[...]


def _final_norm(x, g, *, tm, first_tile, n_tiles):
    d = x.shape[1]
    return pl.pallas_call(
        _final_norm_kernel,
        grid=(n_tiles,),
        in_specs=[pl.BlockSpec((tm, d), lambda i: (first_tile + i, 0)), pl.BlockSpec((1, d), lambda i: (0, 0))],
        out_specs=pl.BlockSpec((tm, d), lambda i: (i, 0)),
        out_shape=jax.ShapeDtypeStruct((n_tiles * tm, d), F32),
        compiler_params=_params(("arbitrary",)),
    )(x, g)


def _rope_tables(n_ctx_tokens, n_lat_seq, lat_seq):
    pos = jnp.arange(lat_seq)
    rows = (pos // GRID_W).astype(F32)
    cols = (pos % GRID_W).astype(F32)
    pairs_per_axis = HEAD_DIM // 4
    inv = ROPE_THETA ** (-jnp.arange(pairs_per_axis, dtype=F32) / pairs_per_axis)
    ang = jnp.concatenate([rows[:, None] * inv, cols[:, None] * inv], axis=-1)
    cos = jnp.repeat(jnp.cos(ang), 2, axis=-1)
    sin = jnp.repeat(jnp.sin(ang), 2, axis=-1) * jnp.tile(jnp.array([-1.0, 1.0], F32), HEAD_DIM // 2)
    reps = LANES // HEAD_DIM
    cos = jnp.tile(cos, (n_lat_seq, reps))
    sin = jnp.tile(sin, (n_lat_seq, reps))
    cos = jnp.concatenate([jnp.ones((n_ctx_tokens, LANES), F32), cos], axis=0)
    sin = jnp.concatenate([jnp.zeros((n_ctx_tokens, LANES), F32), sin], axis=0)
    return cos, sin


def kernel(x_prompt, x_sample, cache_k, cache_v, c, c_ctx, w_ada, b_ada, norm_mix, norm_ffn, norm_final,
           w_qkv, w_o, q_norm, k_norm, sinks, w_ffn_up, w_ffn_down, w_router, w_exp_up, w_exp_down):
    n_ctx_seq, seq, d = x_prompt.shape
    n_lat_seq, lat_seq, _ = x_sample.shape
    depth = w_ada.shape[0]
    past = cache_k.shape[2]
    n_experts = w_router.shape[-1]
    kv_dim = N_KV_HEADS * HEAD_DIM
    n_ctx_tokens = n_ctx_seq * seq
    n_lat_tokens = n_lat_seq * lat_seq
    assert d == N_HEADS * HEAD_DIM and seq == Q_TILE and lat_seq % Q_TILE == 0 and kv_dim == MXU_DIM
    assert n_ctx_tokens % lat_seq == 0 and n_lat_seq < MOD_ROWS and n_experts <= LANES

    tm = 2 * Q_TILE if (lat_seq % (2 * Q_TILE) == 0 and n_ctx_tokens % (2 * Q_TILE) == 0) else Q_TILE
    til = _Tiling(tm, n_ctx_tokens, lat_seq, n_lat_seq, d)

    x = (x_prompt.reshape(n_ctx_tokens, d), x_sample.reshape(n_lat_tokens, d))

    cvecs = jnp.zeros((MOD_ROWS, d), F32).at[:n_lat_seq].set(c).at[n_lat_seq].set(c_ctx)
    mods = _ada_mods(cvecs, w_ada, b_ada).reshape(depth, MOD_ROWS * N_MOD, 1, d)

    cos, sin = _rope_tables(n_ctx_tokens, n_lat_seq, lat_seq)
    head_id = jnp.arange(MXU_DIM) // HEAD_DIM
    ones = (head_id[:, None] == head_id[None, :]).astype(BF16)

    kc_all = cache_k.transpose(1, 3, 0, 2, 4).reshape(depth, N_KV_HEADS, n_lat_seq * past, HEAD_DIM).astype(BF16)
    vc_all = cache_v.transpose(1, 3, 0, 2, 4).reshape(depth, N_KV_HEADS, n_lat_seq * past, HEAD_DIM).astype(BF16)
    ones_col = (jnp.arange(V_WIDTH - HEAD_DIM) == 0).astype(BF16)
    vc_all = jnp.concatenate([vc_all, jnp.broadcast_to(ones_col, vc_all.shape[:-1] + ones_col.shape)], axis=-1)

    w_qkv_b = w_qkv.astype(BF16)
    w_o_b = w_o.astype(BF16)
    w_ffn_up_b = w_ffn_up.astype(BF16)
    w_ffn_down_b = w_ffn_down.astype(BF16)
    w_exp_up_b = w_exp_up.astype(BF16)
    w_exp_down_b = w_exp_down.astype(BF16)
    g_mix = norm_mix.reshape(depth, 1, d)
    g_ffn = norm_ffn.reshape(depth, 1, d)
    gf = norm_final.reshape(1, d)

    tf_expert = _pick_tile(w_exp_down.shape[2], (1792, 1024, 512, 256, 128))
    lat = dict(n_ctx_tokens=n_ctx_tokens, n_lat_seq=n_lat_seq, lat_seq=lat_seq, past=past)

    k_state = jnp.zeros((n_ctx_seq, depth, seq, kv_dim), F32)
    v_state = jnp.zeros((n_ctx_seq, depth, seq, kv_dim), F32)
    sorted_buf = None
    for l in range(depth):
        idx = l // 2
        use_a = (l % 2 == 0)
        if use_a:
            qn = jnp.tile(q_norm[idx], d // HEAD_DIM).reshape(1, d)
            kn = jnp.tile(k_norm[idx], N_KV_HEADS).reshape(1, kv_dim)
            sink = jnp.zeros((N_HEADS,), F32)
        else:
            qn = jnp.ones((1, d), F32)
            kn = jnp.ones((1, kv_dim), F32)
            sink = sinks[idx].astype(F32) * LOG2E

        q, k_state, v_state, kh, vh, *joined = _qkv(x, mods, g_mix, w_qkv_b, qn, kn, cos, sin, ones, k_state,
                                                    v_state, layer=l, use_a=use_a, til=til)
        if joined:
            x = joined[0]

        o_ctx = _attn_ctx(sink, q, kh, vh, n_ctx_seq=n_ctx_seq, seq=seq, has_sink=not use_a)
        if use_a:
            o_lat = _attn_global(q, kh, vh, kc_all, vc_all, layer=l, **lat)
        else:
            o_lat = _attn_window(sink, q, kh, vh, kc_all, vc_all, layer=l, **lat)
        if use_a:
            x = _ffn(o_ctx, o_lat, w_o_b, x, mods, g_ffn, w_ffn_up_b, w_ffn_down_b, layer=l, idx=idx, til=til)
        else:
            x, sorted_buf = _moe(o_ctx, o_lat, w_o_b, x, mods, g_ffn, w_router[idx], w_exp_up_b, w_exp_down_b,
                                 gf if l == depth - 1 else None, sorted_buf, layer=l, idx=idx, til=til,
                                 tf=tf_expert)

    if depth % 2 == 0:
        y_prompt, y_sample = x
    else:
        y_prompt = _final_norm(x, gf, tm=tm, first_tile=0, n_tiles=til.n_ctx_tiles)
        y_sample = _final_norm(x, gf, tm=tm, first_tile=til.n_ctx_tiles, n_tiles=n_lat_tokens // tm)
    y_prompt = y_prompt.reshape(n_ctx_seq, seq, d)
    y_sample = y_sample.reshape(n_lat_seq, lat_seq, d)
    state_shape = (n_ctx_seq, depth, seq, N_KV_HEADS, HEAD_DIM)
    return (y_prompt, y_sample, k_state.reshape(state_shape), v_state.reshape(state_shape))
```

```python
import functools

import jax
import jax.numpy as jnp
from jax import lax
from jax.experimental import pallas as pl
from jax.experimental.pallas import tpu as pltpu

N_HEADS = 16
N_KV_HEADS = 4
GROUP = N_HEADS // N_KV_HEADS
HEAD_DIM = 64
GRID_W = 64
WINDOW = 128
ROPE_THETA = 10000.0
N_MOD = 6
TOP_K = 2
EPS = 1e-6
NEG_INF = -1e30
ATTN_SCALE = HEAD_DIM ** -0.5
LOG2E = 1.4426950408889634
V_WIDTH = 2 * HEAD_DIM

LANES = 128
SUBLANES = 8
MXU_DIM = 256
MOD_ROWS = 16
Q_TILE = 256
GLOBAL_Q_TILE = 1024
WINDOW_Q_TILE = 256
VMEM_LIMIT = 56 * 1024 * 1024

F32 = jnp.float32
BF16 = jnp.bfloat16
NT_DIMS = (((1,), (1,)), ((), ()))


def _params(sem, vmem=VMEM_LIMIT):
    return pltpu.CompilerParams(dimension_semantics=sem, vmem_limit_bytes=vmem)


def _pick_tile(n, candidates):
    for c in candidates:
        if n % c == 0:
            return c
    return n


def _ada_kernel(c_ref, w_ref, b_ref, o_ref):
    c = c_ref[...]
    a = c / (1.0 + jnp.exp(-c))
    o_ref[...] = jnp.dot(a, w_ref[...], preferred_element_type=F32,
                         precision=lax.Precision.HIGHEST) + b_ref[...]


def _ada_mods(cvecs, w_ada, b_ada):
    depth, d, n = w_ada.shape
    tn = _pick_tile(n, (1536, 1024, 512, 256, 128))
    return pl.pallas_call(
        _ada_kernel,
        grid=(depth, n // tn),
        in_specs=[
            pl.BlockSpec((MOD_ROWS, d), lambda l, j: (0, 0)),
            pl.BlockSpec((None, d, tn), lambda l, j: (l, 0, j)),
            pl.BlockSpec((None, 1, tn), lambda l, j: (l, 0, j)),
        ],
        out_specs=pl.BlockSpec((None, MOD_ROWS, tn), lambda l, j: (l, 0, j)),
        out_shape=jax.ShapeDtypeStruct((depth, MOD_ROWS, n), F32),
        compiler_params=_params(("arbitrary", "arbitrary")),
    )(cvecs, w_ada, b_ada.reshape(depth, 1, n))


def _norm_mod(x, g, shift, scale):
    ms = jnp.mean(x * x, axis=-1, keepdims=True)
    y = (x * lax.rsqrt(ms + EPS)) * g
    return y * (1.0 + scale) + shift


class _Tiling:
    def __init__(self, tm, n_ctx_tokens, lat_seq, n_lat_seq, d):
        self.tm = tm
        self.n_ctx_tiles = n_ctx_tokens // tm
        self.tiles_per_seq = lat_seq // tm
        self.ctx_row = n_lat_seq
        self.d = d

    def mod_spec(self, layer, which):
        def index(i, *_):
            row = jnp.where(i < self.n_ctx_tiles, self.ctx_row, (i - self.n_ctx_tiles) // self.tiles_per_seq)
            return (layer, row * N_MOD + which, 0, 0)
        return pl.BlockSpec((None, None, 1, self.d), index)


def _layer_vec_spec(layer, d):
    return pl.BlockSpec((None, 1, d), lambda *_: (layer, 0, 0))


def _qkv_kernel(*refs, use_a, d_model, n_ctx_tiles, split_input):
    if split_input:
        (xc_ref, xl_ref, sh_ref, sc_ref, g_ref, w_ref, qn_ref, kn_ref, cos_ref, sin_ref, ones_ref, _, _,
         q_ref, kf_ref, vf_ref, kh_ref, vh_ref, x_ref, h_ref, y_ref, ss_ref) = refs
    else:
        (x_ref, sh_ref, sc_ref, g_ref, w_ref, qn_ref, kn_ref, cos_ref, sin_ref, ones_ref, _, _,
         q_ref, kf_ref, vf_ref, kh_ref, vh_ref, h_ref, y_ref, ss_ref) = refs
    q_dim = d_model
    kv_dim = N_KV_HEADS * HEAD_DIM
    is_ctx = pl.program_id(0) < n_ctx_tiles
    if split_input:
        @pl.when(is_ctx)
        def _():
            x_ref[...] = xc_ref[...]

        @pl.when(jnp.logical_not(is_ctx))
        def _():
            x_ref[...] = xl_ref[...]
    h_ref[...] = _norm_mod(x_ref[...], g_ref[...], sh_ref[...], sc_ref[...]).astype(BF16)

    cos = cos_ref[...]
    sin = sin_ref[...]
    even = (lax.broadcasted_iota(jnp.int32, cos.shape, 1) % 2) == 0

    for c0 in range(0, q_dim + 2 * kv_dim, MXU_DIM):
        y_ref[:, c0:c0 + MXU_DIM] = jnp.dot(h_ref[...], w_ref[:, c0:c0 + MXU_DIM], preferred_element_type=F32)
    if use_a:
        for c0 in range(0, q_dim + kv_dim, MXU_DIM):
            yc = y_ref[:, c0:c0 + MXU_DIM]
            ss_ref[:, c0:c0 + MXU_DIM] = jnp.dot((yc * yc).astype(BF16), ones_ref[...], preferred_element_type=F32)

    def project(col0, gain):
        xc = y_ref[:, col0:col0 + MXU_DIM]
        if use_a and gain is not None:
            ss = ss_ref[:, col0:col0 + MXU_DIM]
            xc = xc * lax.rsqrt(ss * (1.0 / HEAD_DIM) + EPS) * gain
        return xc

    def rope(xc):
        cols = []
        for l0 in range(0, MXU_DIM, LANES):
            xl = xc[:, l0:l0 + LANES]
            swapped = jnp.where(even, pltpu.roll(xl, LANES - 1, 1), pltpu.roll(xl, 1, 1))
            cols.append(xl * cos + swapped * sin)
        return jnp.concatenate(cols, axis=-1)

    for c0 in range(0, q_dim, MXU_DIM):
        qc = rope(project(c0, qn_ref[:, c0:c0 + MXU_DIM]))
        q_ref[:, c0:c0 + MXU_DIM] = (qc * (ATTN_SCALE * LOG2E)).astype(BF16)
    k = rope(project(q_dim, kn_ref[...]))
    v = project(q_dim + kv_dim, None)
    lane = lax.broadcasted_iota(jnp.int32, cos.shape, 1)
    ones_col = jnp.where(lane == HEAD_DIM, 1.0, 0.0)
    for hh in range(N_KV_HEADS):
        kh_ref[hh] = k[:, hh * HEAD_DIM:(hh + 1) * HEAD_DIM].astype(BF16)
        pair = v[:, (hh // 2) * LANES:(hh // 2 + 1) * LANES]
        if hh % 2:
            pair = pltpu.roll(pair, HEAD_DIM, 1)
        vh_ref[hh] = jnp.where(lane < HEAD_DIM, pair, ones_col).astype(BF16)

    @pl.when(is_ctx)
    def _():
        kf_ref[...] = k.reshape(kf_ref.shape)
        vf_ref[...] = v.reshape(vf_ref.shape)


def _qkv(x, mods, g_all, w_all, qn, kn, cos, sin, ones, k_state, v_state, *, layer, use_a, til):
    split_input = isinstance(x, tuple)
    seqs_per_tile = til.tm // k_state.shape[2]
    state_block = (seqs_per_tile, None) + k_state.shape[2:]
    d = til.d
    t = sum(part.shape[0] for part in x) if split_input else x.shape[0]
    tm = til.tm
    kv_dim = N_KV_HEADS * HEAD_DIM
    n_out = w_all.shape[2]
    n_ctx_tiles = til.n_ctx_tiles
    row = lambda i: (i, 0)
    fixed = lambda i: (0, 0)
    ctx_row = lambda i: (jnp.minimum(i, n_ctx_tiles - 1), 0)
    lat_row = lambda i: (jnp.maximum(i - n_ctx_tiles, 0), 0)
    state_spec = pl.BlockSpec(state_block, lambda i: (jnp.minimum(i, n_ctx_tiles - 1), layer, 0, 0))
    if split_input:
        x_specs = [pl.BlockSpec((tm, d), ctx_row), pl.BlockSpec((tm, d), lat_row)]
        x_args = list(x)
        extra_out_specs = [pl.BlockSpec((tm, d), row)]
        extra_out_shape = [jax.ShapeDtypeStruct((t, d), F32)]
    else:
        x_specs = [pl.BlockSpec((tm, d), row)]
        x_args = [x]
        extra_out_specs = []
        extra_out_shape = []
    return pl.pallas_call(
        functools.partial(_qkv_kernel, use_a=use_a, d_model=d, n_ctx_tiles=n_ctx_tiles, split_input=split_input),
        grid=(t // tm,),
        in_specs=[
            *x_specs,
            til.mod_spec(layer, 0),
            til.mod_spec(layer, 1),
            _layer_vec_spec(layer, d),
            pl.BlockSpec((None, d, n_out), lambda i: (layer, 0, 0)),
            pl.BlockSpec((1, d), fixed),
            pl.BlockSpec((1, kv_dim), fixed),
            pl.BlockSpec((tm, LANES), row),
            pl.BlockSpec((tm, LANES), row),
            pl.BlockSpec((MXU_DIM, MXU_DIM), fixed),
            pl.BlockSpec(memory_space=pl.ANY),
            pl.BlockSpec(memory_space=pl.ANY),
        ],
        out_specs=[
            pl.BlockSpec((tm, d), row),
            state_spec,
            state_spec,
            pl.BlockSpec((N_KV_HEADS, tm, HEAD_DIM), lambda i: (0, i, 0)),
            pl.BlockSpec((N_KV_HEADS, tm, V_WIDTH), lambda i: (0, i, 0)),
            *extra_out_specs,
        ],
        out_shape=[
            jax.ShapeDtypeStruct((t, d), BF16),
            jax.ShapeDtypeStruct(k_state.shape, F32),
            jax.ShapeDtypeStruct(v_state.shape, F32),
            jax.ShapeDtypeStruct((N_KV_HEADS, t, HEAD_DIM), BF16),
            jax.ShapeDtypeStruct((N_KV_HEADS, t, V_WIDTH), BF16),
            *extra_out_shape,
        ],
        scratch_shapes=[pltpu.VMEM((tm, d), BF16), pltpu.VMEM((tm, n_out), F32),
                        pltpu.VMEM((tm, d + kv_dim), F32)],
        input_output_aliases={len(x_args) + 9: 1, len(x_args) + 10: 2},
        compiler_params=_params(("arbitrary",)),
    )(*x_args, mods, mods, g_all, w_all, qn, kn, cos, sin, ones, k_state, v_state)


def _softmax_pv(score_parts, value_parts, sink):
    m = score_parts[0].max(axis=-1, keepdims=True)
    for s in score_parts[1:]:
        m = jnp.maximum(m, s.max(axis=-1, keepdims=True))
    if sink is not None:
        m = jnp.maximum(m, sink)
    acc = None
    for s, v in zip(score_parts, value_parts):
        pv = jnp.dot(jnp.exp2(s - m).astype(BF16), v, preferred_element_type=F32)
        acc = pv if acc is None else acc + pv
    denom = acc[:, HEAD_DIM:HEAD_DIM + 1]
    if sink is not None:
        denom = denom + jnp.exp2(sink - m)
    return acc[:, :HEAD_DIM] / denom


def _attn_ctx_kernel(sink_ref, q_ref, k_ref, v_ref, o_ref, s_ref, p_ref, m_ref, *, has_sink):
    def cols(head):
        return slice(head * HEAD_DIM, (head + 1) * HEAD_DIM)

    def scores(head):
        s_ref[head] = lax.dot_general(q_ref[:, cols(head)], k_ref[head // GROUP], NT_DIMS,
                                      preferred_element_type=F32)

    def numerators(head):
        s = s_ref[head]
        m = s.max(axis=-1, keepdims=True)
        if has_sink:
            m = jnp.maximum(m, sink_ref[head])
            m_ref[head] = jnp.broadcast_to(jnp.exp2(sink_ref[head] - m), m_ref.shape[1:])
        p_ref[head] = jnp.exp2(s - m).astype(BF16)

    def outputs(head):
        acc = jnp.dot(p_ref[head], v_ref[head // GROUP], preferred_element_type=F32)
        denom = acc[:, HEAD_DIM:HEAD_DIM + 1]
        if has_sink:
            denom = denom + m_ref[head][:, :1]
        o_ref[:, cols(head)] = (acc[:, :HEAD_DIM] / denom).astype(o_ref.dtype)

    def fused_outputs(head):
        sink = sink_ref[head] if has_sink else None
        o_ref[:, cols(head)] = _softmax_pv([s_ref[head]], [v_ref[head // GROUP]], sink).astype(o_ref.dtype)

    if has_sink:
        _phased(N_HEADS, (scores, fused_outputs))
    else:
        _phased(N_HEADS, (scores, numerators, outputs))


def _attn_ctx(sink, q, kh, vh, *, n_ctx_seq, seq, has_sink):
    t, d = q.shape
    return pl.pallas_call(
        functools.partial(_attn_ctx_kernel, has_sink=has_sink),
        grid=(n_ctx_seq,),
        in_specs=[
            pl.BlockSpec(memory_space=pltpu.SMEM),
            pl.BlockSpec((seq, d), lambda b: (b, 0)),
            pl.BlockSpec((N_KV_HEADS, seq, HEAD_DIM), lambda b: (0, b, 0)),
            pl.BlockSpec((N_KV_HEADS, seq, V_WIDTH), lambda b: (0, b, 0)),
        ],
        out_specs=pl.BlockSpec((seq, d), lambda b: (b, 0)),
        out_shape=jax.ShapeDtypeStruct((n_ctx_seq * seq, d), BF16),
        scratch_shapes=[pltpu.VMEM((N_HEADS, seq, seq), F32), pltpu.VMEM((N_HEADS, seq, seq), BF16),
                        pltpu.VMEM((N_HEADS, seq, LANES), F32)],
        compiler_params=_params(("arbitrary",)),
    )(sink, q, kh, vh)


def _phased(n, stages):
    for stage in stages:
        for j in range(n):
            stage(j)


def _attn_global_kernel(q_ref, k_ref, v_ref, kc_ref, vc_ref, o_ref, sl_ref, sc_ref):
    def cols(j):
        return slice(j * HEAD_DIM, (j + 1) * HEAD_DIM)

    def scores(j):
        qj = q_ref[:, cols(j)]
        sl_ref[j] = lax.dot_general(qj, k_ref[...], NT_DIMS, preferred_element_type=F32)
        sc_ref[j] = lax.dot_general(qj, kc_ref[...], NT_DIMS, preferred_element_type=F32)

    def outputs(j):
        o = _softmax_pv([sl_ref[j], sc_ref[j]], [v_ref[...], vc_ref[...]], None)
        o_ref[:, cols(j)] = o.astype(o_ref.dtype)

    _phased(GROUP, (scores, outputs))


def _cache_spec(layer, past, width):
    return pl.BlockSpec((None, None, past, width), lambda b, g, i: (layer, g, b, 0))


def _attn_global(q, kh, vh, kc_all, vc_all, *, layer, n_ctx_tokens, n_lat_seq, lat_seq, past):
    t, d = q.shape
    gw = GROUP * HEAD_DIM
    tq = GLOBAL_Q_TILE if (lat_seq % GLOBAL_Q_TILE == 0 and n_ctx_tokens % GLOBAL_Q_TILE == 0) else Q_TILE
    q_blocks = lat_seq // tq
    q0 = n_ctx_tokens // tq
    s0 = n_ctx_tokens // lat_seq
    return pl.pallas_call(
        _attn_global_kernel,
        grid=(n_lat_seq, N_KV_HEADS, q_blocks),
        in_specs=[
            pl.BlockSpec((tq, gw), lambda b, g, i: (q0 + b * q_blocks + i, g)),
            pl.BlockSpec((None, lat_seq, HEAD_DIM), lambda b, g, i: (g, s0 + b, 0)),
            pl.BlockSpec((None, lat_seq, V_WIDTH), lambda b, g, i: (g, s0 + b, 0)),
            _cache_spec(layer, past, HEAD_DIM),
            _cache_spec(layer, past, V_WIDTH),
        ],
        out_specs=pl.BlockSpec((tq, gw), lambda b, g, i: (b * q_blocks + i, g)),
        out_shape=jax.ShapeDtypeStruct((n_lat_seq * lat_seq, d), BF16),
        scratch_shapes=[pltpu.VMEM((GROUP, tq, lat_seq), F32), pltpu.VMEM((GROUP, tq, past), F32)],
        compiler_params=_params(("arbitrary", "arbitrary", "arbitrary")),
    )(q, kh, vh, kc_all, vc_all)


def _attn_window_kernel(sink_ref, q_ref, kp_ref, kcur_ref, kn_ref, vp_ref, vcur_ref, vn_ref,
                        kc_ref, vc_ref, o_ref, sl_ref, sc_ref, *, lat_seq, tq):
    grp = pl.program_id(1)
    start = pl.program_id(2) * tq
    span = tq + 2 * WINDOW
    kw = jnp.concatenate([kp_ref[...], kcur_ref[...], kn_ref[...]], axis=0)
    vw = jnp.concatenate([vp_ref[...], vcur_ref[...], vn_ref[...]], axis=0)
    qpos = start + lax.broadcasted_iota(jnp.int32, (tq, span), 0)
    kpos = start - WINDOW + lax.broadcasted_iota(jnp.int32, (tq, span), 1)
    valid = (kpos >= 0) & (kpos < lat_seq) & (jnp.abs(qpos - kpos) <= WINDOW)

    def cols(j):
        return slice(j * HEAD_DIM, (j + 1) * HEAD_DIM)

    def scores(j):
        qj = q_ref[:, cols(j)]
        s_loc = lax.dot_general(qj, kw, NT_DIMS, preferred_element_type=F32)
        sl_ref[j] = jnp.where(valid, s_loc, NEG_INF)
        sc_ref[j] = lax.dot_general(qj, kc_ref[...], NT_DIMS, preferred_element_type=F32)

    def outputs(j):
        o = _softmax_pv([sl_ref[j], sc_ref[j]], [vw, vc_ref[...]], sink_ref[grp * GROUP + j])
        o_ref[:, cols(j)] = o.astype(o_ref.dtype)

    _phased(GROUP, (scores, outputs))


def _attn_window(sink, q, kh, vh, kc_all, vc_all, *, layer, n_ctx_tokens, n_lat_seq, lat_seq, past):
    t, d = q.shape
    gw = GROUP * HEAD_DIM
    tq = WINDOW_Q_TILE if (lat_seq % WINDOW_Q_TILE == 0 and n_ctx_tokens % WINDOW_Q_TILE == 0) else Q_TILE
    q_blocks = lat_seq // tq
    q0 = n_ctx_tokens // tq
    span = tq + 2 * WINDOW
    half = tq // WINDOW
    w0 = n_ctx_tokens // WINDOW
    w_blocks = lat_seq // WINDOW
    prev = lambda b, g, i: (g, w0 + b * w_blocks + jnp.maximum(i * half - 1, 0), 0)
    cur = lambda b, g, i: (g, q0 + b * q_blocks + i, 0)
    nxt = lambda b, g, i: (g, w0 + b * w_blocks + jnp.minimum((i + 1) * half, w_blocks - 1), 0)
    def band_specs(width):
        return [
            pl.BlockSpec((None, WINDOW, width), prev),
            pl.BlockSpec((None, tq, width), cur),
            pl.BlockSpec((None, WINDOW, width), nxt),
        ]
    return pl.pallas_call(
        functools.partial(_attn_window_kernel, lat_seq=lat_seq, tq=tq),
        grid=(n_lat_seq, N_KV_HEADS, q_blocks),
        in_specs=[
            pl.BlockSpec(memory_space=pltpu.SMEM),
            pl.BlockSpec((tq, gw), lambda b, g, i: (q0 + b * q_blocks + i, g)),
            *band_specs(HEAD_DIM), *band_specs(V_WIDTH),
            _cache_spec(layer, past, HEAD_DIM),
            _cache_spec(layer, past, V_WIDTH),
        ],
        out_specs=pl.BlockSpec((tq, gw), lambda b, g, i: (b * q_blocks + i, g)),
        out_shape=jax.ShapeDtypeStruct((n_lat_seq * lat_seq, d), BF16),
        scratch_shapes=[pltpu.VMEM((GROUP, tq, span), F32), pltpu.VMEM((GROUP, tq, past), F32)],
        compiler_params=_params(("arbitrary", "arbitrary", "arbitrary")),
    )(sink, q, kh, kh, kh, vh, vh, vh, kc_all, vc_all)


def _attn_residual(oc_ref, ol_ref, wo_ref, x_ref, gate_ref, dst_ref, n_ctx_tiles):
    def project(o_ref):
        y = jnp.dot(o_ref[...], wo_ref[...], preferred_element_type=F32)
        dst_ref[...] = x_ref[...] + gate_ref[...] * y

    @pl.when(pl.program_id(0) < n_ctx_tiles)
    def _():
        project(oc_ref)

    @pl.when(pl.program_id(0) >= n_ctx_tiles)
    def _():
        project(ol_ref)


def _attn_residual_specs(layer, til):
    tm, d, n_ctx_tiles = til.tm, til.d, til.n_ctx_tiles
    return [
        pl.BlockSpec((tm, d), lambda i, *_: (jnp.minimum(i, n_ctx_tiles - 1), 0)),
        pl.BlockSpec((tm, d), lambda i, *_: (jnp.maximum(i - n_ctx_tiles, 0), 0)),
        pl.BlockSpec((None, d, d), lambda i, *_: (layer, 0, 0), pipeline_mode=pl.Buffered(1)),
        pl.BlockSpec((tm, d), lambda i, *_: (i, 0)),
        til.mod_spec(layer, 2),
    ]


def _swiglu_act(h, wg, wu):
    g = jnp.dot(h, wg, preferred_element_type=F32)
    u = jnp.dot(h, wu, preferred_element_type=F32)
    return (g / (1.0 + jnp.exp(-g))) * u


def _ffn_kernel(oc_ref, ol_ref, wo_ref, x_ref, g1_ref, sh_ref, sc_ref, gate_ref, g_ref, wg_ref, wu_ref, wd_ref,
                out_ref, x1_ref, *, n_ctx_tiles):
    _attn_residual(oc_ref, ol_ref, wo_ref, x_ref, g1_ref, x1_ref, n_ctx_tiles)
    x = x1_ref[...]
    h = _norm_mod(x, g_ref[...], sh_ref[...], sc_ref[...]).astype(BF16)
    a = _swiglu_act(h, wg_ref[...], wu_ref[...])
    f = jnp.dot(a.astype(BF16), wd_ref[...], preferred_element_type=F32)
    out_ref[...] = x + gate_ref[...] * f


def _ffn(o_ctx, o_lat, w_o_all, x, mods, g_all, w_up_all, w_down_all, *, layer, idx, til):
    t, d = x.shape
    tm = til.tm
    d_ff = w_down_all.shape[1]
    row = lambda i: (i, 0)
    resident = pl.Buffered(1)
    return pl.pallas_call(
        functools.partial(_ffn_kernel, n_ctx_tiles=til.n_ctx_tiles),
        grid=(t // tm,),
        in_specs=[
            *_attn_residual_specs(layer, til),
            til.mod_spec(layer, 3),
            til.mod_spec(layer, 4),
            til.mod_spec(layer, 5),
            _layer_vec_spec(layer, d),
            pl.BlockSpec((None, d, d_ff), lambda i: (idx, 0, 0), pipeline_mode=resident),
            pl.BlockSpec((None, d, d_ff), lambda i: (idx, 0, 1), pipeline_mode=resident),
            pl.BlockSpec((None, d_ff, d), lambda i: (idx, 0, 0), pipeline_mode=resident),
        ],
        out_specs=pl.BlockSpec((tm, d), row),
        out_shape=jax.ShapeDtypeStruct((t, d), F32),
        scratch_shapes=[pltpu.VMEM((tm, d), F32)],
        compiler_params=_params(("arbitrary",)),
    )(o_ctx, o_lat, w_o_all, x, mods, mods, mods, mods, g_all, w_up_all, w_up_all, w_down_all)


META_IDX, META_GATE, META_RANK = 0, 2, 4


def _router_kernel(oc_ref, ol_ref, wo_ref, x_ref, g1_ref, sh_ref, sc_ref, g_ref, wr_ref, tri_ref,
                   x1_ref, meta_ref, meta_t_ref, cnt_ref, run_ref, *, n_experts, n_ctx_tiles):
    @pl.when(pl.program_id(0) == 0)
    def _():
        run_ref[...] = jnp.zeros_like(run_ref)

    _attn_residual(oc_ref, ol_ref, wo_ref, x_ref, g1_ref, x1_ref, n_ctx_tiles)
    h = _norm_mod(x1_ref[...], g_ref[...], sh_ref[...], sc_ref[...])
    logits = jnp.dot(h, wr_ref[...], preferred_element_type=F32, precision=lax.Precision.HIGHEST)
    lane = lax.broadcasted_iota(jnp.int32, logits.shape, 1).astype(F32)
    logits = jnp.where(lane < n_experts, logits, -jnp.inf)
    top1 = logits.max(axis=-1, keepdims=True)
    idx1 = jnp.where(logits == top1, lane, float(LANES)).min(axis=-1, keepdims=True)
    rest = jnp.where(lane == idx1, -jnp.inf, logits)
    top2 = rest.max(axis=-1, keepdims=True)
    idx2 = jnp.where(rest == top2, lane, float(LANES)).min(axis=-1, keepdims=True)
    e = jnp.exp(top2 - top1)
    g1 = 1.0 / (1.0 + e)
    g2 = e / (1.0 + e)

    sel1 = lane == idx1
    sel2 = lane == idx2
    sel = jnp.where(sel1, 1.0, 0.0) + jnp.where(sel2, 1.0, 0.0)
    before = jnp.dot(tri_ref[...], sel.astype(BF16), preferred_element_type=F32) + run_ref[...]
    r1 = jnp.where(sel1, before, 0.0).sum(axis=-1, keepdims=True)
    r2 = jnp.where(sel2, before, 0.0).sum(axis=-1, keepdims=True)
    run_ref[...] += sel.sum(axis=0, keepdims=True)
    cnt_ref[...] = run_ref[...]

    meta = jnp.zeros_like(logits)
    for off, (a, b) in ((META_IDX, (idx1, idx2)), (META_GATE, (g1, g2)), (META_RANK, (r1, r2))):
        meta = jnp.where(lane == off, a, meta)
        meta = jnp.where(lane == off + 1, b, meta)
    meta_ref[...] = meta
    meta_t_ref[...] = meta.T[:SUBLANES]


def _router(o_ctx, o_lat, w_o_all, x, mods, g_all, wr, tri, *, layer, n_experts, til):
    t, d = x.shape
    tm = til.tm
    row = lambda i: (i, 0)
    fixed = lambda i: (0, 0)
    return pl.pallas_call(
        functools.partial(_router_kernel, n_experts=n_experts, n_ctx_tiles=til.n_ctx_tiles),
        grid=(t // tm,),
        in_specs=[
            *_attn_residual_specs(layer, til),
            til.mod_spec(layer, 3),
            til.mod_spec(layer, 4),
            _layer_vec_spec(layer, d),
            pl.BlockSpec((d, LANES), fixed),
            pl.BlockSpec((tm, tm), fixed),
        ],
        out_specs=[pl.BlockSpec((tm, d), row), pl.BlockSpec((tm, LANES), row),
                   pl.BlockSpec((SUBLANES, tm), lambda i: (0, i)), pl.BlockSpec((1, LANES), fixed)],
        out_shape=[jax.ShapeDtypeStruct((t, d), F32), jax.ShapeDtypeStruct((t, LANES), F32),
                   jax.ShapeDtypeStruct((SUBLANES, t), F32), jax.ShapeDtypeStruct((1, LANES), F32)],
        scratch_shapes=[pltpu.VMEM((1, LANES), F32)],
        compiler_params=_params(("arbitrary",)),
    )(o_ctx, o_lat, w_o_all, x, mods, mods, mods, g_all, wr, tri)


def _row_copy(src_ref, src_row, dst_ref, dst_row, sem):
    return pltpu.make_async_copy(src_ref.at[pl.ds(src_row, 1)], dst_ref.at[pl.ds(dst_row, 1)], sem)


def _dispatch_kernel(pos_ref, x_ref, sh_ref, sc_ref, g_ref, hs_init_ref, hs_ref, h_ref, sem, *, tm, n_tokens):
    del hs_init_ref
    i = pl.program_id(0)
    slot = i % 2

    def wait_slot(s):
        for _ in range(TOP_K * tm):
            _row_copy(h_ref.at[s], 0, hs_ref, 0, sem.at[s]).wait()

    @pl.when(i >= 2)
    def _():
        wait_slot(slot)

    h_ref[slot] = _norm_mod(x_ref[...], g_ref[...], sh_ref[...], sc_ref[...])

    for r in range(tm):
        for k in range(TOP_K):
            dst = pos_ref[k * n_tokens + i * tm + r]
            _row_copy(h_ref.at[slot], r, hs_ref, dst, sem.at[slot]).start(priority=k)

    @pl.when(i == pl.num_programs(0) - 1)
    def _():
        wait_slot(slot)

        @pl.when(i >= 1)
        def _():
            wait_slot(1 - slot)


def _dispatch(pos, x, mods, g_all, hs_init, *, layer, til):
    t, d = x.shape
    tm = til.tm
    return pl.pallas_call(
        functools.partial(_dispatch_kernel, tm=tm, n_tokens=t),
        grid_spec=pltpu.PrefetchScalarGridSpec(
            num_scalar_prefetch=1,
            grid=(t // tm,),
            in_specs=[
                pl.BlockSpec((tm, d), lambda i, pos: (i, 0)),
                til.mod_spec(layer, 3),
                til.mod_spec(layer, 4),
                _layer_vec_spec(layer, d),
                pl.BlockSpec(memory_space=pl.ANY),
            ],
            out_specs=pl.BlockSpec(memory_space=pl.ANY),
            scratch_shapes=[pltpu.VMEM((2, tm, d), F32), pltpu.SemaphoreType.DMA((2,))],
        ),
        out_shape=jax.ShapeDtypeStruct(hs_init.shape, F32),
        input_output_aliases={5: 0},
        compiler_params=_params(("arbitrary",)),
    )(pos, x, mods, mods, g_all, hs_init)


def _expert_kernel(te_ref, tv_ref, hs_ref, wg_ref, wu_ref, wd_ref, ys_ref):
    del te_ref

    @pl.when(tv_ref[pl.program_id(0)] != 0)
    def _():
        a = _swiglu_act(hs_ref[...].astype(BF16), wg_ref[...], wu_ref[...])
        ys_ref[...] = jnp.dot(a.astype(BF16), wd_ref[...], preferred_element_type=F32)

    @pl.when(tv_ref[pl.program_id(0)] == 0)
    def _():
        ys_ref[...] = jnp.zeros_like(ys_ref)


def _experts(tile_expert, tile_valid, hs, w_up_all, w_down_all, *, idx, tm):
    p, d = hs.shape
    d_ff = w_down_all.shape[2]
    resident = pl.Buffered(1)
    return pl.pallas_call(
        _expert_kernel,
        grid_spec=pltpu.PrefetchScalarGridSpec(
            num_scalar_prefetch=2,
            grid=(p // tm,),
            in_specs=[
                pl.BlockSpec((tm, d), lambda r, te, tv: (r, 0)),
                pl.BlockSpec((None, None, d, d_ff), lambda r, te, tv: (idx, te[r], 0, 0), pipeline_mode=resident),
                pl.BlockSpec((None, None, d, d_ff), lambda r, te, tv: (idx, te[r], 0, 1), pipeline_mode=resident),
                pl.BlockSpec((None, None, d_ff, d), lambda r, te, tv: (idx, te[r], 0, 0), pipeline_mode=resident),
            ],
            out_specs=pl.BlockSpec((tm, d), lambda r, te, tv: (r, 0)),
        ),
        out_shape=jax.ShapeDtypeStruct((p, d), F32),
        compiler_params=_params(("arbitrary",)),
    )(tile_expert, tile_valid, hs, w_up_all, w_up_all, w_down_all)


def _combine_kernel(pos_ref, ys_ref, meta_ref, x_ref, gate_ref, *rest, tm, n_tokens, n_ctx_tiles, final):
    if final:
        gf_ref, yc_ref, yl_ref, y_ref, sem = rest
    else:
        out_ref, y_ref, sem = rest
    i = pl.program_id(0)
    slot = i % 2

    def fetch(step, s):
        for r in range(tm):
            for k in range(TOP_K):
                src = pos_ref[k * n_tokens + step * tm + r]
                _row_copy(ys_ref, src, y_ref.at[s, k], r, sem.at[s]).start(priority=k)

    @pl.when(i == 0)
    def _():
        fetch(0, 0)

    @pl.when(i + 1 < pl.num_programs(0))
    def _():
        fetch(i + 1, 1 - slot)

    for _ in range(tm):
        for k in range(TOP_K):
            _row_copy(ys_ref, 0, y_ref.at[slot, k], 0, sem.at[slot]).wait()

    meta = meta_ref[...]
    f = meta[:, META_GATE:META_GATE + 1] * y_ref[slot, 0] + meta[:, META_GATE + 1:META_GATE + 2] * y_ref[slot, 1]
    x_new = x_ref[...] + gate_ref[...] * f
    if not final:
        out_ref[...] = x_new
        return

    ms = jnp.mean(x_new * x_new, axis=-1, keepdims=True)
    y = (x_new * lax.rsqrt(ms + EPS)) * gf_ref[...]

    @pl.when(i < n_ctx_tiles)
    def _():
        yc_ref[...] = y

    @pl.when(i >= n_ctx_tiles)
    def _():
        yl_ref[...] = y


def _combine(pos, ys, meta, x, mods, final_gain, *, layer, til):
    t, d = x.shape
    tm = til.tm
    n_ctx_tiles = til.n_ctx_tiles
    final = final_gain is not None
    row = lambda i, pos: (i, 0)
    in_specs = [
        pl.BlockSpec(memory_space=pl.ANY),
        pl.BlockSpec((tm, LANES), row),
        pl.BlockSpec((tm, d), row),
        til.mod_spec(layer, 5),
    ]
    args = [pos, ys, meta, x, mods]
    if final:
        in_specs.append(pl.BlockSpec((1, d), lambda i, pos: (0, 0)))
        args.append(final_gain)
        out_specs = [pl.BlockSpec((tm, d), lambda i, pos: (jnp.minimum(i, n_ctx_tiles - 1), 0)),
                     pl.BlockSpec((tm, d), lambda i, pos: (jnp.maximum(i - n_ctx_tiles, 0), 0))]
        out_shape = [jax.ShapeDtypeStruct((n_ctx_tiles * tm, d), F32),
                     jax.ShapeDtypeStruct((t - n_ctx_tiles * tm, d), F32)]
    else:
        out_specs = pl.BlockSpec((tm, d), row)
        out_shape = jax.ShapeDtypeStruct((t, d), F32)
    return pl.pallas_call(
        functools.partial(_combine_kernel, tm=tm, n_tokens=t, n_ctx_tiles=n_ctx_tiles, final=final),
        grid_spec=pltpu.PrefetchScalarGridSpec(
            num_scalar_prefetch=1,
            grid=(t // tm,),
            in_specs=in_specs,
            out_specs=out_specs,
            scratch_shapes=[pltpu.VMEM((2, TOP_K, tm, d), F32), pltpu.SemaphoreType.DMA((2,))],
        ),
        out_shape=out_shape,
        compiler_params=_params(("arbitrary",)),
    )(*args)


def _moe(o_ctx, o_lat, w_o_all, x, mods, g_all, w_router, w_up_all, w_down_all, final_gain, sorted_buf,
         *, layer, idx, til):
    t, d = x.shape
    n_experts = w_router.shape[-1]
    tm = til.tm
    wr = jnp.zeros((d, LANES), F32).at[:, :n_experts].set(w_router)
    tri = (jnp.arange(tm)[:, None] > jnp.arange(tm)[None, :]).astype(BF16)
    x, meta, meta_t, counts = _router(o_ctx, o_lat, w_o_all, x, mods, g_all, wr, tri, layer=layer,
                                      n_experts=n_experts, til=til)

    counts = counts[0, :n_experts].astype(jnp.int32)
    padded = ((counts + tm - 1) // tm) * tm
    ends = jnp.cumsum(padded)
    starts = ends - padded
    choice = meta_t[META_IDX:META_IDX + TOP_K].astype(jnp.int32)
    rank = meta_t[META_RANK:META_RANK + TOP_K].astype(jnp.int32)
    group_start = sum(jnp.where(choice == e, starts[e], 0) for e in range(n_experts))
    pos = (group_start + rank).reshape(TOP_K * t)
    n_tiles = (t * TOP_K) // tm + n_experts
    tile_start = jnp.arange(n_tiles, dtype=jnp.int32) * tm
    tile_expert = jnp.minimum((tile_start[:, None] >= ends[None, :]).sum(axis=-1), n_experts - 1).astype(jnp.int32)
    tile_valid = (tile_start < ends[-1]).astype(jnp.int32)

    if sorted_buf is None:
        sorted_buf = jnp.zeros((n_tiles * tm, d), F32)
    hs = _dispatch(pos, x, mods, g_all, sorted_buf, layer=layer, til=til)
    ys = _experts(tile_expert, tile_valid, hs, w_up_all, w_down_all, idx=idx, tm=tm)
    return _combine(pos, ys, meta, x, mods, final_gain, layer=layer, til=til), hs


def _final_norm_kernel(x_ref, g_ref, o_ref):
    x = x_ref[...]
    ms = jnp.mean(x * x, axis=-1, keepdims=True)
    o_ref[...] = (x * lax.rsqrt(ms + EPS)) * g_ref[...]


def _final_norm(x, g, *, tm, first_tile, n_tiles):
    d = x.shape[1]
    return pl.pallas_call(
        _final_norm_kernel,
        grid=(n_tiles,),
        in_specs=[pl.BlockSpec((tm, d), lambda i: (first_tile + i, 0)), pl.BlockSpec((1, d), lambda i: (0, 0))],
        out_specs=pl.BlockSpec((tm, d), lambda i: (i, 0)),
        out_shape=jax.ShapeDtypeStruct((n_tiles * tm, d), F32),
        compiler_params=_params(("arbitrary",)),
    )(x, g)


def _rope_tables(n_ctx_tokens, n_lat_seq, lat_seq):
    pos = jnp.arange(lat_seq)
    rows = (pos // GRID_W).astype(F32)
    cols = (pos % GRID_W).astype(F32)
    pairs_per_axis = HEAD_DIM // 4
    inv = ROPE_THETA ** (-jnp.arange(pairs_per_axis, dtype=F32) / pairs_per_axis)
    ang = jnp.concatenate([rows[:, None] * inv, cols[:, None] * inv], axis=-1)
    cos = jnp.repeat(jnp.cos(ang), 2, axis=-1)
    sin = jnp.repeat(jnp.sin(ang), 2, axis=-1) * jnp.tile(jnp.array([-1.0, 1.0], F32), HEAD_DIM // 2)
    reps = LANES // HEAD_DIM
    cos = jnp.tile(cos, (n_lat_seq, reps))
    sin = jnp.tile(sin, (n_lat_seq, reps))
    cos = jnp.concatenate([jnp.ones((n_ctx_tokens, LANES), F32), cos], axis=0)
    sin = jnp.concatenate([jnp.zeros((n_ctx_tokens, LANES), F32), sin], axis=0)
    return cos, sin


def kernel(x_prompt, x_sample, cache_k, cache_v, c, c_ctx, w_ada, b_ada, norm_mix, norm_ffn, norm_final,
           w_qkv, w_o, q_norm, k_norm, sinks, w_ffn_up, w_ffn_down, w_router, w_exp_up, w_exp_down):
    n_ctx_seq, seq, d = x_prompt.shape
    n_lat_seq, lat_seq, _ = x_sample.shape
    depth = w_ada.shape[0]
    past = cache_k.shape[2]
    n_experts = w_router.shape[-1]
    kv_dim = N_KV_HEADS * HEAD_DIM
    n_ctx_tokens = n_ctx_seq * seq
    n_lat_tokens = n_lat_seq * lat_seq
    assert d == N_HEADS * HEAD_DIM and seq == Q_TILE and lat_seq % Q_TILE == 0 and kv_dim == MXU_DIM
    assert n_ctx_tokens % lat_seq == 0 and n_lat_seq < MOD_ROWS and n_experts <= LANES

    tm = 2 * Q_TILE if (lat_seq % (2 * Q_TILE) == 0 and n_ctx_tokens % (2 * Q_TILE) == 0) else Q_TILE
    til = _Tiling(tm, n_ctx_tokens, lat_seq, n_lat_seq, d)

    x = (x_prompt.reshape(n_ctx_tokens, d), x_sample.reshape(n_lat_tokens, d))

    cvecs = jnp.zeros((MOD_ROWS, d), F32).at[:n_lat_seq].set(c).at[n_lat_seq].set(c_ctx)
    mods = _ada_mods(cvecs, w_ada, b_ada).reshape(depth, MOD_ROWS * N_MOD, 1, d)

    cos, sin = _rope_tables(n_ctx_tokens, n_lat_seq, lat_seq)
    head_id = jnp.arange(MXU_DIM) // HEAD_DIM
    ones = (head_id[:, None] == head_id[None, :]).astype(BF16)

    kc_all = cache_k.transpose(1, 3, 0, 2, 4).reshape(depth, N_KV_HEADS, n_lat_seq * past, HEAD_DIM).astype(BF16)
    vc_all = cache_v.transpose(1, 3, 0, 2, 4).reshape(depth, N_KV_HEADS, n_lat_seq * past, HEAD_DIM).astype(BF16)
    ones_col = (jnp.arange(V_WIDTH - HEAD_DIM) == 0).astype(BF16)
    vc_all = jnp.concatenate([vc_all, jnp.broadcast_to(ones_col, vc_all.shape[:-1] + ones_col.shape)], axis=-1)

    w_qkv_b = w_qkv.astype(BF16)
    w_o_b = w_o.astype(BF16)
    w_ffn_up_b = w_ffn_up.astype(BF16)
    w_ffn_down_b = w_ffn_down.astype(BF16)
    w_exp_up_b = w_exp_up.astype(BF16)
    w_exp_down_b = w_exp_down.astype(BF16)
    g_mix = norm_mix.reshape(depth, 1, d)
    g_ffn = norm_ffn.reshape(depth, 1, d)
    gf = norm_final.reshape(1, d)

    lat = dict(n_ctx_tokens=n_ctx_tokens, n_lat_seq=n_lat_seq, lat_seq=lat_seq, past=past)

    k_state = jnp.zeros((n_ctx_seq, depth, seq, kv_dim), F32)
    v_state = jnp.zeros((n_ctx_seq, depth, seq, kv_dim), F32)
    sorted_buf = None
    for l in range(depth):
        idx = l // 2
        use_a = (l % 2 == 0)
        if use_a:
            qn = jnp.tile(q_norm[idx], d // HEAD_DIM).reshape(1, d)
            kn = jnp.tile(k_norm[idx], N_KV_HEADS).reshape(1, kv_dim)
            sink = jnp.zeros((N_HEADS,), F32)
        else:
            qn = jnp.ones((1, d), F32)
            kn = jnp.ones((1, kv_dim), F32)
            sink = sinks[idx].astype(F32) * LOG2E

        q, k_state, v_state, kh, vh, *joined = _qkv(x, mods, g_mix, w_qkv_b, qn, kn, cos, sin, ones, k_state,
                                                    v_state, layer=l, use_a=use_a, til=til)
        if joined:
            x = joined[0]

        o_ctx = _attn_ctx(sink, q, kh, vh, n_ctx_seq=n_ctx_seq, seq=seq, has_sink=not use_a)
        if use_a:
            o_lat = _attn_global(q, kh, vh, kc_all, vc_all, layer=l, **lat)
        else:
            o_lat = _attn_window(sink, q, kh, vh, kc_all, vc_all, layer=l, **lat)
        if use_a:
            x = _ffn(o_ctx, o_lat, w_o_b, x, mods, g_ffn, w_ffn_up_b, w_ffn_down_b, layer=l, idx=idx, til=til)
        else:
            x, sorted_buf = _moe(o_ctx, o_lat, w_o_b, x, mods, g_ffn, w_router[idx], w_exp_up_b, w_exp_down_b,
                                 gf if l == depth - 1 else None, sorted_buf, layer=l, idx=idx, til=til)

    if depth % 2 == 0:
        y_prompt, y_sample = x
    else:
        y_prompt = _final_norm(x, gf, tm=tm, first_tile=0, n_tiles=til.n_ctx_tiles)
        y_sample = _final_norm(x, gf, tm=tm, first_tile=til.n_ctx_tiles, n_tiles=n_lat_tokens // tm)
    y_prompt = y_prompt.reshape(n_ctx_seq, seq, d)
    y_sample = y_sample.reshape(n_lat_seq, lat_seq, d)
    state_shape = (n_ctx_seq, depth, seq, N_KV_HEADS, HEAD_DIM)
    return (y_prompt, y_sample, k_state.reshape(state_shape), v_state.reshape(state_shape))
```

```python
import functools

import jax
import jax.numpy as jnp
from jax import lax
from jax.experimental import pallas as pl
from jax.experimental.pallas import tpu as pltpu

N_HEADS = 16
N_KV_HEADS = 4
GROUP = N_HEADS // N_KV_HEADS
HEAD_DIM = 64
GRID_W = 64
WINDOW = 128
ROPE_THETA = 10000.0
N_MOD = 6
TOP_K = 2
EPS = 1e-6
NEG_INF = -1e30
ATTN_SCALE = HEAD_DIM ** -0.5
LOG2E = 1.4426950408889634
V_WIDTH = 2 * HEAD_DIM

LANES = 128
SUBLANES = 8
MXU_DIM = 256
MOD_ROWS = 16
Q_TILE = 256
GLOBAL_Q_TILE = 1024
WINDOW_Q_TILE = 256
VMEM_LIMIT = 56 * 1024 * 1024

F32 = jnp.float32
BF16 = jnp.bfloat16
NT_DIMS = (((1,), (1,)), ((), ()))


def _params(sem, vmem=VMEM_LIMIT):
    return pltpu.CompilerParams(dimension_semantics=sem, vmem_limit_bytes=vmem)


def _pick_tile(n, candidates):
    for c in candidates:
        if n % c == 0:
            return c
    return n


def _ada_kernel(c_ref, w_ref, b_ref, o_ref):
    c = c_ref[...]
    a = c / (1.0 + jnp.exp(-c))
    o_ref[...] = jnp.dot(a, w_ref[...], preferred_element_type=F32,
                         precision=lax.Precision.HIGHEST) + b_ref[...]


def _ada_mods(cvecs, w_ada, b_ada):
    depth, d, n = w_ada.shape
    tn = _pick_tile(n, (1536, 1024, 512, 256, 128))
    return pl.pallas_call(
        _ada_kernel,
        grid=(depth, n // tn),
        in_specs=[
            pl.BlockSpec((MOD_ROWS, d), lambda l, j: (0, 0)),
            pl.BlockSpec((None, d, tn), lambda l, j: (l, 0, j)),
            pl.BlockSpec((None, 1, tn), lambda l, j: (l, 0, j)),
        ],
        out_specs=pl.BlockSpec((None, MOD_ROWS, tn), lambda l, j: (l, 0, j)),
        out_shape=jax.ShapeDtypeStruct((depth, MOD_ROWS, n), F32),
        compiler_params=_params(("arbitrary", "arbitrary")),
    )(cvecs, w_ada, b_ada.reshape(depth, 1, n))


def _norm_mod(x, g, shift, scale):
    ms = jnp.mean(x * x, axis=-1, keepdims=True)
    y = (x * lax.rsqrt(ms + EPS)) * g
    return y * (1.0 + scale) + shift


class _Tiling:
    def __init__(self, tm, n_ctx_tokens, lat_seq, n_lat_seq, d):
        self.tm = tm
        self.n_ctx_tiles = n_ctx_tokens // tm
        self.tiles_per_seq = lat_seq // tm
        self.ctx_row = n_lat_seq
        self.d = d

    def mod_spec(self, layer, which):
        def index(i, *_):
            row = jnp.where(i < self.n_ctx_tiles, self.ctx_row, (i - self.n_ctx_tiles) // self.tiles_per_seq)
            return (layer, row * N_MOD + which, 0, 0)
        return pl.BlockSpec((None, None, 1, self.d), index)


def _layer_vec_spec(layer, d):
    return pl.BlockSpec((None, 1, d), lambda *_: (layer, 0, 0))


def _qkv_kernel(*refs, use_a, d_model, n_ctx_tiles, split_input):
    if split_input:
        (xc_ref, xl_ref, sh_ref, sc_ref, g_ref, w_ref, qn_ref, kn_ref, cos_ref, sin_ref, ones_ref, _, _,
         q_ref, kf_ref, vf_ref, kh_ref, vh_ref, x_ref, h_ref, y_ref, ss_ref) = refs
    else:
        (x_ref, sh_ref, sc_ref, g_ref, w_ref, qn_ref, kn_ref, cos_ref, sin_ref, ones_ref, _, _,
         q_ref, kf_ref, vf_ref, kh_ref, vh_ref, h_ref, y_ref, ss_ref) = refs
    q_dim = d_model
    kv_dim = N_KV_HEADS * HEAD_DIM
    is_ctx = pl.program_id(0) < n_ctx_tiles
    if split_input:
        @pl.when(is_ctx)
        def _():
            x_ref[...] = xc_ref[...]

        @pl.when(jnp.logical_not(is_ctx))
        def _():
            x_ref[...] = xl_ref[...]
    h_ref[...] = _norm_mod(x_ref[...], g_ref[...], sh_ref[...], sc_ref[...]).astype(BF16)

    cos = cos_ref[...]
    sin = sin_ref[...]
    even = (lax.broadcasted_iota(jnp.int32, cos.shape, 1) % 2) == 0

    for c0 in range(0, q_dim + 2 * kv_dim, MXU_DIM):
        y_ref[:, c0:c0 + MXU_DIM] = jnp.dot(h_ref[...], w_ref[:, c0:c0 + MXU_DIM], preferred_element_type=F32)
    if use_a:
        for c0 in range(0, q_dim + kv_dim, MXU_DIM):
            yc = y_ref[:, c0:c0 + MXU_DIM]
            ss_ref[:, c0:c0 + MXU_DIM] = jnp.dot((yc * yc).astype(BF16), ones_ref[...], preferred_element_type=F32)

    def project(col0, gain):
        xc = y_ref[:, col0:col0 + MXU_DIM]
        if use_a and gain is not None:
            ss = ss_ref[:, col0:col0 + MXU_DIM]
            xc = xc * lax.rsqrt(ss * (1.0 / HEAD_DIM) + EPS) * gain
        return xc

    def rope(xc):
        cols = []
        for l0 in range(0, MXU_DIM, LANES):
            xl = xc[:, l0:l0 + LANES]
            swapped = jnp.where(even, pltpu.roll(xl, LANES - 1, 1), pltpu.roll(xl, 1, 1))
            cols.append(xl * cos + swapped * sin)
        return jnp.concatenate(cols, axis=-1)

    for c0 in range(0, q_dim, MXU_DIM):
        qc = rope(project(c0, qn_ref[:, c0:c0 + MXU_DIM]))
        q_ref[:, c0:c0 + MXU_DIM] = (qc * (ATTN_SCALE * LOG2E)).astype(BF16)
    k = rope(project(q_dim, kn_ref[...]))
    v = project(q_dim + kv_dim, None)
    lane = lax.broadcasted_iota(jnp.int32, cos.shape, 1)
    ones_col = jnp.where(lane == HEAD_DIM, 1.0, 0.0)
    for hh in range(N_KV_HEADS):
        kh_ref[hh] = k[:, hh * HEAD_DIM:(hh + 1) * HEAD_DIM].astype(BF16)
        pair = v[:, (hh // 2) * LANES:(hh // 2 + 1) * LANES]
        if hh % 2:
            pair = pltpu.roll(pair, HEAD_DIM, 1)
        vh_ref[hh] = jnp.where(lane < HEAD_DIM, pair, ones_col).astype(BF16)

    @pl.when(is_ctx)
    def _():
        kf_ref[...] = k.reshape(kf_ref.shape)
        vf_ref[...] = v.reshape(vf_ref.shape)


def _qkv(x, mods, g_all, w_all, qn, kn, cos, sin, ones, k_state, v_state, *, layer, use_a, til):
    split_input = isinstance(x, tuple)
    seqs_per_tile = til.tm // k_state.shape[2]
    state_block = (seqs_per_tile, None) + k_state.shape[2:]
    d = til.d
    t = sum(part.shape[0] for part in x) if split_input else x.shape[0]
    tm = til.tm
    kv_dim = N_KV_HEADS * HEAD_DIM
    n_out = w_all.shape[2]
    n_ctx_tiles = til.n_ctx_tiles
    row = lambda i: (i, 0)
    fixed = lambda i: (0, 0)
    ctx_row = lambda i: (jnp.minimum(i, n_ctx_tiles - 1), 0)
    lat_row = lambda i: (jnp.maximum(i - n_ctx_tiles, 0), 0)
    state_spec = pl.BlockSpec(state_block, lambda i: (jnp.minimum(i, n_ctx_tiles - 1), layer, 0, 0))
    if split_input:
        x_specs = [pl.BlockSpec((tm, d), ctx_row), pl.BlockSpec((tm, d), lat_row)]
        x_args = list(x)
        extra_out_specs = [pl.BlockSpec((tm, d), row)]
        extra_out_shape = [jax.ShapeDtypeStruct((t, d), F32)]
    else:
        x_specs = [pl.BlockSpec((tm, d), row)]
        x_args = [x]
        extra_out_specs = []
        extra_out_shape = []
    return pl.pallas_call(
        functools.partial(_qkv_kernel, use_a=use_a, d_model=d, n_ctx_tiles=n_ctx_tiles, split_input=split_input),
        grid=(t // tm,),
        in_specs=[
            *x_specs,
            til.mod_spec(layer, 0),
            til.mod_spec(layer, 1),
            _layer_vec_spec(layer, d),
            pl.BlockSpec((None, d, n_out), lambda i: (layer, 0, 0)),
            pl.BlockSpec((1, d), fixed),
            pl.BlockSpec((1, kv_dim), fixed),
            pl.BlockSpec((tm, LANES), row),
            pl.BlockSpec((tm, LANES), row),
            pl.BlockSpec((MXU_DIM, MXU_DIM), fixed),
            pl.BlockSpec(memory_space=pl.ANY),
            pl.BlockSpec(memory_space=pl.ANY),
        ],
        out_specs=[
            pl.BlockSpec((tm, d), row),
            state_spec,
            state_spec,
            pl.BlockSpec((N_KV_HEADS, tm, HEAD_DIM), lambda i: (0, i, 0)),
            pl.BlockSpec((N_KV_HEADS, tm, V_WIDTH), lambda i: (0, i, 0)),
            *extra_out_specs,
        ],
        out_shape=[
            jax.ShapeDtypeStruct((t, d), BF16),
            jax.ShapeDtypeStruct(k_state.shape, F32),
            jax.ShapeDtypeStruct(v_state.shape, F32),
            jax.ShapeDtypeStruct((N_KV_HEADS, t, HEAD_DIM), BF16),
            jax.ShapeDtypeStruct((N_KV_HEADS, t, V_WIDTH), BF16),
            *extra_out_shape,
        ],
        scratch_shapes=[pltpu.VMEM((tm, d), BF16), pltpu.VMEM((tm, n_out), F32),
                        pltpu.VMEM((tm, d + kv_dim), F32)],
        input_output_aliases={len(x_args) + 9: 1, len(x_args) + 10: 2},
        compiler_params=_params(("arbitrary",)),
    )(*x_args, mods, mods, g_all, w_all, qn, kn, cos, sin, ones, k_state, v_state)


def _softmax_pv(score_parts, value_parts, sink):
    m = score_parts[0].max(axis=-1, keepdims=True)
    for s in score_parts[1:]:
        m = jnp.maximum(m, s.max(axis=-1, keepdims=True))
    if sink is not None:
        m = jnp.maximum(m, sink)
    acc = None
    for s, v in zip(score_parts, value_parts):
        pv = jnp.dot(jnp.exp2(s - m).astype(BF16), v, preferred_element_type=F32)
        acc = pv if acc is None else acc + pv
    denom = acc[:, HEAD_DIM:HEAD_DIM + 1]
    if sink is not None:
        denom = denom + jnp.exp2(sink - m)
    return acc[:, :HEAD_DIM] / denom


def _attn_ctx_kernel(sink_ref, q_ref, k_ref, v_ref, o_ref, s_ref, p_ref, m_ref, *, has_sink):
    def cols(head):
        return slice(head * HEAD_DIM, (head + 1) * HEAD_DIM)

    def scores(head):
        s_ref[head] = lax.dot_general(q_ref[:, cols(head)], k_ref[head // GROUP], NT_DIMS,
                                      preferred_element_type=F32)

    def numerators(head):
        s = s_ref[head]
        m = s.max(axis=-1, keepdims=True)
        if has_sink:
            m = jnp.maximum(m, sink_ref[head])
            m_ref[head] = jnp.broadcast_to(jnp.exp2(sink_ref[head] - m), m_ref.shape[1:])
        p_ref[head] = jnp.exp2(s - m).astype(BF16)

    def outputs(head):
        acc = jnp.dot(p_ref[head], v_ref[head // GROUP], preferred_element_type=F32)
        denom = acc[:, HEAD_DIM:HEAD_DIM + 1]
        if has_sink:
            denom = denom + m_ref[head][:, :1]
        o_ref[:, cols(head)] = (acc[:, :HEAD_DIM] / denom).astype(o_ref.dtype)

    def fused_outputs(head):
        sink = sink_ref[head] if has_sink else None
        o_ref[:, cols(head)] = _softmax_pv([s_ref[head]], [v_ref[head // GROUP]], sink).astype(o_ref.dtype)

    if has_sink:
        _phased(N_HEADS, (scores, fused_outputs))
    else:
        _phased(N_HEADS, (scores, numerators, outputs))


def _attn_ctx(sink, q, kh, vh, *, n_ctx_seq, seq, has_sink):
    t, d = q.shape
    return pl.pallas_call(
        functools.partial(_attn_ctx_kernel, has_sink=has_sink),
        grid=(n_ctx_seq,),
        in_specs=[
            pl.BlockSpec(memory_space=pltpu.SMEM),
            pl.BlockSpec((seq, d), lambda b: (b, 0)),
            pl.BlockSpec((N_KV_HEADS, seq, HEAD_DIM), lambda b: (0, b, 0)),
            pl.BlockSpec((N_KV_HEADS, seq, V_WIDTH), lambda b: (0, b, 0)),
        ],
        out_specs=pl.BlockSpec((seq, d), lambda b: (b, 0)),
        out_shape=jax.ShapeDtypeStruct((n_ctx_seq * seq, d), BF16),
        scratch_shapes=[pltpu.VMEM((N_HEADS, seq, seq), F32), pltpu.VMEM((N_HEADS, seq, seq), BF16),
                        pltpu.VMEM((N_HEADS, seq, LANES), F32)],
        compiler_params=_params(("arbitrary",)),
    )(sink, q, kh, vh)


def _phased(n, stages):
    for stage in stages:
        for j in range(n):
            stage(j)


def _attn_global_kernel(q_ref, k_ref, v_ref, kc_ref, vc_ref, o_ref, sl_ref, sc_ref):
    def cols(j):
        return slice(j * HEAD_DIM, (j + 1) * HEAD_DIM)

    def scores(j):
        qj = q_ref[:, cols(j)]
        sl_ref[j] = lax.dot_general(qj, k_ref[...], NT_DIMS, preferred_element_type=F32)
        sc_ref[j] = lax.dot_general(qj, kc_ref[...], NT_DIMS, preferred_element_type=F32)

    def outputs(j):
        o = _softmax_pv([sl_ref[j], sc_ref[j]], [v_ref[...], vc_ref[...]], None)
        o_ref[:, cols(j)] = o.astype(o_ref.dtype)

    _phased(GROUP, (scores, outputs))


def _cache_spec(layer, past, width):
    return pl.BlockSpec((None, None, past, width), lambda b, g, i: (layer, g, b, 0))


def _attn_global(q, kh, vh, kc_all, vc_all, *, layer, n_ctx_tokens, n_lat_seq, lat_seq, past):
    t, d = q.shape
    gw = GROUP * HEAD_DIM
    tq = GLOBAL_Q_TILE if (lat_seq % GLOBAL_Q_TILE == 0 and n_ctx_tokens % GLOBAL_Q_TILE == 0) else Q_TILE
    q_blocks = lat_seq // tq
    q0 = n_ctx_tokens // tq
    s0 = n_ctx_tokens // lat_seq
    return pl.pallas_call(
        _attn_global_kernel,
        grid=(n_lat_seq, N_KV_HEADS, q_blocks),
        in_specs=[
            pl.BlockSpec((tq, gw), lambda b, g, i: (q0 + b * q_blocks + i, g)),
            pl.BlockSpec((None, lat_seq, HEAD_DIM), lambda b, g, i: (g, s0 + b, 0)),
            pl.BlockSpec((None, lat_seq, V_WIDTH), lambda b, g, i: (g, s0 + b, 0)),
            _cache_spec(layer, past, HEAD_DIM),
            _cache_spec(layer, past, V_WIDTH),
        ],
        out_specs=pl.BlockSpec((tq, gw), lambda b, g, i: (b * q_blocks + i, g)),
        out_shape=jax.ShapeDtypeStruct((n_lat_seq * lat_seq, d), BF16),
        scratch_shapes=[pltpu.VMEM((GROUP, tq, lat_seq), F32), pltpu.VMEM((GROUP, tq, past), F32)],
        compiler_params=_params(("arbitrary", "arbitrary", "arbitrary")),
    )(q, kh, vh, kc_all, vc_all)


def _attn_window_kernel(sink_ref, q_ref, kp_ref, kcur_ref, kn_ref, vp_ref, vcur_ref, vn_ref,
                        kc_ref, vc_ref, o_ref, sl_ref, sc_ref, *, lat_seq, tq):
    start = pl.program_id(1) * tq
    span = tq + 2 * WINDOW
    qpos = start + lax.broadcasted_iota(jnp.int32, (tq, span), 0)
    kpos = start - WINDOW + lax.broadcasted_iota(jnp.int32, (tq, span), 1)
    valid = (kpos >= 0) & (kpos < lat_seq) & (jnp.abs(qpos - kpos) <= WINDOW)

    def cols(head):
        return slice(head * HEAD_DIM, (head + 1) * HEAD_DIM)

    def band(prev_ref, cur_ref, next_ref, g):
        return jnp.concatenate([prev_ref[g], cur_ref[g], next_ref[g]], axis=0)

    def scores(head):
        g = head // GROUP
        qj = q_ref[:, cols(head)]
        s_loc = lax.dot_general(qj, band(kp_ref, kcur_ref, kn_ref, g), NT_DIMS, preferred_element_type=F32)
        sl_ref[head] = jnp.where(valid, s_loc, NEG_INF)
        sc_ref[head] = lax.dot_general(qj, kc_ref[g], NT_DIMS, preferred_element_type=F32)

    def outputs(head):
        g = head // GROUP
        o = _softmax_pv([sl_ref[head], sc_ref[head]], [band(vp_ref, vcur_ref, vn_ref, g), vc_ref[g]],
                        sink_ref[head])
        o_ref[:, cols(head)] = o.astype(o_ref.dtype)

    _phased(N_HEADS, (scores, outputs))


def _attn_window(sink, q, kh, vh, kc_all, vc_all, *, layer, n_ctx_tokens, n_lat_seq, lat_seq, past):
    t, d = q.shape
    gw = GROUP * HEAD_DIM
    tq = WINDOW_Q_TILE if (lat_seq % WINDOW_Q_TILE == 0 and n_ctx_tokens % WINDOW_Q_TILE == 0) else Q_TILE
    q_blocks = lat_seq // tq
    q0 = n_ctx_tokens // tq
    span = tq + 2 * WINDOW
    half = tq // WINDOW
    w0 = n_ctx_tokens // WINDOW
    w_blocks = lat_seq // WINDOW
    prev = lambda b, i: (0, w0 + b * w_blocks + jnp.maximum(i * half - 1, 0), 0)
    cur = lambda b, i: (0, q0 + b * q_blocks + i, 0)
    nxt = lambda b, i: (0, w0 + b * w_blocks + jnp.minimum((i + 1) * half, w_blocks - 1), 0)

    def band_specs(width):
        return [
            pl.BlockSpec((N_KV_HEADS, WINDOW, width), prev),
            pl.BlockSpec((N_KV_HEADS, tq, width), cur),
            pl.BlockSpec((N_KV_HEADS, WINDOW, width), nxt),
        ]

    def cache_spec(width):
        return pl.BlockSpec((None, N_KV_HEADS, past, width), lambda b, i: (layer, 0, b, 0))

    return pl.pallas_call(
        functools.partial(_attn_window_kernel, lat_seq=lat_seq, tq=tq),
        grid=(n_lat_seq, q_blocks),
        in_specs=[
            pl.BlockSpec(memory_space=pltpu.SMEM),
            pl.BlockSpec((tq, d), lambda b, i: (q0 + b * q_blocks + i, 0)),
            *band_specs(HEAD_DIM), *band_specs(V_WIDTH),
            cache_spec(HEAD_DIM),
            cache_spec(V_WIDTH),
        ],
        out_specs=pl.BlockSpec((tq, d), lambda b, i: (b * q_blocks + i, 0)),
        out_shape=jax.ShapeDtypeStruct((n_lat_seq * lat_seq, d), BF16),
        scratch_shapes=[pltpu.VMEM((N_HEADS, tq, span), F32), pltpu.VMEM((N_HEADS, tq, past), F32)],
        compiler_params=_params(("arbitrary", "arbitrary")),
    )(sink, q, kh, kh, kh, vh, vh, vh, kc_all, vc_all)


def _attn_residual(oc_ref, ol_ref, wo_ref, x_ref, gate_ref, dst_ref, n_ctx_tiles):
    def project(o_ref):
        y = jnp.dot(o_ref[...], wo_ref[...], preferred_element_type=F32)
        dst_ref[...] = x_ref[...] + gate_ref[...] * y

    @pl.when(pl.program_id(0) < n_ctx_tiles)
    def _():
        project(oc_ref)

    @pl.when(pl.program_id(0) >= n_ctx_tiles)
    def _():
        project(ol_ref)


def _attn_residual_specs(layer, til):
    tm, d, n_ctx_tiles = til.tm, til.d, til.n_ctx_tiles
    return [
        pl.BlockSpec((tm, d), lambda i, *_: (jnp.minimum(i, n_ctx_tiles - 1), 0)),
        pl.BlockSpec((tm, d), lambda i, *_: (jnp.maximum(i - n_ctx_tiles, 0), 0)),
        pl.BlockSpec((None, d, d), lambda i, *_: (layer, 0, 0), pipeline_mode=pl.Buffered(1)),
        pl.BlockSpec((tm, d), lambda i, *_: (i, 0)),
        til.mod_spec(layer, 2),
    ]


def _swiglu_act(h, wg, wu):
    g = jnp.dot(h, wg, preferred_element_type=F32)
    u = jnp.dot(h, wu, preferred_element_type=F32)
    return (g / (1.0 + jnp.exp(-g))) * u


def _ffn_kernel(oc_ref, ol_ref, wo_ref, x_ref, g1_ref, sh_ref, sc_ref, gate_ref, g_ref, wg_ref, wu_ref, wd_ref,
                out_ref, x1_ref, *, n_ctx_tiles):
    _attn_residual(oc_ref, ol_ref, wo_ref, x_ref, g1_ref, x1_ref, n_ctx_tiles)
    x = x1_ref[...]
    h = _norm_mod(x, g_ref[...], sh_ref[...], sc_ref[...]).astype(BF16)
    a = _swiglu_act(h, wg_ref[...], wu_ref[...])
    f = jnp.dot(a.astype(BF16), wd_ref[...], preferred_element_type=F32)
    out_ref[...] = x + gate_ref[...] * f


def _ffn(o_ctx, o_lat, w_o_all, x, mods, g_all, w_up_all, w_down_all, *, layer, idx, til):
    t, d = x.shape
    tm = til.tm
    d_ff = w_down_all.shape[1]
    row = lambda i: (i, 0)
    resident = pl.Buffered(1)
    return pl.pallas_call(
        functools.partial(_ffn_kernel, n_ctx_tiles=til.n_ctx_tiles),
        grid=(t // tm,),
        in_specs=[
            *_attn_residual_specs(layer, til),
            til.mod_spec(layer, 3),
            til.mod_spec(layer, 4),
            til.mod_spec(layer, 5),
            _layer_vec_spec(layer, d),
            pl.BlockSpec((None, d, d_ff), lambda i: (idx, 0, 0), pipeline_mode=resident),
            pl.BlockSpec((None, d, d_ff), lambda i: (idx, 0, 1), pipeline_mode=resident),
            pl.BlockSpec((None, d_ff, d), lambda i: (idx, 0, 0), pipeline_mode=resident),
        ],
        out_specs=pl.BlockSpec((tm, d), row),
        out_shape=jax.ShapeDtypeStruct((t, d), F32),
        scratch_shapes=[pltpu.VMEM((tm, d), F32)],
        compiler_params=_params(("arbitrary",)),
    )(o_ctx, o_lat, w_o_all, x, mods, mods, mods, mods, g_all, w_up_all, w_up_all, w_down_all)


META_IDX, META_GATE, META_RANK = 0, 2, 4


def _router_kernel(oc_ref, ol_ref, wo_ref, x_ref, g1_ref, sh_ref, sc_ref, g_ref, wr_ref, tri_ref,
                   x1_ref, meta_ref, meta_t_ref, cnt_ref, run_ref, *, n_experts, n_ctx_tiles):
    @pl.when(pl.program_id(0) == 0)
    def _():
        run_ref[...] = jnp.zeros_like(run_ref)

    _attn_residual(oc_ref, ol_ref, wo_ref, x_ref, g1_ref, x1_ref, n_ctx_tiles)
    h = _norm_mod(x1_ref[...], g_ref[...], sh_ref[...], sc_ref[...])
    logits = jnp.dot(h, wr_ref[...], preferred_element_type=F32, precision=lax.Precision.HIGHEST)
    lane = lax.broadcasted_iota(jnp.int32, logits.shape, 1).astype(F32)
    logits = jnp.where(lane < n_experts, logits, -jnp.inf)
    top1 = logits.max(axis=-1, keepdims=True)
    idx1 = jnp.where(logits == top1, lane, float(LANES)).min(axis=-1, keepdims=True)
    rest = jnp.where(lane == idx1, -jnp.inf, logits)
    top2 = rest.max(axis=-1, keepdims=True)
    idx2 = jnp.where(rest == top2, lane, float(LANES)).min(axis=-1, keepdims=True)
    e = jnp.exp(top2 - top1)
    g1 = 1.0 / (1.0 + e)
    g2 = e / (1.0 + e)

    sel1 = lane == idx1
    sel2 = lane == idx2
    sel = jnp.where(sel1, 1.0, 0.0) + jnp.where(sel2, 1.0, 0.0)
    before = jnp.dot(tri_ref[...], sel.astype(BF16), preferred_element_type=F32) + run_ref[...]
    r1 = jnp.where(sel1, before, 0.0).sum(axis=-1, keepdims=True)
    r2 = jnp.where(sel2, before, 0.0).sum(axis=-1, keepdims=True)
    run_ref[...] += sel.sum(axis=0, keepdims=True)
    cnt_ref[...] = run_ref[...]

    meta = jnp.zeros_like(logits)
    for off, (a, b) in ((META_IDX, (idx1, idx2)), (META_GATE, (g1, g2)), (META_RANK, (r1, r2))):
        meta = jnp.where(lane == off, a, meta)
        meta = jnp.where(lane == off + 1, b, meta)
    meta_ref[...] = meta
    meta_t_ref[...] = meta.T[:SUBLANES]


def _router(o_ctx, o_lat, w_o_all, x, mods, g_all, wr, tri, *, layer, n_experts, til):
    t, d = x.shape
    tm = til.tm
    row = lambda i: (i, 0)
    fixed = lambda i: (0, 0)
    return pl.pallas_call(
        functools.partial(_router_kernel, n_experts=n_experts, n_ctx_tiles=til.n_ctx_tiles),
        grid=(t // tm,),
        in_specs=[
            *_attn_residual_specs(layer, til),
            til.mod_spec(layer, 3),
            til.mod_spec(layer, 4),
            _layer_vec_spec(layer, d),
            pl.BlockSpec((d, LANES), fixed),
            pl.BlockSpec((tm, tm), fixed),
        ],
        out_specs=[pl.BlockSpec((tm, d), row), pl.BlockSpec((tm, LANES), row),
                   pl.BlockSpec((SUBLANES, tm), lambda i: (0, i)), pl.BlockSpec((1, LANES), fixed)],
        out_shape=[jax.ShapeDtypeStruct((t, d), F32), jax.ShapeDtypeStruct((t, LANES), F32),
                   jax.ShapeDtypeStruct((SUBLANES, t), F32), jax.ShapeDtypeStruct((1, LANES), F32)],
        scratch_shapes=[pltpu.VMEM((1, LANES), F32)],
        compiler_params=_params(("arbitrary",)),
    )(o_ctx, o_lat, w_o_all, x, mods, mods, mods, g_all, wr, tri)


def _row_copy(src_ref, src_row, dst_ref, dst_row, sem):
    return pltpu.make_async_copy(src_ref.at[pl.ds(src_row, 1)], dst_ref.at[pl.ds(dst_row, 1)], sem)


def _dispatch_kernel(pos_ref, x_ref, sh_ref, sc_ref, g_ref, hs_init_ref, hs_ref, h_ref, sem, *, tm, n_tokens):
    del hs_init_ref
    i = pl.program_id(0)
    slot = i % 2

    def wait_slot(s):
        for _ in range(TOP_K * tm):
            _row_copy(h_ref.at[s], 0, hs_ref, 0, sem.at[s]).wait()

    @pl.when(i >= 2)
    def _():
        wait_slot(slot)

    h_ref[slot] = _norm_mod(x_ref[...], g_ref[...], sh_ref[...], sc_ref[...])

    for r in range(tm):
        for k in range(TOP_K):
            dst = pos_ref[k * n_tokens + i * tm + r]
            _row_copy(h_ref.at[slot], r, hs_ref, dst, sem.at[slot]).start(priority=k)

    @pl.when(i == pl.num_programs(0) - 1)
    def _():
        wait_slot(slot)

        @pl.when(i >= 1)
        def _():
            wait_slot(1 - slot)


def _dispatch(pos, x, mods, g_all, hs_init, *, layer, til):
    t, d = x.shape
    tm = til.tm
    return pl.pallas_call(
        functools.partial(_dispatch_kernel, tm=tm, n_tokens=t),
        grid_spec=pltpu.PrefetchScalarGridSpec(
            num_scalar_prefetch=1,
            grid=(t // tm,),
            in_specs=[
                pl.BlockSpec((tm, d), lambda i, pos: (i, 0)),
                til.mod_spec(layer, 3),
                til.mod_spec(layer, 4),
                _layer_vec_spec(layer, d),
                pl.BlockSpec(memory_space=pl.ANY),
            ],
            out_specs=pl.BlockSpec(memory_space=pl.ANY),
            scratch_shapes=[pltpu.VMEM((2, tm, d), F32), pltpu.SemaphoreType.DMA((2,))],
        ),
        out_shape=jax.ShapeDtypeStruct(hs_init.shape, F32),
        input_output_aliases={5: 0},
        compiler_params=_params(("arbitrary",)),
    )(pos, x, mods, mods, g_all, hs_init)


def _expert_kernel(te_ref, tv_ref, hs_ref, wg_ref, wu_ref, wd_ref, ys_ref):
    del te_ref

    @pl.when(tv_ref[pl.program_id(0)] != 0)
    def _():
        a = _swiglu_act(hs_ref[...].astype(BF16), wg_ref[...], wu_ref[...])
        ys_ref[...] = jnp.dot(a.astype(BF16), wd_ref[...], preferred_element_type=F32)

    @pl.when(tv_ref[pl.program_id(0)] == 0)
    def _():
        ys_ref[...] = jnp.zeros_like(ys_ref)


def _experts(tile_expert, tile_valid, hs, w_up_all, w_down_all, *, idx, tm):
    p, d = hs.shape
    d_ff = w_down_all.shape[2]
    resident = pl.Buffered(1)
    return pl.pallas_call(
        _expert_kernel,
        grid_spec=pltpu.PrefetchScalarGridSpec(
            num_scalar_prefetch=2,
            grid=(p // tm,),
            in_specs=[
                pl.BlockSpec((tm, d), lambda r, te, tv: (r, 0)),
                pl.BlockSpec((None, None, d, d_ff), lambda r, te, tv: (idx, te[r], 0, 0), pipeline_mode=resident),
                pl.BlockSpec((None, None, d, d_ff), lambda r, te, tv: (idx, te[r], 0, 1), pipeline_mode=resident),
                pl.BlockSpec((None, None, d_ff, d), lambda r, te, tv: (idx, te[r], 0, 0), pipeline_mode=resident),
            ],
            out_specs=pl.BlockSpec((tm, d), lambda r, te, tv: (r, 0)),
        ),
        out_shape=jax.ShapeDtypeStruct((p, d), F32),
        compiler_params=_params(("arbitrary",)),
    )(tile_expert, tile_valid, hs, w_up_all, w_up_all, w_down_all)


def _combine_kernel(pos_ref, ys_ref, meta_ref, x_ref, gate_ref, *rest, tm, n_tokens, n_ctx_tiles, final):
    if final:
        gf_ref, yc_ref, yl_ref, y_ref, sem = rest
    else:
        out_ref, y_ref, sem = rest
    i = pl.program_id(0)
    slot = i % 2

    def fetch(step, s):
        for r in range(tm):
            for k in range(TOP_K):
                src = pos_ref[k * n_tokens + step * tm + r]
                _row_copy(ys_ref, src, y_ref.at[s, k], r, sem.at[s]).start(priority=k)

    @pl.when(i == 0)
    def _():
        fetch(0, 0)

    @pl.when(i + 1 < pl.num_programs(0))
    def _():
        fetch(i + 1, 1 - slot)

    for _ in range(tm):
        for k in range(TOP_K):
            _row_copy(ys_ref, 0, y_ref.at[slot, k], 0, sem.at[slot]).wait()

    meta = meta_ref[...]
    f = meta[:, META_GATE:META_GATE + 1] * y_ref[slot, 0] + meta[:, META_GATE + 1:META_GATE + 2] * y_ref[slot, 1]
    x_new = x_ref[...] + gate_ref[...] * f
    if not final:
        out_ref[...] = x_new
        return

    ms = jnp.mean(x_new * x_new, axis=-1, keepdims=True)
    y = (x_new * lax.rsqrt(ms + EPS)) * gf_ref[...]

    @pl.when(i < n_ctx_tiles)
    def _():
        yc_ref[...] = y

    @pl.when(i >= n_ctx_tiles)
    def _():
        yl_ref[...] = y


def _combine(pos, ys, meta, x, mods, final_gain, *, layer, til):
    t, d = x.shape
    tm = til.tm
    n_ctx_tiles = til.n_ctx_tiles
    final = final_gain is not None
    row = lambda i, pos: (i, 0)
    in_specs = [
        pl.BlockSpec(memory_space=pl.ANY),
        pl.BlockSpec((tm, LANES), row),
        pl.BlockSpec((tm, d), row),
        til.mod_spec(layer, 5),
    ]
    args = [pos, ys, meta, x, mods]
    if final:
        in_specs.append(pl.BlockSpec((1, d), lambda i, pos: (0, 0)))
        args.append(final_gain)
        out_specs = [pl.BlockSpec((tm, d), lambda i, pos: (jnp.minimum(i, n_ctx_tiles - 1), 0)),
                     pl.BlockSpec((tm, d), lambda i, pos: (jnp.maximum(i - n_ctx_tiles, 0), 0))]
        out_shape = [jax.ShapeDtypeStruct((n_ctx_tiles * tm, d), F32),
                     jax.ShapeDtypeStruct((t - n_ctx_tiles * tm, d), F32)]
    else:
        out_specs = pl.BlockSpec((tm, d), row)
        out_shape = jax.ShapeDtypeStruct((t, d), F32)
    return pl.pallas_call(
        functools.partial(_combine_kernel, tm=tm, n_tokens=t, n_ctx_tiles=n_ctx_tiles, final=final),
        grid_spec=pltpu.PrefetchScalarGridSpec(
            num_scalar_prefetch=1,
            grid=(t // tm,),
            in_specs=in_specs,
            out_specs=out_specs,
            scratch_shapes=[pltpu.VMEM((2, TOP_K, tm, d), F32), pltpu.SemaphoreType.DMA((2,))],
        ),
        out_shape=out_shape,
        compiler_params=_params(("arbitrary",)),
    )(*args)


def _moe(o_ctx, o_lat, w_o_all, x, mods, g_all, w_router, w_up_all, w_down_all, final_gain, sorted_buf,
         *, layer, idx, til):
    t, d = x.shape
    n_experts = w_router.shape[-1]
    tm = til.tm
    wr = jnp.zeros((d, LANES), F32).at[:, :n_experts].set(w_router)
    tri = (jnp.arange(tm)[:, None] > jnp.arange(tm)[None, :]).astype(BF16)
    x, meta, meta_t, counts = _router(o_ctx, o_lat, w_o_all, x, mods, g_all, wr, tri, layer=layer,
                                      n_experts=n_experts, til=til)

    counts = counts[0, :n_experts].astype(jnp.int32)
    padded = ((counts + tm - 1) // tm) * tm
    ends = jnp.cumsum(padded)
    starts = ends - padded
    choice = meta_t[META_IDX:META_IDX + TOP_K].astype(jnp.int32)
    rank = meta_t[META_RANK:META_RANK + TOP_K].astype(jnp.int32)
    group_start = sum(jnp.where(choice == e, starts[e], 0) for e in range(n_experts))
    pos = (group_start + rank).reshape(TOP_K * t)
    n_tiles = (t * TOP_K) // tm + n_experts
    tile_start = jnp.arange(n_tiles, dtype=jnp.int32) * tm
    tile_expert = jnp.minimum((tile_start[:, None] >= ends[None, :]).sum(axis=-1), n_experts - 1).astype(jnp.int32)
    tile_valid = (tile_start < ends[-1]).astype(jnp.int32)

    if sorted_buf is None:
        sorted_buf = jnp.zeros((n_tiles * tm, d), F32)
    hs = _dispatch(pos, x, mods, g_all, sorted_buf, layer=layer, til=til)
    ys = _experts(tile_expert, tile_valid, hs, w_up_all, w_down_all, idx=idx, tm=tm)
    return _combine(pos, ys, meta, x, mods, final_gain, layer=layer, til=til), hs


def _final_norm_kernel(x_ref, g_ref, o_ref):
    x = x_ref[...]
    ms = jnp.mean(x * x, axis=-1, keepdims=True)
    o_ref[...] = (x * lax.rsqrt(ms + EPS)) * g_ref[...]


def _final_norm(x, g, *, tm, first_tile, n_tiles):
    d = x.shape[1]
    return pl.pallas_call(
        _final_norm_kernel,
        grid=(n_tiles,),
        in_specs=[pl.BlockSpec((tm, d), lambda i: (first_tile + i, 0)), pl.BlockSpec((1, d), lambda i: (0, 0))],
        out_specs=pl.BlockSpec((tm, d), lambda i: (i, 0)),
        out_shape=jax.ShapeDtypeStruct((n_tiles * tm, d), F32),
        compiler_params=_params(("arbitrary",)),
    )(x, g)


def _rope_tables(n_ctx_tokens, n_lat_seq, lat_seq):
    pos = jnp.arange(lat_seq)
    rows = (pos // GRID_W).astype(F32)
    cols = (pos % GRID_W).astype(F32)
    pairs_per_axis = HEAD_DIM // 4
    inv = ROPE_THETA ** (-jnp.arange(pairs_per_axis, dtype=F32) / pairs_per_axis)
    ang = jnp.concatenate([rows[:, None] * inv, cols[:, None] * inv], axis=-1)
    cos = jnp.repeat(jnp.cos(ang), 2, axis=-1)
    sin = jnp.repeat(jnp.sin(ang), 2, axis=-1) * jnp.tile(jnp.array([-1.0, 1.0], F32), HEAD_DIM // 2)
    reps = LANES // HEAD_DIM
    cos = jnp.tile(cos, (n_lat_seq, reps))
    sin = jnp.tile(sin, (n_lat_seq, reps))
    cos = jnp.concatenate([jnp.ones((n_ctx_tokens, LANES), F32), cos], axis=0)
    sin = jnp.concatenate([jnp.zeros((n_ctx_tokens, LANES), F32), sin], axis=0)
    return cos, sin


def kernel(x_prompt, x_sample, cache_k, cache_v, c, c_ctx, w_ada, b_ada, norm_mix, norm_ffn, norm_final,
           w_qkv, w_o, q_norm, k_norm, sinks, w_ffn_up, w_ffn_down, w_router, w_exp_up, w_exp_down):
    n_ctx_seq, seq, d = x_prompt.shape
    n_lat_seq, lat_seq, _ = x_sample.shape
    depth = w_ada.shape[0]
    past = cache_k.shape[2]
    n_experts = w_router.shape[-1]
    kv_dim = N_KV_HEADS * HEAD_DIM
    n_ctx_tokens = n_ctx_seq * seq
    n_lat_tokens = n_lat_seq * lat_seq
    assert d == N_HEADS * HEAD_DIM and seq == Q_TILE and lat_seq % Q_TILE == 0 and kv_dim == MXU_DIM
    assert n_ctx_tokens % lat_seq == 0 and n_lat_seq < MOD_ROWS and n_experts <= LANES

    tm = 2 * Q_TILE if (lat_seq % (2 * Q_TILE) == 0 and n_ctx_tokens % (2 * Q_TILE) == 0) else Q_TILE
    til = _Tiling(tm, n_ctx_tokens, lat_seq, n_lat_seq, d)

    x = (x_prompt.reshape(n_ctx_tokens, d), x_sample.reshape(n_lat_tokens, d))

    cvecs = jnp.zeros((MOD_ROWS, d), F32).at[:n_lat_seq].set(c).at[n_lat_seq].set(c_ctx)
    mods = _ada_mods(cvecs, w_ada, b_ada).reshape(depth, MOD_ROWS * N_MOD, 1, d)

    cos, sin = _rope_tables(n_ctx_tokens, n_lat_seq, lat_seq)
    head_id = jnp.arange(MXU_DIM) // HEAD_DIM
    ones = (head_id[:, None] == head_id[None, :]).astype(BF16)

    kc_all = cache_k.transpose(1, 3, 0, 2, 4).reshape(depth, N_KV_HEADS, n_lat_seq * past, HEAD_DIM).astype(BF16)
    vc_all = cache_v.transpose(1, 3, 0, 2, 4).reshape(depth, N_KV_HEADS, n_lat_seq * past, HEAD_DIM).astype(BF16)
    ones_col = (jnp.arange(V_WIDTH - HEAD_DIM) == 0).astype(BF16)
    vc_all = jnp.concatenate([vc_all, jnp.broadcast_to(ones_col, vc_all.shape[:-1] + ones_col.shape)], axis=-1)

    w_qkv_b = w_qkv.astype(BF16)
    w_o_b = w_o.astype(BF16)
    w_ffn_up_b = w_ffn_up.astype(BF16)
    w_ffn_down_b = w_ffn_down.astype(BF16)
    w_exp_up_b = w_exp_up.astype(BF16)
    w_exp_down_b = w_exp_down.astype(BF16)
    g_mix = norm_mix.reshape(depth, 1, d)
    g_ffn = norm_ffn.reshape(depth, 1, d)
    gf = norm_final.reshape(1, d)

    lat = dict(n_ctx_tokens=n_ctx_tokens, n_lat_seq=n_lat_seq, lat_seq=lat_seq, past=past)

    k_state = jnp.zeros((n_ctx_seq, depth, seq, kv_dim), F32)
    v_state = jnp.zeros((n_ctx_seq, depth, seq, kv_dim), F32)
    sorted_buf = None
    for l in range(depth):
        idx = l // 2
        use_a = (l % 2 == 0)
        if use_a:
            qn = jnp.tile(q_norm[idx], d // HEAD_DIM).reshape(1, d)
            kn = jnp.tile(k_norm[idx], N_KV_HEADS).reshape(1, kv_dim)
            sink = jnp.zeros((N_HEADS,), F32)
        else:
            qn = jnp.ones((1, d), F32)
            kn = jnp.ones((1, kv_dim), F32)
            sink = sinks[idx].astype(F32) * LOG2E

        q, k_state, v_state, kh, vh, *joined = _qkv(x, mods, g_mix, w_qkv_b, qn, kn, cos, sin, ones, k_state,
                                                    v_state, layer=l, use_a=use_a, til=til)
        if joined:
            x = joined[0]

        o_ctx = _attn_ctx(sink, q, kh, vh, n_ctx_seq=n_ctx_seq, seq=seq, has_sink=not use_a)
        if use_a:
            o_lat = _attn_global(q, kh, vh, kc_all, vc_all, layer=l, **lat)
        else:
            o_lat = _attn_window(sink, q, kh, vh, kc_all, vc_all, layer=l, **lat)
        if use_a:
            x = _ffn(o_ctx, o_lat, w_o_b, x, mods, g_ffn, w_ffn_up_b, w_ffn_down_b, layer=l, idx=idx, til=til)
        else:
            x, sorted_buf = _moe(o_ctx, o_lat, w_o_b, x, mods, g_ffn, w_router[idx], w_exp_up_b, w_exp_down_b,
                                 gf if l == depth - 1 else None, sorted_buf, layer=l, idx=idx, til=til)

    if depth % 2 == 0:
        y_prompt, y_sample = x
    else:
        y_prompt = _final_norm(x, gf, tm=tm, first_tile=0, n_tiles=til.n_ctx_tiles)
        y_sample = _final_norm(x, gf, tm=tm, first_tile=til.n_ctx_tiles, n_tiles=n_lat_tokens // tm)
    y_prompt = y_prompt.reshape(n_ctx_seq, seq, d)
    y_sample = y_sample.reshape(n_lat_seq, lat_seq, d)
    state_shape = (n_ctx_seq, depth, seq, N_KV_HEADS, HEAD_DIM)
    return (y_prompt, y_sample, k_state.reshape(state_shape), v_state.reshape(state_shape))
```

```python
import functools

import jax
import jax.numpy as jnp
from jax import lax
from jax.experimental import pallas as pl
from jax.experimental.pallas import tpu as pltpu

N_HEADS = 16
N_KV_HEADS = 4
GROUP = N_HEADS // N_KV_HEADS
HEAD_DIM = 64
GRID_W = 64
WINDOW = 128
ROPE_THETA = 10000.0
N_MOD = 6
TOP_K = 2
EPS = 1e-6
NEG_INF = -1e30
ATTN_SCALE = HEAD_DIM ** -0.5
LOG2E = 1.4426950408889634
V_WIDTH = 2 * HEAD_DIM

LANES = 128
SUBLANES = 8
MXU_DIM = 256
MOD_ROWS = 16
Q_TILE = 256
GLOBAL_Q_TILE = 1024
WINDOW_Q_TILE = 256
VMEM_LIMIT = 56 * 1024 * 1024

F32 = jnp.float32
BF16 = jnp.bfloat16
NT_DIMS = (((1,), (1,)), ((), ()))


def _params(sem, vmem=VMEM_LIMIT):
    return pltpu.CompilerParams(dimension_semantics=sem, vmem_limit_bytes=vmem)


def _pick_tile(n, candidates):
    for c in candidates:
        if n % c == 0:
            return c
    return n


def _ada_kernel(c_ref, w_ref, b_ref, o_ref):
    c = c_ref[...]
    a = c / (1.0 + jnp.exp(-c))
    o_ref[...] = jnp.dot(a, w_ref[...], preferred_element_type=F32,
                         precision=lax.Precision.HIGHEST) + b_ref[...]


def _ada_mods(cvecs, w_ada, b_ada):
    depth, d, n = w_ada.shape
    tn = _pick_tile(n, (1536, 1024, 512, 256, 128))
    return pl.pallas_call(
        _ada_kernel,
        grid=(depth, n // tn),
        in_specs=[
            pl.BlockSpec((MOD_ROWS, d), lambda l, j: (0, 0)),
            pl.BlockSpec((None, d, tn), lambda l, j: (l, 0, j)),
            pl.BlockSpec((None, 1, tn), lambda l, j: (l, 0, j)),
        ],
        out_specs=pl.BlockSpec((None, MOD_ROWS, tn), lambda l, j: (l, 0, j)),
        out_shape=jax.ShapeDtypeStruct((depth, MOD_ROWS, n), F32),
        compiler_params=_params(("arbitrary", "arbitrary")),
    )(cvecs, w_ada, b_ada.reshape(depth, 1, n))


def _norm_mod(x, g, shift, scale):
    ms = jnp.mean(x * x, axis=-1, keepdims=True)
    y = (x * lax.rsqrt(ms + EPS)) * g
    return y * (1.0 + scale) + shift


class _Tiling:
    def __init__(self, tm, n_ctx_tokens, lat_seq, n_lat_seq, d):
        self.tm = tm
        self.n_ctx_tiles = n_ctx_tokens // tm
        self.tiles_per_seq = lat_seq // tm
        self.ctx_row = n_lat_seq
        self.d = d

    def mod_spec(self, layer, which):
        def index(i, *_):
            row = jnp.where(i < self.n_ctx_tiles, self.ctx_row, (i - self.n_ctx_tiles) // self.tiles_per_seq)
            return (layer, row * N_MOD + which, 0, 0)
        return pl.BlockSpec((None, None, 1, self.d), index)


def _layer_vec_spec(layer, d):
    return pl.BlockSpec((None, 1, d), lambda *_: (layer, 0, 0))


def _qkv_kernel(*refs, use_a, d_model, n_ctx_tiles, split_input):
    if split_input:
        (xc_ref, xl_ref, sh_ref, sc_ref, g_ref, w_ref, qn_ref, kn_ref, cos_ref, sin_ref, ones_ref, _, _,
         q_ref, kf_ref, vf_ref, kh_ref, vh_ref, x_ref, h_ref, y_ref, ss_ref) = refs
    else:
        (x_ref, sh_ref, sc_ref, g_ref, w_ref, qn_ref, kn_ref, cos_ref, sin_ref, ones_ref, _, _,
         q_ref, kf_ref, vf_ref, kh_ref, vh_ref, h_ref, y_ref, ss_ref) = refs
    q_dim = d_model
    kv_dim = N_KV_HEADS * HEAD_DIM
    is_ctx = pl.program_id(0) < n_ctx_tiles
    if split_input:
        @pl.when(is_ctx)
        def _():
            x_ref[...] = xc_ref[...]

        @pl.when(jnp.logical_not(is_ctx))
        def _():
            x_ref[...] = xl_ref[...]
    h_ref[...] = _norm_mod(x_ref[...], g_ref[...], sh_ref[...], sc_ref[...]).astype(BF16)

    cos = cos_ref[...]
    sin = sin_ref[...]
    even = (lax.broadcasted_iota(jnp.int32, cos.shape, 1) % 2) == 0

    for c0 in range(0, q_dim + 2 * kv_dim, MXU_DIM):
        y_ref[:, c0:c0 + MXU_DIM] = jnp.dot(h_ref[...], w_ref[:, c0:c0 + MXU_DIM], preferred_element_type=F32)
    if use_a:
        for c0 in range(0, q_dim + kv_dim, MXU_DIM):
            yc = y_ref[:, c0:c0 + MXU_DIM]
            ss_ref[:, c0:c0 + MXU_DIM] = jnp.dot((yc * yc).astype(BF16), ones_ref[...], preferred_element_type=F32)

    def project(col0, gain):
        xc = y_ref[:, col0:col0 + MXU_DIM]
        if use_a and gain is not None:
            ss = ss_ref[:, col0:col0 + MXU_DIM]
            xc = xc * lax.rsqrt(ss * (1.0 / HEAD_DIM) + EPS) * gain
        return xc

    def rope(xc):
        cols = []
        for l0 in range(0, MXU_DIM, LANES):
            xl = xc[:, l0:l0 + LANES]
            swapped = jnp.where(even, pltpu.roll(xl, LANES - 1, 1), pltpu.roll(xl, 1, 1))
            cols.append(xl * cos + swapped * sin)
        return jnp.concatenate(cols, axis=-1)

    for c0 in range(0, q_dim, MXU_DIM):
        qc = rope(project(c0, qn_ref[:, c0:c0 + MXU_DIM]))
        q_ref[:, c0:c0 + MXU_DIM] = (qc * (ATTN_SCALE * LOG2E)).astype(BF16)
    k = rope(project(q_dim, kn_ref[...]))
    v = project(q_dim + kv_dim, None)
    lane = lax.broadcasted_iota(jnp.int32, cos.shape, 1)
    ones_col = jnp.where(lane == HEAD_DIM, 1.0, 0.0)
    for hh in range(N_KV_HEADS):
        kh_ref[hh] = k[:, hh * HEAD_DIM:(hh + 1) * HEAD_DIM].astype(BF16)
        pair = v[:, (hh // 2) * LANES:(hh // 2 + 1) * LANES]
        if hh % 2:
            pair = pltpu.roll(pair, HEAD_DIM, 1)
        vh_ref[hh] = jnp.where(lane < HEAD_DIM, pair, ones_col).astype(BF16)

    @pl.when(is_ctx)
    def _():
        kf_ref[...] = k.reshape(kf_ref.shape)
        vf_ref[...] = v.reshape(vf_ref.shape)


def _qkv(x, mods, g_all, w_all, qn, kn, cos, sin, ones, k_state, v_state, *, layer, use_a, til):
    split_input = isinstance(x, tuple)
    seqs_per_tile = til.tm // k_state.shape[2]
    state_block = (seqs_per_tile, None) + k_state.shape[2:]
    d = til.d
    t = sum(part.shape[0] for part in x) if split_input else x.shape[0]
    tm = til.tm
    kv_dim = N_KV_HEADS * HEAD_DIM
    n_out = w_all.shape[2]
    n_ctx_tiles = til.n_ctx_tiles
    row = lambda i: (i, 0)
    fixed = lambda i: (0, 0)
    ctx_row = lambda i: (jnp.minimum(i, n_ctx_tiles - 1), 0)
    lat_row = lambda i: (jnp.maximum(i - n_ctx_tiles, 0), 0)
    state_spec = pl.BlockSpec(state_block, lambda i: (jnp.minimum(i, n_ctx_tiles - 1), layer, 0, 0))
    if split_input:
        x_specs = [pl.BlockSpec((tm, d), ctx_row), pl.BlockSpec((tm, d), lat_row)]
        x_args = list(x)
        extra_out_specs = [pl.BlockSpec((tm, d), row)]
        extra_out_shape = [jax.ShapeDtypeStruct((t, d), F32)]
    else:
        x_specs = [pl.BlockSpec((tm, d), row)]
        x_args = [x]
        extra_out_specs = []
        extra_out_shape = []
    return pl.pallas_call(
        functools.partial(_qkv_kernel, use_a=use_a, d_model=d, n_ctx_tiles=n_ctx_tiles, split_input=split_input),
        grid=(t // tm,),
        in_specs=[
            *x_specs,
            til.mod_spec(layer, 0),
            til.mod_spec(layer, 1),
            _layer_vec_spec(layer, d),
            pl.BlockSpec((None, d, n_out), lambda i: (layer, 0, 0)),
            pl.BlockSpec((1, d), fixed),
            pl.BlockSpec((1, kv_dim), fixed),
            pl.BlockSpec((tm, LANES), row),
            pl.BlockSpec((tm, LANES), row),
            pl.BlockSpec((MXU_DIM, MXU_DIM), fixed),
            pl.BlockSpec(memory_space=pl.ANY),
            pl.BlockSpec(memory_space=pl.ANY),
        ],
        out_specs=[
            pl.BlockSpec((tm, d), row),
            state_spec,
            state_spec,
            pl.BlockSpec((N_KV_HEADS, tm, HEAD_DIM), lambda i: (0, i, 0)),
            pl.BlockSpec((N_KV_HEADS, tm, V_WIDTH), lambda i: (0, i, 0)),
            *extra_out_specs,
        ],
        out_shape=[
            jax.ShapeDtypeStruct((t, d), BF16),
            jax.ShapeDtypeStruct(k_state.shape, F32),
            jax.ShapeDtypeStruct(v_state.shape, F32),
            jax.ShapeDtypeStruct((N_KV_HEADS, t, HEAD_DIM), BF16),
            jax.ShapeDtypeStruct((N_KV_HEADS, t, V_WIDTH), BF16),
            *extra_out_shape,
        ],
        scratch_shapes=[pltpu.VMEM((tm, d), BF16), pltpu.VMEM((tm, n_out), F32),
                        pltpu.VMEM((tm, d + kv_dim), F32)],
        input_output_aliases={len(x_args) + 9: 1, len(x_args) + 10: 2},
        compiler_params=_params(("arbitrary",)),
    )(*x_args, mods, mods, g_all, w_all, qn, kn, cos, sin, ones, k_state, v_state)


def _softmax_pv(score_parts, value_parts, sink):
    m = score_parts[0].max(axis=-1, keepdims=True)
    for s in score_parts[1:]:
        m = jnp.maximum(m, s.max(axis=-1, keepdims=True))
    acc = None
    for s, v in zip(score_parts, value_parts):
        pv = jnp.dot(jnp.exp2(s - m).astype(BF16), v, preferred_element_type=F32)
        acc = pv if acc is None else acc + pv
    denom = acc[:, HEAD_DIM:HEAD_DIM + 1]
    if sink is not None:
        denom = denom + jnp.exp2(sink - m)
    return acc[:, :HEAD_DIM] / denom


def _attn_ctx_kernel(sink_ref, q_ref, k_ref, v_ref, o_ref, s_ref, p_ref, m_ref, *, has_sink):
    def cols(head):
        return slice(head * HEAD_DIM, (head + 1) * HEAD_DIM)

    def scores(head):
        s_ref[head] = lax.dot_general(q_ref[:, cols(head)], k_ref[head // GROUP], NT_DIMS,
                                      preferred_element_type=F32)

    def numerators(head):
        s = s_ref[head]
        m = s.max(axis=-1, keepdims=True)
        if has_sink:
            m = jnp.maximum(m, sink_ref[head])
            m_ref[head] = jnp.broadcast_to(jnp.exp2(sink_ref[head] - m), m_ref.shape[1:])
        p_ref[head] = jnp.exp2(s - m).astype(BF16)

    def outputs(head):
        acc = jnp.dot(p_ref[head], v_ref[head // GROUP], preferred_element_type=F32)
        denom = acc[:, HEAD_DIM:HEAD_DIM + 1]
        if has_sink:
            denom = denom + m_ref[head][:, :1]
        o_ref[:, cols(head)] = (acc[:, :HEAD_DIM] / denom).astype(o_ref.dtype)

    def fused_outputs(head):
        sink = sink_ref[head] if has_sink else None
        o_ref[:, cols(head)] = _softmax_pv([s_ref[head]], [v_ref[head // GROUP]], sink).astype(o_ref.dtype)

    if has_sink:
        _phased(N_HEADS, (scores, fused_outputs))
    else:
        _phased(N_HEADS, (scores, numerators, outputs))


def _attn_ctx(sink, q, kh, vh, *, n_ctx_seq, seq, has_sink):
    t, d = q.shape
    return pl.pallas_call(
        functools.partial(_attn_ctx_kernel, has_sink=has_sink),
        grid=(n_ctx_seq,),
        in_specs=[
            pl.BlockSpec(memory_space=pltpu.SMEM),
            pl.BlockSpec((seq, d), lambda b: (b, 0)),
            pl.BlockSpec((N_KV_HEADS, seq, HEAD_DIM), lambda b: (0, b, 0)),
            pl.BlockSpec((N_KV_HEADS, seq, V_WIDTH), lambda b: (0, b, 0)),
        ],
        out_specs=pl.BlockSpec((seq, d), lambda b: (b, 0)),
        out_shape=jax.ShapeDtypeStruct((n_ctx_seq * seq, d), BF16),
        scratch_shapes=[pltpu.VMEM((N_HEADS, seq, seq), F32), pltpu.VMEM((N_HEADS, seq, seq), BF16),
                        pltpu.VMEM((N_HEADS, seq, LANES), F32)],
        compiler_params=_params(("arbitrary",)),
    )(sink, q, kh, vh)


def _phased(n, stages):
    for stage in stages:
        for j in range(n):
            stage(j)


def _attn_global_kernel(q_ref, k_ref, v_ref, kc_ref, vc_ref, o_ref, sl_ref, sc_ref):
    def cols(j):
        return slice(j * HEAD_DIM, (j + 1) * HEAD_DIM)

    def scores(j):
        qj = q_ref[:, cols(j)]
        sl_ref[j] = lax.dot_general(qj, k_ref[...], NT_DIMS, preferred_element_type=F32)
        sc_ref[j] = lax.dot_general(qj, kc_ref[...], NT_DIMS, preferred_element_type=F32)

    def outputs(j):
        o = _softmax_pv([sl_ref[j], sc_ref[j]], [v_ref[...], vc_ref[...]], None)
        o_ref[:, cols(j)] = o.astype(o_ref.dtype)

    _phased(GROUP, (scores, outputs))


def _cache_spec(layer, past, width):
    return pl.BlockSpec((None, None, past, width), lambda b, g, i: (layer, g, b, 0))


def _attn_global(q, kh, vh, kc_all, vc_all, *, layer, n_ctx_tokens, n_lat_seq, lat_seq, past):
    t, d = q.shape
    gw = GROUP * HEAD_DIM
    tq = GLOBAL_Q_TILE if (lat_seq % GLOBAL_Q_TILE == 0 and n_ctx_tokens % GLOBAL_Q_TILE == 0) else Q_TILE
    q_blocks = lat_seq // tq
    q0 = n_ctx_tokens // tq
    s0 = n_ctx_tokens // lat_seq
    return pl.pallas_call(
        _attn_global_kernel,
        grid=(n_lat_seq, N_KV_HEADS, q_blocks),
        in_specs=[
            pl.BlockSpec((tq, gw), lambda b, g, i: (q0 + b * q_blocks + i, g)),
            pl.BlockSpec((None, lat_seq, HEAD_DIM), lambda b, g, i: (g, s0 + b, 0)),
            pl.BlockSpec((None, lat_seq, V_WIDTH), lambda b, g, i: (g, s0 + b, 0)),
            _cache_spec(layer, past, HEAD_DIM),
            _cache_spec(layer, past, V_WIDTH),
        ],
        out_specs=pl.BlockSpec((tq, gw), lambda b, g, i: (b * q_blocks + i, g)),
        out_shape=jax.ShapeDtypeStruct((n_lat_seq * lat_seq, d), BF16),
        scratch_shapes=[pltpu.VMEM((GROUP, tq, lat_seq), F32), pltpu.VMEM((GROUP, tq, past), F32)],
        compiler_params=_params(("arbitrary", "arbitrary", "arbitrary")),
    )(q, kh, vh, kc_all, vc_all)


def _attn_window_kernel(sink_ref, q_ref, kp_ref, kcur_ref, kn_ref, vp_ref, vcur_ref, vn_ref,
                        kc_ref, vc_ref, o_ref, sl_ref, sc_ref, *, lat_seq, tq):
    start = pl.program_id(1) * tq
    span = tq + 2 * WINDOW
    qpos = start + lax.broadcasted_iota(jnp.int32, (tq, span), 0)
    kpos = start - WINDOW + lax.broadcasted_iota(jnp.int32, (tq, span), 1)
    valid = (kpos >= 0) & (kpos < lat_seq) & (jnp.abs(qpos - kpos) <= WINDOW)

    def cols(head):
        return slice(head * HEAD_DIM, (head + 1) * HEAD_DIM)

    def band(prev_ref, cur_ref, next_ref, g):
        return jnp.concatenate([prev_ref[g], cur_ref[g], next_ref[g]], axis=0)

    def scores(head):
        g = head // GROUP
        qj = q_ref[:, cols(head)]
        s_loc = lax.dot_general(qj, band(kp_ref, kcur_ref, kn_ref, g), NT_DIMS, preferred_element_type=F32)
        sl_ref[head] = jnp.where(valid, s_loc, NEG_INF)
        sc_ref[head] = lax.dot_general(qj, kc_ref[g], NT_DIMS, preferred_element_type=F32)

    def outputs(head):
        g = head // GROUP
        o = _softmax_pv([sl_ref[head], sc_ref[head]], [band(vp_ref, vcur_ref, vn_ref, g), vc_ref[g]],
                        sink_ref[head])
        o_ref[:, cols(head)] = o.astype(o_ref.dtype)

    _phased(N_HEADS, (scores, outputs))


def _attn_window(sink, q, kh, vh, kc_all, vc_all, *, layer, n_ctx_tokens, n_lat_seq, lat_seq, past):
    t, d = q.shape
    gw = GROUP * HEAD_DIM
    tq = WINDOW_Q_TILE if (lat_seq % WINDOW_Q_TILE == 0 and n_ctx_tokens % WINDOW_Q_TILE == 0) else Q_TILE
    q_blocks = lat_seq // tq
    q0 = n_ctx_tokens // tq
    span = tq + 2 * WINDOW
    half = tq // WINDOW
    w0 = n_ctx_tokens // WINDOW
    w_blocks = lat_seq // WINDOW
    prev = lambda b, i: (0, w0 + b * w_blocks + jnp.maximum(i * half - 1, 0), 0)
    cur = lambda b, i: (0, q0 + b * q_blocks + i, 0)
    nxt = lambda b, i: (0, w0 + b * w_blocks + jnp.minimum((i + 1) * half, w_blocks - 1), 0)

    def band_specs(width):
        return [
            pl.BlockSpec((N_KV_HEADS, WINDOW, width), prev),
            pl.BlockSpec((N_KV_HEADS, tq, width), cur),
            pl.BlockSpec((N_KV_HEADS, WINDOW, width), nxt),
        ]

    def cache_spec(width):
        return pl.BlockSpec((None, N_KV_HEADS, past, width), lambda b, i: (layer, 0, b, 0))

    return pl.pallas_call(
        functools.partial(_attn_window_kernel, lat_seq=lat_seq, tq=tq),
        grid=(n_lat_seq, q_blocks),
        in_specs=[
            pl.BlockSpec(memory_space=pltpu.SMEM),
            pl.BlockSpec((tq, d), lambda b, i: (q0 + b * q_blocks + i, 0)),
            *band_specs(HEAD_DIM), *band_specs(V_WIDTH),
            cache_spec(HEAD_DIM),
            cache_spec(V_WIDTH),
        ],
        out_specs=pl.BlockSpec((tq, d), lambda b, i: (b * q_blocks + i, 0)),
        out_shape=jax.ShapeDtypeStruct((n_lat_seq * lat_seq, d), BF16),
        scratch_shapes=[pltpu.VMEM((N_HEADS, tq, span), F32), pltpu.VMEM((N_HEADS, tq, past), F32)],
        compiler_params=_params(("arbitrary", "arbitrary")),
    )(sink, q, kh, kh, kh, vh, vh, vh, kc_all, vc_all)


def _attn_residual(oc_ref, ol_ref, wo_ref, x_ref, gate_ref, dst_ref, n_ctx_tiles):
    def project(o_ref):
        y = jnp.dot(o_ref[...], wo_ref[...], preferred_element_type=F32)
        dst_ref[...] = x_ref[...] + gate_ref[...] * y

    @pl.when(pl.program_id(0) < n_ctx_tiles)
    def _():
        project(oc_ref)

    @pl.when(pl.program_id(0) >= n_ctx_tiles)
    def _():
        project(ol_ref)


def _attn_residual_specs(layer, til):
    tm, d, n_ctx_tiles = til.tm, til.d, til.n_ctx_tiles
    return [
        pl.BlockSpec((tm, d), lambda i, *_: (jnp.minimum(i, n_ctx_tiles - 1), 0)),
        pl.BlockSpec((tm, d), lambda i, *_: (jnp.maximum(i - n_ctx_tiles, 0), 0)),
        pl.BlockSpec((None, d, d), lambda i, *_: (layer, 0, 0), pipeline_mode=pl.Buffered(1)),
        pl.BlockSpec((tm, d), lambda i, *_: (i, 0)),
        til.mod_spec(layer, 2),
    ]


def _swiglu_act(h, wg, wu):
    g = jnp.dot(h, wg, preferred_element_type=F32)
    u = jnp.dot(h, wu, preferred_element_type=F32)
    return (g / (1.0 + jnp.exp(-g))) * u


def _ffn_kernel(oc_ref, ol_ref, wo_ref, x_ref, g1_ref, sh_ref, sc_ref, gate_ref, g_ref, wg_ref, wu_ref, wd_ref,
                out_ref, x1_ref, *, n_ctx_tiles):
    _attn_residual(oc_ref, ol_ref, wo_ref, x_ref, g1_ref, x1_ref, n_ctx_tiles)
    x = x1_ref[...]
    h = _norm_mod(x, g_ref[...], sh_ref[...], sc_ref[...]).astype(BF16)
    a = _swiglu_act(h, wg_ref[...], wu_ref[...])
    f = jnp.dot(a.astype(BF16), wd_ref[...], preferred_element_type=F32)
    out_ref[...] = x + gate_ref[...] * f


def _ffn(o_ctx, o_lat, w_o_all, x, mods, g_all, w_up_all, w_down_all, *, layer, idx, til):
    t, d = x.shape
    tm = til.tm
    d_ff = w_down_all.shape[1]
    row = lambda i: (i, 0)
    resident = pl.Buffered(1)
    return pl.pallas_call(
        functools.partial(_ffn_kernel, n_ctx_tiles=til.n_ctx_tiles),
        grid=(t // tm,),
        in_specs=[
            *_attn_residual_specs(layer, til),
            til.mod_spec(layer, 3),
            til.mod_spec(layer, 4),
            til.mod_spec(layer, 5),
            _layer_vec_spec(layer, d),
            pl.BlockSpec((None, d, d_ff), lambda i: (idx, 0, 0), pipeline_mode=resident),
            pl.BlockSpec((None, d, d_ff), lambda i: (idx, 0, 1), pipeline_mode=resident),
            pl.BlockSpec((None, d_ff, d), lambda i: (idx, 0, 0), pipeline_mode=resident),
        ],
        out_specs=pl.BlockSpec((tm, d), row),
        out_shape=jax.ShapeDtypeStruct((t, d), F32),
        scratch_shapes=[pltpu.VMEM((tm, d), F32)],
        compiler_params=_params(("arbitrary",)),
    )(o_ctx, o_lat, w_o_all, x, mods, mods, mods, mods, g_all, w_up_all, w_up_all, w_down_all)


META_IDX, META_GATE, META_RANK = 0, 2, 4


def _router_kernel(oc_ref, ol_ref, wo_ref, x_ref, g1_ref, sh_ref, sc_ref, g_ref, wr_ref, tri_ref,
                   x1_ref, meta_ref, meta_t_ref, cnt_ref, run_ref, *, n_experts, n_ctx_tiles):
    @pl.when(pl.program_id(0) == 0)
    def _():
        run_ref[...] = jnp.zeros_like(run_ref)

    _attn_residual(oc_ref, ol_ref, wo_ref, x_ref, g1_ref, x1_ref, n_ctx_tiles)
    h = _norm_mod(x1_ref[...], g_ref[...], sh_ref[...], sc_ref[...])
    logits = jnp.dot(h, wr_ref[...], preferred_element_type=F32, precision=lax.Precision.HIGHEST)
    lane = lax.broadcasted_iota(jnp.int32, logits.shape, 1).astype(F32)
    logits = jnp.where(lane < n_experts, logits, -jnp.inf)
    top1 = logits.max(axis=-1, keepdims=True)
    idx1 = jnp.where(logits == top1, lane, float(LANES)).min(axis=-1, keepdims=True)
    rest = jnp.where(lane == idx1, -jnp.inf, logits)
    top2 = rest.max(axis=-1, keepdims=True)
    idx2 = jnp.where(rest == top2, lane, float(LANES)).min(axis=-1, keepdims=True)
    e = jnp.exp(top2 - top1)
    g1 = 1.0 / (1.0 + e)
    g2 = e / (1.0 + e)

    sel1 = lane == idx1
    sel2 = lane == idx2
    sel = jnp.where(sel1, 1.0, 0.0) + jnp.where(sel2, 1.0, 0.0)
    before = jnp.dot(tri_ref[...], sel.astype(BF16), preferred_element_type=F32) + run_ref[...]
    r1 = jnp.where(sel1, before, 0.0).sum(axis=-1, keepdims=True)
    r2 = jnp.where(sel2, before, 0.0).sum(axis=-1, keepdims=True)
    run_ref[...] += sel.sum(axis=0, keepdims=True)
    cnt_ref[...] = run_ref[...]

    meta = jnp.zeros_like(logits)
    for off, (a, b) in ((META_IDX, (idx1, idx2)), (META_GATE, (g1, g2)), (META_RANK, (r1, r2))):
        meta = jnp.where(lane == off, a, meta)
        meta = jnp.where(lane == off + 1, b, meta)
    meta_ref[...] = meta
    meta_t_ref[...] = meta.T[:SUBLANES]


def _router(o_ctx, o_lat, w_o_all, x, mods, g_all, wr, tri, *, layer, n_experts, til):
    t, d = x.shape
    tm = til.tm
    row = lambda i: (i, 0)
    fixed = lambda i: (0, 0)
    return pl.pallas_call(
        functools.partial(_router_kernel, n_experts=n_experts, n_ctx_tiles=til.n_ctx_tiles),
        grid=(t // tm,),
        in_specs=[
            *_attn_residual_specs(layer, til),
            til.mod_spec(layer, 3),
            til.mod_spec(layer, 4),
            _layer_vec_spec(layer, d),
            pl.BlockSpec((d, LANES), fixed),
            pl.BlockSpec((tm, tm), fixed),
        ],
        out_specs=[pl.BlockSpec((tm, d), row), pl.BlockSpec((tm, LANES), row),
                   pl.BlockSpec((SUBLANES, tm), lambda i: (0, i)), pl.BlockSpec((1, LANES), fixed)],
        out_shape=[jax.ShapeDtypeStruct((t, d), F32), jax.ShapeDtypeStruct((t, LANES), F32),
                   jax.ShapeDtypeStruct((SUBLANES, t), F32), jax.ShapeDtypeStruct((1, LANES), F32)],
        scratch_shapes=[pltpu.VMEM((1, LANES), F32)],
        compiler_params=_params(("arbitrary",)),
    )(o_ctx, o_lat, w_o_all, x, mods, mods, mods, g_all, wr, tri)


def _row_copy(src_ref, src_row, dst_ref, dst_row, sem):
    return pltpu.make_async_copy(src_ref.at[pl.ds(src_row, 1)], dst_ref.at[pl.ds(dst_row, 1)], sem)


def _dispatch_kernel(pos_ref, x_ref, sh_ref, sc_ref, g_ref, hs_init_ref, hs_ref, h_ref, sem, *, tm, n_tokens):
    del hs_init_ref
    i = pl.program_id(0)
    slot = i % 2

    def wait_slot(s):
        for _ in range(TOP_K * tm):
            _row_copy(h_ref.at[s], 0, hs_ref, 0, sem.at[s]).wait()

    @pl.when(i >= 2)
    def _():
        wait_slot(slot)

    h_ref[slot] = _norm_mod(x_ref[...], g_ref[...], sh_ref[...], sc_ref[...])

    for r in range(tm):
        for k in range(TOP_K):
            dst = pos_ref[k * n_tokens + i * tm + r]
            _row_copy(h_ref.at[slot], r, hs_ref, dst, sem.at[slot]).start(priority=k)

    @pl.when(i == pl.num_programs(0) - 1)
    def _():
        wait_slot(slot)

        @pl.when(i >= 1)
        def _():
            wait_slot(1 - slot)


def _dispatch(pos, x, mods, g_all, hs_init, *, layer, til):
    t, d = x.shape
    tm = til.tm
    return pl.pallas_call(
        functools.partial(_dispatch_kernel, tm=tm, n_tokens=t),
        grid_spec=pltpu.PrefetchScalarGridSpec(
            num_scalar_prefetch=1,
            grid=(t // tm,),
            in_specs=[
                pl.BlockSpec((tm, d), lambda i, pos: (i, 0)),
                til.mod_spec(layer, 3),
                til.mod_spec(layer, 4),
                _layer_vec_spec(layer, d),
                pl.BlockSpec(memory_space=pl.ANY),
            ],
            out_specs=pl.BlockSpec(memory_space=pl.ANY),
            scratch_shapes=[pltpu.VMEM((2, tm, d), F32), pltpu.SemaphoreType.DMA((2,))],
        ),
        out_shape=jax.ShapeDtypeStruct(hs_init.shape, F32),
        input_output_aliases={5: 0},
        compiler_params=_params(("arbitrary",)),
    )(pos, x, mods, mods, g_all, hs_init)


def _expert_kernel(te_ref, tv_ref, hs_ref, wg_ref, wu_ref, wd_ref, ys_ref):
    del te_ref

    @pl.when(tv_ref[pl.program_id(0)] != 0)
    def _():
        a = _swiglu_act(hs_ref[...].astype(BF16), wg_ref[...], wu_ref[...])
        ys_ref[...] = jnp.dot(a.astype(BF16), wd_ref[...], preferred_element_type=F32)

    @pl.when(tv_ref[pl.program_id(0)] == 0)
    def _():
        ys_ref[...] = jnp.zeros_like(ys_ref)


def _experts(tile_expert, tile_valid, hs, w_up_all, w_down_all, *, idx, tm):
    p, d = hs.shape
    d_ff = w_down_all.shape[2]
    resident = pl.Buffered(1)
    return pl.pallas_call(
        _expert_kernel,
        grid_spec=pltpu.PrefetchScalarGridSpec(
            num_scalar_prefetch=2,
            grid=(p // tm,),
            in_specs=[
                pl.BlockSpec((tm, d), lambda r, te, tv: (r, 0)),
                pl.BlockSpec((None, None, d, d_ff), lambda r, te, tv: (idx, te[r], 0, 0), pipeline_mode=resident),
                pl.BlockSpec((None, None, d, d_ff), lambda r, te, tv: (idx, te[r], 0, 1), pipeline_mode=resident),
                pl.BlockSpec((None, None, d_ff, d), lambda r, te, tv: (idx, te[r], 0, 0), pipeline_mode=resident),
            ],
            out_specs=pl.BlockSpec((tm, d), lambda r, te, tv: (r, 0)),
        ),
        out_shape=jax.ShapeDtypeStruct((p, d), F32),
        compiler_params=_params(("arbitrary",)),
    )(tile_expert, tile_valid, hs, w_up_all, w_up_all, w_down_all)


def _combine_kernel(pos_ref, ys_ref, meta_ref, x_ref, gate_ref, *rest, tm, n_tokens, n_ctx_tiles, final):
    if final:
        gf_ref, yc_ref, yl_ref, y_ref, sem = rest
    else:
        out_ref, y_ref, sem = rest
    i = pl.program_id(0)
    slot = i % 2

    def fetch(step, s):
        for r in range(tm):
            for k in range(TOP_K):
                src = pos_ref[k * n_tokens + step * tm + r]
                _row_copy(ys_ref, src, y_ref.at[s, k], r, sem.at[s]).start(priority=k)

    @pl.when(i == 0)
    def _():
        fetch(0, 0)

    @pl.when(i + 1 < pl.num_programs(0))
    def _():
        fetch(i + 1, 1 - slot)

    for _ in range(tm):
        for k in range(TOP_K):
            _row_copy(ys_ref, 0, y_ref.at[slot, k], 0, sem.at[slot]).wait()

    meta = meta_ref[...]
    f = meta[:, META_GATE:META_GATE + 1] * y_ref[slot, 0] + meta[:, META_GATE + 1:META_GATE + 2] * y_ref[slot, 1]
    x_new = x_ref[...] + gate_ref[...] * f
    if not final:
        out_ref[...] = x_new
        return

    ms = jnp.mean(x_new * x_new, axis=-1, keepdims=True)
    y = (x_new * lax.rsqrt(ms + EPS)) * gf_ref[...]

    @pl.when(i < n_ctx_tiles)
    def _():
        yc_ref[...] = y

    @pl.when(i >= n_ctx_tiles)
    def _():
        yl_ref[...] = y


def _combine(pos, ys, meta, x, mods, final_gain, *, layer, til):
    t, d = x.shape
    tm = til.tm
    n_ctx_tiles = til.n_ctx_tiles
    final = final_gain is not None
    row = lambda i, pos: (i, 0)
    in_specs = [
        pl.BlockSpec(memory_space=pl.ANY),
        pl.BlockSpec((tm, LANES), row),
        pl.BlockSpec((tm, d), row),
        til.mod_spec(layer, 5),
    ]
    args = [pos, ys, meta, x, mods]
    if final:
        in_specs.append(pl.BlockSpec((1, d), lambda i, pos: (0, 0)))
        args.append(final_gain)
        out_specs = [pl.BlockSpec((tm, d), lambda i, pos: (jnp.minimum(i, n_ctx_tiles - 1), 0)),
                     pl.BlockSpec((tm, d), lambda i, pos: (jnp.maximum(i - n_ctx_tiles, 0), 0))]
        out_shape = [jax.ShapeDtypeStruct((n_ctx_tiles * tm, d), F32),
                     jax.ShapeDtypeStruct((t - n_ctx_tiles * tm, d), F32)]
    else:
        out_specs = pl.BlockSpec((tm, d), row)
        out_shape = jax.ShapeDtypeStruct((t, d), F32)
    return pl.pallas_call(
        functools.partial(_combine_kernel, tm=tm, n_tokens=t, n_ctx_tiles=n_ctx_tiles, final=final),
        grid_spec=pltpu.PrefetchScalarGridSpec(
            num_scalar_prefetch=1,
            grid=(t // tm,),
            in_specs=in_specs,
            out_specs=out_specs,
            scratch_shapes=[pltpu.VMEM((2, TOP_K, tm, d), F32), pltpu.SemaphoreType.DMA((2,))],
        ),
        out_shape=out_shape,
        compiler_params=_params(("arbitrary",)),
    )(*args)


def _moe(o_ctx, o_lat, w_o_all, x, mods, g_all, w_router, w_up_all, w_down_all, final_gain, sorted_buf,
         *, layer, idx, til):
    t, d = x.shape
    n_experts = w_router.shape[-1]
    tm = til.tm
    wr = jnp.zeros((d, LANES), F32).at[:, :n_experts].set(w_router)
    tri = (jnp.arange(tm)[:, None] > jnp.arange(tm)[None, :]).astype(BF16)
    x, meta, meta_t, counts = _router(o_ctx, o_lat, w_o_all, x, mods, g_all, wr, tri, layer=layer,
                                      n_experts=n_experts, til=til)

    counts = counts[0, :n_experts].astype(jnp.int32)
    padded = ((counts + tm - 1) // tm) * tm
    ends = jnp.cumsum(padded)
    starts = ends - padded
    choice = meta_t[META_IDX:META_IDX + TOP_K].astype(jnp.int32)
    rank = meta_t[META_RANK:META_RANK + TOP_K].astype(jnp.int32)
    group_start = sum(jnp.where(choice == e, starts[e], 0) for e in range(n_experts))
    pos = (group_start + rank).reshape(TOP_K * t)
    n_tiles = (t * TOP_K) // tm + n_experts
    tile_start = jnp.arange(n_tiles, dtype=jnp.int32) * tm
    tile_expert = jnp.minimum((tile_start[:, None] >= ends[None, :]).sum(axis=-1), n_experts - 1).astype(jnp.int32)
    tile_valid = (tile_start < ends[-1]).astype(jnp.int32)

    if sorted_buf is None:
        sorted_buf = jnp.zeros((n_tiles * tm, d), F32)
    hs = _dispatch(pos, x, mods, g_all, sorted_buf, layer=layer, til=til)
    ys = _experts(tile_expert, tile_valid, hs, w_up_all, w_down_all, idx=idx, tm=tm)
    return _combine(pos, ys, meta, x, mods, final_gain, layer=layer, til=til), hs


def _final_norm_kernel(x_ref, g_ref, o_ref):
    x = x_ref[...]
    ms = jnp.mean(x * x, axis=-1, keepdims=True)
    o_ref[...] = (x * lax.rsqrt(ms + EPS)) * g_ref[...]


def _final_norm(x, g, *, tm, first_tile, n_tiles):
    d = x.shape[1]
    return pl.pallas_call(
        _final_norm_kernel,
        grid=(n_tiles,),
        in_specs=[pl.BlockSpec((tm, d), lambda i: (first_tile + i, 0)), pl.BlockSpec((1, d), lambda i: (0, 0))],
        out_specs=pl.BlockSpec((tm, d), lambda i: (i, 0)),
        out_shape=jax.ShapeDtypeStruct((n_tiles * tm, d), F32),
        compiler_params=_params(("arbitrary",)),
    )(x, g)


def _rope_tables(n_ctx_tokens, n_lat_seq, lat_seq):
    pos = jnp.arange(lat_seq)
    rows = (pos // GRID_W).astype(F32)
    cols = (pos % GRID_W).astype(F32)
    pairs_per_axis = HEAD_DIM // 4
    inv = ROPE_THETA ** (-jnp.arange(pairs_per_axis, dtype=F32) / pairs_per_axis)
    ang = jnp.concatenate([rows[:, None] * inv, cols[:, None] * inv], axis=-1)
    cos = jnp.repeat(jnp.cos(ang), 2, axis=-1)
    sin = jnp.repeat(jnp.sin(ang), 2, axis=-1) * jnp.tile(jnp.array([-1.0, 1.0], F32), HEAD_DIM // 2)
    reps = LANES // HEAD_DIM
    cos = jnp.tile(cos, (n_lat_seq, reps))
    sin = jnp.tile(sin, (n_lat_seq, reps))
    cos = jnp.concatenate([jnp.ones((n_ctx_tokens, LANES), F32), cos], axis=0)
    sin = jnp.concatenate([jnp.zeros((n_ctx_tokens, LANES), F32), sin], axis=0)
    return cos, sin


def kernel(x_prompt, x_sample, cache_k, cache_v, c, c_ctx, w_ada, b_ada, norm_mix, norm_ffn, norm_final,
           w_qkv, w_o, q_norm, k_norm, sinks, w_ffn_up, w_ffn_down, w_router, w_exp_up, w_exp_down):
    n_ctx_seq, seq, d = x_prompt.shape
    n_lat_seq, lat_seq, _ = x_sample.shape
    depth = w_ada.shape[0]
    past = cache_k.shape[2]
    n_experts = w_router.shape[-1]
    kv_dim = N_KV_HEADS * HEAD_DIM
    n_ctx_tokens = n_ctx_seq * seq
    n_lat_tokens = n_lat_seq * lat_seq
    assert d == N_HEADS * HEAD_DIM and seq == Q_TILE and lat_seq % Q_TILE == 0 and kv_dim == MXU_DIM
    assert n_ctx_tokens % lat_seq == 0 and n_lat_seq < MOD_ROWS and n_experts <= LANES

    tm = 2 * Q_TILE if (lat_seq % (2 * Q_TILE) == 0 and n_ctx_tokens % (2 * Q_TILE) == 0) else Q_TILE
    til = _Tiling(tm, n_ctx_tokens, lat_seq, n_lat_seq, d)

    x = (x_prompt.reshape(n_ctx_tokens, d), x_sample.reshape(n_lat_tokens, d))

    cvecs = jnp.zeros((MOD_ROWS, d), F32).at[:n_lat_seq].set(c).at[n_lat_seq].set(c_ctx)
    mods = _ada_mods(cvecs, w_ada, b_ada).reshape(depth, MOD_ROWS * N_MOD, 1, d)

    cos, sin = _rope_tables(n_ctx_tokens, n_lat_seq, lat_seq)
    head_id = jnp.arange(MXU_DIM) // HEAD_DIM
    ones = (head_id[:, None] == head_id[None, :]).astype(BF16)

    kc_all = cache_k.transpose(1, 3, 0, 2, 4).reshape(depth, N_KV_HEADS, n_lat_seq * past, HEAD_DIM).astype(BF16)
    vc_all = cache_v.transpose(1, 3, 0, 2, 4).reshape(depth, N_KV_HEADS, n_lat_seq * past, HEAD_DIM).astype(BF16)
    ones_col = (jnp.arange(V_WIDTH - HEAD_DIM) == 0).astype(BF16)
    vc_all = jnp.concatenate([vc_all, jnp.broadcast_to(ones_col, vc_all.shape[:-1] + ones_col.shape)], axis=-1)

    w_qkv_b = w_qkv.astype(BF16)
    w_o_b = w_o.astype(BF16)
    w_ffn_up_b = w_ffn_up.astype(BF16)
    w_ffn_down_b = w_ffn_down.astype(BF16)
    w_exp_up_b = w_exp_up.astype(BF16)
    w_exp_down_b = w_exp_down.astype(BF16)
    g_mix = norm_mix.reshape(depth, 1, d)
    g_ffn = norm_ffn.reshape(depth, 1, d)
    gf = norm_final.reshape(1, d)

    lat = dict(n_ctx_tokens=n_ctx_tokens, n_lat_seq=n_lat_seq, lat_seq=lat_seq, past=past)

    k_state = jnp.zeros((n_ctx_seq, depth, seq, kv_dim), F32)
    v_state = jnp.zeros((n_ctx_seq, depth, seq, kv_dim), F32)
    sorted_buf = None
    for l in range(depth):
        idx = l // 2
        use_a = (l % 2 == 0)
        if use_a:
            qn = jnp.tile(q_norm[idx], d // HEAD_DIM).reshape(1, d)
            kn = jnp.tile(k_norm[idx], N_KV_HEADS).reshape(1, kv_dim)
            sink = jnp.zeros((N_HEADS,), F32)
        else:
            qn = jnp.ones((1, d), F32)
            kn = jnp.ones((1, kv_dim), F32)
            sink = sinks[idx].astype(F32) * LOG2E

        q, k_state, v_state, kh, vh, *joined = _qkv(x, mods, g_mix, w_qkv_b, qn, kn, cos, sin, ones, k_state,
                                                    v_state, layer=l, use_a=use_a, til=til)
        if joined:
            x = joined[0]

        o_ctx = _attn_ctx(sink, q, kh, vh, n_ctx_seq=n_ctx_seq, seq=seq, has_sink=not use_a)
        if use_a:
            o_lat = _attn_global(q, kh, vh, kc_all, vc_all, layer=l, **lat)
        else:
            o_lat = _attn_window(sink, q, kh, vh, kc_all, vc_all, layer=l, **lat)
        if use_a:
            x = _ffn(o_ctx, o_lat, w_o_b, x, mods, g_ffn, w_ffn_up_b, w_ffn_down_b, layer=l, idx=idx, til=til)
        else:
            x, sorted_buf = _moe(o_ctx, o_lat, w_o_b, x, mods, g_ffn, w_router[idx], w_exp_up_b, w_exp_down_b,
                                 gf if l == depth - 1 else None, sorted_buf, layer=l, idx=idx, til=til)

    if depth % 2 == 0:
        y_prompt, y_sample = x
    else:
        y_prompt = _final_norm(x, gf, tm=tm, first_tile=0, n_tiles=til.n_ctx_tiles)
        y_sample = _final_norm(x, gf, tm=tm, first_tile=til.n_ctx_tiles, n_tiles=n_lat_tokens // tm)
    y_prompt = y_prompt.reshape(n_ctx_seq, seq, d)
    y_sample = y_sample.reshape(n_lat_seq, lat_seq, d)
    state_shape = (n_ctx_seq, depth, seq, N_KV_HEADS, HEAD_DIM)
    return (y_prompt, y_sample, k_state.reshape(state_shape), v_state.reshape(state_shape))
```
